```python
import math
import jax, jax.numpy as jnp
from jax import lax
import numpy as np

D_MODEL = 1024
BATCH = 4
SEQ = 8192
DEPTH = 2

HEAD_DIM = 64
N_TOK_HEADS = 12
TOK_W = N_TOK_HEADS * HEAD_DIM
N_MEM_HEADS = 4
MEM_W = N_MEM_HEADS * HEAD_DIM
MIX_W = TOK_W + MEM_W
N_MEM = 256
Q_BLOCK = 128
DECAY_LORA = 64
AAA_LORA = 64
GATE_LORA = 128
RWKV_SHIFT_W = 3 * TOK_W + DECAY_LORA + AAA_LORA + GATE_LORA
RWKV_IN_W = RWKV_SHIFT_W + MEM_W
FOX_IN_W = 4 * TOK_W + N_TOK_HEADS + MEM_W
N_EXPERTS = 64
TOP_K = 6
D_EXPERT = 256
D_SHARED = 256
ROUTED_SCALE = 2.5
EXPERT_BLOCK = 128
N_FOX_LAYERS = (DEPTH + 1) // 2
N_RWKV_LAYERS = DEPTH // 2
DEEPNORM_ALPHA = (2 * DEPTH) ** 0.25
DEEPNORM_BETA = (8 * DEPTH) ** -0.25
LN_EPS = 1e-5
GN_EPS = 64e-5

kernel_name = 'hybrid_fox_rwkv7_memxattn_moe_deepnorm'


def layer_norm(x, g, b):
    xf = x.astype(jnp.float32)
    mu = jnp.mean(xf, axis=-1, keepdims=True)
    var = jnp.mean(jnp.square(xf - mu), axis=-1, keepdims=True)
    return ((xf - mu) * lax.rsqrt(var + LN_EPS) * g + b).astype(x.dtype)


def token_shift(h):
    return jnp.pad(h[:, :-1], ((0, 0), (1, 0), (0, 0)))


def split_heads(t, n_heads):
    return t.reshape(t.shape[0], t.shape[1], n_heads, HEAD_DIM)


def memory_cross_attention(q_mem, k_mem, v_mem):
    B_, S_, _ = q_mem.shape
    q = split_heads(q_mem, N_MEM_HEADS)
    s = jnp.einsum('bshd,bmhd->bhsm', q, k_mem).astype(jnp.float32) * (HEAD_DIM ** -0.5)
    p = jax.nn.softmax(s, axis=-1).astype(v_mem.dtype)
    return jnp.einsum('bhsm,bmhd->bshd', p, v_mem).reshape(B_, S_, MEM_W)


def forgetting_attention(q, k, v, c):
    cT = jnp.swapaxes(c, 1, 2)
    scale = HEAD_DIM ** -0.5
    n_blocks = q.shape[1] // Q_BLOCK
    outs = []
    for blk in range(n_blocks):
        q0 = blk * Q_BLOCK
        q1 = q0 + Q_BLOCK
        s = jnp.einsum('bqhd,bkhd->bhqk', q[:, q0:q1], k[:, :q1]).astype(jnp.float32) * scale
        s = s + cT[:, :, q0:q1, None] - cT[:, :, None, :q1]
        causal = jnp.arange(q1)[None, :] <= jnp.arange(q0, q1)[:, None]
        s = jnp.where(causal, s, -jnp.inf)
        p = jax.nn.softmax(s, axis=-1).astype(v.dtype)
        outs.append(jnp.einsum('bhqk,bkhd->bqhd', p, v[:, :q1]))
    return jnp.concatenate(outs, axis=1)


def fox_mix(x, w_in, b_f):
    h = x @ w_in
    q, k, v, og, fl, q_mem = jnp.split(
        h, [TOK_W, 2 * TOK_W, 3 * TOK_W, 4 * TOK_W, 4 * TOK_W + N_TOK_HEADS], axis=-1)
    log_f = jax.nn.log_sigmoid((fl + b_f).astype(jnp.float32))
    c = jnp.cumsum(log_f, axis=1)
    o = forgetting_attention(split_heads(q, N_TOK_HEADS), split_heads(k, N_TOK_HEADS),
                             split_heads(v, N_TOK_HEADS), c)
    tok = o.reshape(x.shape[0], x.shape[1], TOK_W) * jax.nn.sigmoid(og)
    return tok.astype(x.dtype), q_mem


def wkv7_scan(r, w, k, v, a, b):
    B_, S_, H_, N_ = r.shape

    def step(state, inp):
        r_t, w_t, k_t, v_t, a_t, b_t = inp
        sa = jnp.einsum('bhij,bhj->bhi', state, a_t)
        state = (state * w_t[:, :, None, :] + sa[..., None] * b_t[:, :, None, :]
                 + v_t[..., None] * k_t[:, :, None, :])
        return state, jnp.einsum('bhij,bhj->bhi', state, r_t)

    xs = tuple(jnp.moveaxis(t, 1, 0) for t in (r, w, k, v, a, b))
    state0 = jnp.zeros((B_, H_, N_, N_), jnp.float32)
    _, y = lax.scan(step, state0, xs)
    return jnp.moveaxis(y, 0, 1)


def rwkv7_mix(x, w_in, mu, w0, w2, a0, a2, g2, k_k, k_a, r_k, lnx_g, lnx_b):
    B_, S_, _ = x.shape
    h = x @ w_in
    hs = h[..., :RWKV_SHIFT_W].astype(jnp.float32)
    hs = hs + mu * (token_shift(hs) - hs)
    r, k, v, wd, ad, gd = jnp.split(
        hs, [TOK_W, 2 * TOK_W, 3 * TOK_W, 3 * TOK_W + DECAY_LORA, 3 * TOK_W + DECAY_LORA + AAA_LORA], axis=-1)
    q_mem = h[..., RWKV_SHIFT_W:]
    w = -jax.nn.softplus(-(w0 + jnp.tanh(wd) @ w2)) - 0.5
    decay = jnp.exp(-jnp.exp(w))
    a = jax.nn.sigmoid(a0 + ad @ a2)
    g = jax.nn.sigmoid(gd) @ g2
    kk = split_heads(k * k_k, N_TOK_HEADS)
    kk = kk / jnp.maximum(jnp.sqrt(jnp.sum(kk * kk, axis=-1, keepdims=True)), 1e-12)
    k = k * (1.0 + (a - 1.0) * k_a)
    rh = split_heads(r, N_TOK_HEADS)
    kh = split_heads(k, N_TOK_HEADS)
    vh = split_heads(v, N_TOK_HEADS)
    ah = split_heads(a, N_TOK_HEADS)
    y = wkv7_scan(rh, split_heads(decay, N_TOK_HEADS), kh, vh, -kk, kk * ah)
    m = jnp.mean(y, axis=-1, keepdims=True)
    var = jnp.mean(jnp.square(y - m), axis=-1, keepdims=True)
    y = ((y - m) * lax.rsqrt(var + GN_EPS)).reshape(B_, S_, TOK_W) * lnx_g + lnx_b
    y = split_heads(y, N_TOK_HEADS) + jnp.sum(rh * kh * r_k, axis=-1, keepdims=True) * vh
    tok = y.reshape(B_, S_, TOK_W) * g
    return tok.astype(x.dtype), q_mem


def moe_ffn(x2, w_router, router_bias, w_gate, w_up, w_down, w_sh_gate, w_sh_up, w_sh_down):
    T = x2.shape[0]
    TK = T * TOP_K
    scores = jax.nn.sigmoid((x2 @ w_router).astype(jnp.float32))
    _, idx = lax.top_k(scores + router_bias.astype(jnp.float32), TOP_K)
    gate = jnp.take_along_axis(scores, idx, axis=1)
    gate = gate / jnp.sum(gate, axis=-1, keepdims=True) * ROUTED_SCALE
    e = idx.reshape(-1)
    tok = jnp.arange(TK, dtype=jnp.int32) // TOP_K
    gflat = gate.reshape(-1)
    counts = jnp.bincount(e, length=N_EXPERTS)
    padded = (counts + EXPERT_BLOCK - 1) // EXPERT_BLOCK * EXPERT_BLOCK
    ends = jnp.cumsum(padded)
    starts = ends - padded
    order = jnp.argsort(e)
    e_sorted = e[order]
    rank = jnp.arange(TK, dtype=jnp.int32) - (jnp.cumsum(counts) - counts)[e_sorted]
    dest = starts[e_sorted] + rank
    n_slots = TK + N_EXPERTS * EXPERT_BLOCK
    n_blocks = n_slots // EXPERT_BLOCK
    slot_tok = jnp.full((n_slots,), T, jnp.int32).at[dest].set(tok[order])
    slot_gate = jnp.zeros((n_slots,), jnp.float32).at[dest].set(gflat[order])
    block_exp = jnp.minimum(
        jnp.searchsorted(ends, jnp.arange(n_blocks, dtype=ends.dtype) * EXPERT_BLOCK, side='right'),
        N_EXPERTS - 1)
    x_pad = jnp.concatenate([x2, jnp.zeros((1, x2.shape[1]), x2.dtype)], axis=0)

    def body(acc, blk):
        tok_b, gate_b, eb = blk
        xb = x_pad[tok_b]
        hb = jax.nn.silu(xb @ w_gate[eb]) * (xb @ w_up[eb])
        yb = hb @ w_down[eb]
        return acc.at[tok_b].add((yb * gate_b[:, None]).astype(acc.dtype)), None

    acc, _ = lax.scan(body, jnp.zeros((T + 1, x2.shape[1]), x2.dtype),
                      (slot_tok.reshape(n_blocks, EXPERT_BLOCK),
                       slot_gate.reshape(n_blocks, EXPERT_BLOCK), block_exp))
    shared = (jax.nn.silu(x2 @ w_sh_gate) * (x2 @ w_sh_up)) @ w_sh_down
    return acc[:T] + shared


def setup_inputs(seed: int = 0) -> dict:
    key = jax.random.key(seed)
    ks = jax.random.split(key, 32)
    n = lambda k, shape: jax.random.normal(k, shape, jnp.float32)
    D = D_MODEL
    NF, NR = N_FOX_LAYERS, N_RWKV_LAYERS
    return {
        'x': n(ks[0], (BATCH, SEQ, D)),
        'mem': n(ks[1], (BATCH, N_MEM, D)),
        'mem_ln_g': 1.0 + 0.02 * n(ks[2], (D,)),
        'mem_ln_b': 0.02 * n(ks[3], (D,)),
        'w_mem_kv': n(ks[4], (D, 2 * MEM_W)) * D ** -0.5,
        'fox_w_in': n(ks[5], (NF, D, FOX_IN_W)) * D ** -0.5,
        'fox_b_f': 2.0 + 0.5 * n(ks[6], (NF, N_TOK_HEADS)),
        'rwkv_w_in': n(ks[7], (NR, D, RWKV_IN_W)) * D ** -0.5,
        'rwkv_mu': jax.random.uniform(ks[8], (NR, RWKV_SHIFT_W), jnp.float32),
        'rwkv_w0': -2.0 + 1.5 * n(ks[9], (NR, TOK_W)),
        'rwkv_w2': n(ks[10], (NR, DECAY_LORA, TOK_W)) * 0.5 * DECAY_LORA ** -0.5,
        'rwkv_a0': 0.5 * n(ks[11], (NR, TOK_W)),
        'rwkv_a2': n(ks[12], (NR, AAA_LORA, TOK_W)) * 0.5 * AAA_LORA ** -0.5,
        'rwkv_g2': n(ks[13], (NR, GATE_LORA, TOK_W)) * GATE_LORA ** -0.5,
        'rwkv_k_k': 1.0 + 0.1 * n(ks[14], (NR, TOK_W)),
        'rwkv_k_a': 1.0 + 0.1 * n(ks[15], (NR, TOK_W)),
        'rwkv_r_k': 0.1 * n(ks[16], (NR, N_TOK_HEADS, HEAD_DIM)),
        'rwkv_lnx_g': 1.0 + 0.02 * n(ks[17], (NR, TOK_W)),
        'rwkv_lnx_b': 0.02 * n(ks[18], (NR, TOK_W)),
        'w_out': n(ks[19], (DEPTH, MIX_W, D)) * MIX_W ** -0.5 * DEEPNORM_BETA,
        'ln1_g': 1.0 + 0.02 * n(ks[20], (DEPTH, D)),
        'ln1_b': 0.02 * n(ks[21], (DEPTH, D)),
        'w_router': n(ks[22], (DEPTH, D, N_EXPERTS)) * D ** -0.5,
        'router_bias': 0.01 * n(ks[23], (DEPTH, N_EXPERTS)),
        'w_exp_gate': n(ks[24], (DEPTH, N_EXPERTS, D, D_EXPERT)) * D ** -0.5,
        'w_exp_up': n(ks[25], (DEPTH, N_EXPERTS, D, D_EXPERT)) * D ** -0.5,
        'w_exp_down': n(ks[26], (DEPTH, N_EXPERTS, D_EXPERT, D)) * D_EXPERT ** -0.5 * DEEPNORM_BETA,
        'w_sh_gate': n(ks[27], (DEPTH, D, D_SHARED)) * D ** -0.5,
        'w_sh_up': n(ks[28], (DEPTH, D, D_SHARED)) * D ** -0.5,
        'w_sh_down': n(ks[29], (DEPTH, D_SHARED, D)) * D_SHARED ** -0.5 * DEEPNORM_BETA,
        'ln2_g': 1.0 + 0.02 * n(ks[30], (DEPTH, D)),
        'ln2_b': 0.02 * n(ks[31], (DEPTH, D)),
    }


def reference(x, mem, mem_ln_g, mem_ln_b, w_mem_kv, fox_w_in, fox_b_f, rwkv_w_in, rwkv_mu,
              rwkv_w0, rwkv_w2, rwkv_a0, rwkv_a2, rwkv_g2, rwkv_k_k, rwkv_k_a, rwkv_r_k,
              rwkv_lnx_g, rwkv_lnx_b, w_out, ln1_g, ln1_b, w_router, router_bias,
              w_exp_gate, w_exp_up, w_exp_down, w_sh_gate, w_sh_up, w_sh_down, ln2_g, ln2_b):
    B_, S_, D = x.shape
    kv = layer_norm(mem, mem_ln_g, mem_ln_b) @ w_mem_kv
    k_mem = split_heads(kv[..., :MEM_W], N_MEM_HEADS)
    v_mem = split_heads(kv[..., MEM_W:], N_MEM_HEADS)
    for i in range(DEPTH):
        j = i // 2
        if i % 2 == 0:
            tok, q_mem = fox_mix(x, fox_w_in[j], fox_b_f[j])
        else:
            tok, q_mem = rwkv7_mix(x, rwkv_w_in[j], rwkv_mu[j], rwkv_w0[j], rwkv_w2[j], rwkv_a0[j],
                                   rwkv_a2[j], rwkv_g2[j], rwkv_k_k[j], rwkv_k_a[j], rwkv_r_k[j],
                                   rwkv_lnx_g[j], rwkv_lnx_b[j])
        mem_out = memory_cross_attention(q_mem, k_mem, v_mem)
        mixed = jnp.concatenate([tok, mem_out.astype(tok.dtype)], axis=-1) @ w_out[i]
        x = layer_norm(DEEPNORM_ALPHA * x + mixed, ln1_g[i], ln1_b[i])
        ffn = moe_ffn(x.reshape(B_ * S_, D), w_router[i], router_bias[i], w_exp_gate[i], w_exp_up[i],
                      w_exp_down[i], w_sh_gate[i], w_sh_up[i], w_sh_down[i]).reshape(B_, S_, D)
        x = layer_norm(DEEPNORM_ALPHA * x + ffn, ln2_g[i], ln2_b[i])
    return x
```

```python
import functools
import math

import jax
import jax.numpy as jnp
from jax import lax
from jax.experimental import pallas as pl
from jax.experimental.pallas import tpu as pltpu

F32 = jnp.float32
BF16 = jnp.bfloat16

D_MODEL = 1024
HEAD_DIM = 64
N_TOK_HEADS = 12
TOK_W = N_TOK_HEADS * HEAD_DIM
N_HEAD_PAIRS = N_TOK_HEADS // 2
N_MEM_HEADS = 4
MEM_W = N_MEM_HEADS * HEAD_DIM
DECAY_LORA = 64
AAA_LORA = 64
GATE_LORA = 128
RWKV_SHIFT_W = 3 * TOK_W + DECAY_LORA + AAA_LORA + GATE_LORA
N_EXPERTS = 64
TOP_K = 6
D_EXPERT = 256
ROUTED_SCALE = 2.5
DEPTH = 2
DEEPNORM_ALPHA = (2 * DEPTH) ** 0.25
LN_EPS = 1e-5
GN_EPS = 64e-5

LANES = 128
AUG_W = 2 * HEAD_DIM
CHUNK = 64
VMEM_LIMIT = 56 * 1024 * 1024


def _cparams(sem):
    return pltpu.CompilerParams(dimension_semantics=sem, vmem_limit_bytes=VMEM_LIMIT)


def _dot(a, b):
    return jnp.dot(a, b, preferred_element_type=F32)


def _dot_nt(a, b):
    return lax.dot_general(a, b, (((1,), (1,)), ((), ())), preferred_element_type=F32)


def _split2(x):
    hi = x.astype(BF16)
    lo = (x - hi.astype(F32)).astype(BF16)
    return hi, lo


def _split3(x):
    hi = x.astype(BF16)
    r1 = x - hi.astype(F32)
    mid = r1.astype(BF16)
    lo = (r1 - mid.astype(F32)).astype(BF16)
    return hi, mid, lo


def _dot_sel(sel, x):
    hi, mid, lo = _split3(x)
    return _dot(sel, hi) + _dot(sel, mid) + _dot(sel, lo)


def _dot3(x, w_hi, w_lo):
    x_hi, x_lo = _split2(x)
    return _dot(x_hi, w_hi) + _dot(x_lo, w_hi) + _dot(x_hi, w_lo)


def _sigmoid(x):
    return 1.0 / (1.0 + jnp.exp(-x))


def _layer_norm(y, g, b):
    mu = jnp.mean(y, axis=-1, keepdims=True)
    yc = y - mu
    var = jnp.mean(yc * yc, axis=-1, keepdims=True)
    return yc * lax.rsqrt(var + LN_EPS) * g + b


def _iota(shape, dim):
    return lax.broadcasted_iota(jnp.int32, shape, dim)


def _div_pow2(x, n):
    return jnp.right_shift(x, int(math.log2(n)))


def _mod_pow2(x, n):
    return jnp.bitwise_and(x, n - 1)


def _full_spec(shape):
    n = len(shape)
    return pl.BlockSpec(shape, lambda *_: (0,) * n)


def _mem_kv_kernel(mem_ref, g_ref, b_ref, w_ref, k_ref, v_ref):
    m = _layer_norm(mem_ref[0], g_ref[...], b_ref[...])
    kv = _dot(m.astype(BF16), w_ref[...])
    k = kv[:, :MEM_W]
    v = kv[:, MEM_W:]
    head = _div_pow2(_iota(k.shape, 1), HEAD_DIM)
    for h in range(N_MEM_HEADS):
        k_ref[0, h] = jnp.where(head == h, k, 0.0).astype(BF16)
        v_ref[0, h] = jnp.where(head == h, v, 0.0).astype(BF16)


def _mem_kv(mem, g, b, w):
    B, n_mem, _ = mem.shape
    out = jax.ShapeDtypeStruct((B, N_MEM_HEADS, n_mem, MEM_W), BF16)
    out_spec = pl.BlockSpec((1, N_MEM_HEADS, n_mem, MEM_W), lambda i: (i, 0, 0, 0))
    return pl.pallas_call(
        _mem_kv_kernel,
        grid=(B,),
        in_specs=[pl.BlockSpec((1, n_mem, D_MODEL), lambda i: (i, 0, 0)),
                  _full_spec((1, D_MODEL)), _full_spec((1, D_MODEL)),
                  _full_spec((D_MODEL, 2 * MEM_W))],
        out_specs=[out_spec, out_spec],
        out_shape=[out, out],
        compiler_params=_cparams(("arbitrary",)),
        name="mem_kv",
    )(mem, g.reshape(1, -1), b.reshape(1, -1), w.astype(BF16))


def _fox_in_kernel(x_ref, wq_ref, wk_ref, wv_ref, wg_ref, wfh_ref, wfl_ref, wm_ref, bf_ref,
                   qc_ref, place_ref, q_ref, k_ref, v_ref, g_ref, qm_ref, carry_ref):
    @pl.when(pl.program_id(1) == 0)
    def _():
        carry_ref[...] = jnp.zeros_like(carry_ref)

    x = x_ref[0]
    xb = x.astype(BF16)
    tm = x.shape[0]
    z = _dot3(x, wfh_ref[...], wfl_ref[...]) + bf_ref[...]
    log_f = jnp.minimum(z, 0.0) - jnp.log(1.0 + jnp.exp(-jnp.abs(z)))
    tril = (_iota((tm, tm), 1) <= _iota((tm, tm), 0)).astype(BF16)
    c = _dot_sel(tril, log_f) + carry_ref[...]
    carry_ref[...] = c[tm - 1:tm, :]
    c_hi, c_mid, c_lo = _split3(c)
    c_parts = jnp.concatenate([c_hi, c_mid, c_lo], axis=1)
    q_ref[0] = (_dot(xb, wq_ref[...]) + qc_ref[...]).astype(BF16)
    k_ref[0] = (_dot(xb, wk_ref[...]) + _dot(c_parts, place_ref[...])).astype(BF16)
    v_ref[0] = _dot(xb, wv_ref[...]).astype(BF16)
    g_ref[0] = _sigmoid(_dot(xb, wg_ref[...])).astype(BF16)
    qm_ref[0] = _dot(xb, wm_ref[...]).astype(BF16)


def _fox_in(x, w_in, b_f, tm):
    B, S, _ = x.shape
    scale = HEAD_DIM ** -0.5
    wq, wk, wv, wg, wf, wm = jnp.split(
        w_in, [TOK_W, 2 * TOK_W, 3 * TOK_W, 4 * TOK_W, 4 * TOK_W + N_TOK_HEADS], axis=1)

    def per_head_pad(w):
        w = w.reshape(D_MODEL, N_TOK_HEADS, HEAD_DIM)
        return jnp.pad(w, ((0, 0), (0, 0), (0, AUG_W - HEAD_DIM))).reshape(D_MODEL, N_TOK_HEADS * AUG_W)

    wq_aug = per_head_pad(wq * scale).astype(BF16)
    wk_aug = per_head_pad(wk).astype(BF16)
    wf_pad = jnp.pad(wf, ((0, 0), (0, LANES - N_TOK_HEADS)))
    wf_hi = wf_pad.astype(BF16)
    wf_lo = (wf_pad - wf_hi.astype(F32)).astype(BF16)
    bf_pad = jnp.pad(b_f, (0, LANES - N_TOK_HEADS)).reshape(1, LANES)
    col = jnp.arange(N_TOK_HEADS * AUG_W)
    is_c = (col % AUG_W >= HEAD_DIM) & (col % AUG_W < HEAD_DIM + 3)
    q_const = jnp.where(is_c, -1.0, 0.0).astype(F32).reshape(1, -1)
    row = jnp.arange(3 * LANES)
    place = ((row[:, None] % LANES == col[None, :] // AUG_W)
             & (col[None, :] % AUG_W == HEAD_DIM + row[:, None] // LANES)).astype(BF16)

    aug_w = N_TOK_HEADS * AUG_W
    row_spec = lambda w: pl.BlockSpec((1, tm, w), lambda b, i: (b, i, 0))
    return pl.pallas_call(
        _fox_in_kernel,
        grid=(B, S // tm),
        in_specs=[row_spec(D_MODEL),
                  _full_spec((D_MODEL, aug_w)), _full_spec((D_MODEL, aug_w)),
                  _full_spec((D_MODEL, TOK_W)), _full_spec((D_MODEL, TOK_W)),
                  _full_spec((D_MODEL, LANES)), _full_spec((D_MODEL, LANES)),
                  _full_spec((D_MODEL, MEM_W)), _full_spec((1, LANES)),
                  _full_spec((1, aug_w)), _full_spec((3 * LANES, aug_w))],
        out_specs=[row_spec(aug_w), row_spec(aug_w), row_spec(TOK_W), row_spec(TOK_W), row_spec(MEM_W)],
        out_shape=[jax.ShapeDtypeStruct((B, S, aug_w), BF16), jax.ShapeDtypeStruct((B, S, aug_w), BF16),
                   jax.ShapeDtypeStruct((B, S, TOK_W), BF16), jax.ShapeDtypeStruct((B, S, TOK_W), BF16),
                   jax.ShapeDtypeStruct((B, S, MEM_W), BF16)],
        scratch_shapes=[pltpu.VMEM((1, LANES), F32)],
        compiler_params=_cparams(("arbitrary", "arbitrary")),
        name="fox_in",
    )(x, wq_aug, wk_aug, wv.astype(BF16), wg.astype(BF16), wf_hi, wf_lo,
      (wm * scale).astype(BF16), bf_pad, q_const, place)


def _fox_attn_kernel(q_ref, k_ref, v_ref, g_ref, o_ref, *, tq):
    qi = pl.program_id(2)
    lane = _iota((tq, LANES), 1)
    outs = []
    for h in range(2):
        q = q_ref[0, :, h * AUG_W:(h + 1) * AUG_W]

        def block(kj, carry, masked):
            m, l, acc = carry
            start = pl.multiple_of(kj * tq, tq)
            k = k_ref[0, pl.ds(start, tq), h * AUG_W:(h + 1) * AUG_W]
            s = _dot_nt(q, k)
            if masked:
                s = jnp.where(_iota((tq, tq), 1) <= _iota((tq, tq), 0), s, -jnp.inf)
            m_new = jnp.maximum(m, jnp.max(s, axis=1, keepdims=True))
            p = jnp.exp(s - m_new)
            alpha = jnp.exp(m - m_new)
            l = alpha * l + jnp.sum(p, axis=1, keepdims=True)
            v = v_ref[0, pl.ds(start, tq), :]
            acc = alpha * acc + _dot(p.astype(BF16), v)
            return m_new, l, acc

        init = (jnp.full((tq, 1), -jnp.inf, F32), jnp.zeros((tq, 1), F32), jnp.zeros((tq, LANES), F32))
        carry = lax.fori_loop(0, qi, functools.partial(block, masked=False), init)
        _, l, acc = block(qi, carry, masked=True)
        outs.append(acc / l)
    o = jnp.where(lane < HEAD_DIM, outs[0], outs[1])
    o_ref[0] = (o * g_ref[0].astype(F32)).astype(BF16)


def _fox_attn(q_aug, k_aug, v, g, tq):
    B, S, _ = v.shape
    return pl.pallas_call(
        functools.partial(_fox_attn_kernel, tq=tq),
        grid=(B, N_HEAD_PAIRS, S // tq),
        in_specs=[pl.BlockSpec((1, tq, 2 * AUG_W), lambda b, p, i: (b, i, p)),
                  pl.BlockSpec((1, S, 2 * AUG_W), lambda b, p, i: (b, 0, p)),
                  pl.BlockSpec((1, S, LANES), lambda b, p, i: (b, 0, p)),
                  pl.BlockSpec((1, tq, LANES), lambda b, p, i: (b, i, p))],
        out_specs=pl.BlockSpec((1, tq, LANES), lambda b, p, i: (b, i, p)),
        out_shape=jax.ShapeDtypeStruct((B, S, TOK_W), BF16),
        compiler_params=_cparams(("arbitrary", "arbitrary", "arbitrary")),
        name="fox_attn",
    )(q_aug, k_aug, v, g)


def _rwkv_in_kernel(x_ref, w_ref, mu_ref, w0_ref, w2_ref, a0_ref, a2_ref, g2_ref, kk_ref, ka_ref,
                    rk_ref, head_ones_ref,
                    rt_ref, at_ref, bt_ref, kt_ref, bh_ref, kh_ref, v_ref, pc_ref, bonus_ref, g_ref,
                    qm_ref, prev_ref):
    @pl.when(pl.program_id(1) == 0)
    def _():
        prev_ref[...] = jnp.zeros_like(prev_ref)

    tm = x_ref.shape[1]
    h = _dot(x_ref[0].astype(BF16), w_ref[...])
    qm_ref[0] = (h[:, RWKV_SHIFT_W:] * (HEAD_DIM ** -0.5)).astype(BF16)
    hs = h[:, :RWKV_SHIFT_W]
    row = _iota(hs.shape, 0)
    shifted = jnp.where(row == 0, prev_ref[...], pltpu.roll(hs, 1, 0))
    prev_ref[...] = hs[tm - 1:tm, :]
    hs = hs + mu_ref[...] * (shifted - hs)
    r = hs[:, :TOK_W]
    k = hs[:, TOK_W:2 * TOK_W]
    v = hs[:, 2 * TOK_W:3 * TOK_W]
    wa = hs[:, 3 * TOK_W:3 * TOK_W + LANES]
    gd = hs[:, 3 * TOK_W + LANES:]
    w = w0_ref[...] + _dot(jnp.tanh(wa).astype(BF16), w2_ref[...])
    w = -(jnp.maximum(-w, 0.0) + jnp.log(1.0 + jnp.exp(-jnp.abs(w)))) - 0.5
    log_decay = -jnp.exp(w)
    a = _sigmoid(a0_ref[...] + _dot(wa.astype(BF16), a2_ref[...]))
    g_ref[0] = _dot(_sigmoid(gd).astype(BF16), g2_ref[...])
    kk = k * kk_ref[...]
    ss_hi, ss_lo = _split2(kk * kk)
    norm2 = _dot(ss_hi, head_ones_ref[...]) + _dot(ss_lo, head_ones_ref[...])
    kk = kk / jnp.maximum(jnp.sqrt(norm2), 1e-12)
    k = k * (1.0 + (a - 1.0) * ka_ref[...])
    bonus_ref[0] = r * k * rk_ref[...]
    v_ref[0] = v
    a_s = -kk
    b_s = kk * a
    same_chunk = _div_pow2(_iota((tm, tm), 0), CHUNK) == _div_pow2(_iota((tm, tm), 1), CHUNK)
    incl = (same_chunk & (_iota((tm, tm), 1) <= _iota((tm, tm), 0))).astype(BF16)
    d_hi, d_mid, d_lo = _split3(log_decay)
    cw = _dot(incl, d_hi) + _dot(incl, d_mid) + _dot(incl, d_lo)
    same = same_chunk.astype(BF16)
    cw_end = _dot(same, d_hi) + _dot(same, d_mid) + _dot(same, d_lo)
    e_in = jnp.exp(cw)
    e_out = jnp.exp(-cw)
    e_end = jnp.exp(cw_end - cw)
    rt_ref[0] = r * e_in
    at_ref[0] = a_s * jnp.exp(cw - log_decay)
    bt_ref[0] = b_s * e_out
    kt_ref[0] = k * e_out
    bh_ref[0] = b_s * e_end
    kh_ref[0] = k * e_end
    pc_ref[0] = jnp.exp(cw_end)


def _rwkv_in(x, w_in, mu, w0, w2, a0, a2, g2, k_k, k_a, r_k, tm):
    B, S, _ = x.shape
    in_w = w_in.shape[1]
    w2_pad = jnp.concatenate([w2, jnp.zeros((AAA_LORA, TOK_W), F32)], axis=0).astype(BF16)
    a2_pad = jnp.concatenate([jnp.zeros((DECAY_LORA, TOK_W), F32), a2], axis=0).astype(BF16)
    head = jnp.arange(TOK_W) // HEAD_DIM
    head_ones = (head[:, None] == head[None, :]).astype(BF16)
    vec = lambda t: t.reshape(1, -1)
    row_spec = lambda w: pl.BlockSpec((1, tm, w), lambda b, i: (b, i, 0))
    tok = jax.ShapeDtypeStruct((B, S, TOK_W), F32)
    return pl.pallas_call(
        _rwkv_in_kernel,
        grid=(B, S // tm),
        in_specs=[row_spec(D_MODEL), _full_spec((D_MODEL, in_w)), _full_spec((1, RWKV_SHIFT_W)),
                  _full_spec((1, TOK_W)), _full_spec((LANES, TOK_W)), _full_spec((1, TOK_W)),
                  _full_spec((LANES, TOK_W)), _full_spec((GATE_LORA, TOK_W)), _full_spec((1, TOK_W)),
                  _full_spec((1, TOK_W)), _full_spec((1, TOK_W)), _full_spec((TOK_W, TOK_W))],
        out_specs=[row_spec(TOK_W)] * 10 + [row_spec(MEM_W)],
        out_shape=[tok] * 10 + [jax.ShapeDtypeStruct((B, S, MEM_W), BF16)],
        scratch_shapes=[pltpu.VMEM((1, RWKV_SHIFT_W), F32)],
        compiler_params=_cparams(("arbitrary", "arbitrary")),
        name="rwkv_in",
    )(x, w_in.astype(BF16), vec(mu), vec(w0), w2_pad, vec(a0), a2_pad, g2.astype(BF16), vec(k_k),
      vec(k_a), vec(r_k), head_ones)


def _rwkv_scan_kernel(rt_ref, at_ref, bt_ref, kt_ref, bh_ref, kh_ref, v_ref, pc_ref, y_ref, state_ref,
                      *, n_chunks):
    @pl.when(pl.program_id(2) == 0)
    def _():
        state_ref[...] = jnp.zeros_like(state_ref)

    C = CHUNK
    lane_head = _div_pow2(_iota((2 * C, LANES), 1), HEAD_DIM)
    row_head = _div_pow2(_iota((2 * C, LANES), 0), C)
    keep = lane_head == row_head
    pos_r = _mod_pow2(_iota((2 * C, 2 * C), 0), C)
    pos_c = _mod_pow2(_iota((2 * C, 2 * C), 1), C)
    strict = pos_c < pos_r
    incl = pos_c <= pos_r
    eye = (_iota((2 * C, 2 * C), 0) == _iota((2 * C, 2 * C), 1)).astype(F32)

    def stack(ref, c):
        t = ref[0, c * C:(c + 1) * C, :]
        return jnp.where(keep, jnp.concatenate([t, t], axis=0), 0.0).astype(BF16)

    for c in range(n_chunks):
        a2, r2, b2, k2 = stack(at_ref, c), stack(rt_ref, c), stack(bt_ref, c), stack(kt_ref, c)
        bh2, kh2, v2 = stack(bh_ref, c), stack(kh_ref, c), stack(v_ref, c)
        ar = jnp.concatenate([a2, r2], axis=0)
        gram = _dot_nt(ar, jnp.concatenate([b2, k2], axis=0))
        n_ab = jnp.where(strict, gram[:2 * C, :2 * C], 0.0)
        a_ak = jnp.where(strict, gram[:2 * C, 2 * C:], 0.0)
        a_rb = jnp.where(incl, gram[2 * C:, :2 * C], 0.0)
        a_rk = jnp.where(incl, gram[2 * C:, 2 * C:], 0.0)
        inv = eye + n_ab
        power = n_ab
        for _ in range(int(math.log2(C)) - 1):
            pb = power.astype(BF16)
            power = _dot(pb, pb)
            inv = inv + _dot(inv.astype(BF16), power.astype(BF16))
        state = state_ref[...]
        from_state = _dot_nt(ar, state.astype(BF16))
        from_v = _dot(jnp.concatenate([a_ak, a_rk], axis=0).astype(BF16), v2)
        u2 = _dot(inv.astype(BF16), (from_state[:2 * C] + from_v[:2 * C]).astype(BF16))
        y2 = from_state[2 * C:] + from_v[2 * C:] + _dot(a_rb.astype(BF16), u2.astype(BF16))
        y_ref[0, c * C:(c + 1) * C, :] = y2[:C] + y2[C:]
        uv_t = jnp.concatenate([u2, v2.astype(F32)], axis=0).T.astype(BF16)
        state_ref[...] = state * pc_ref[0, c * C:c * C + 1, :] + _dot(
            uv_t, jnp.concatenate([bh2, kh2], axis=0))


def _rwkv_scan(rt, at, bt, kt, bh, kh, v, pc, rows):
    B, S, _ = v.shape
    spec = pl.BlockSpec((1, rows, LANES), lambda b, p, i: (b, i, p))
    return pl.pallas_call(
        functools.partial(_rwkv_scan_kernel, n_chunks=rows // CHUNK),
        grid=(B, N_HEAD_PAIRS, S // rows),
        in_specs=[spec] * 8,
        out_specs=spec,
        out_shape=jax.ShapeDtypeStruct((B, S, TOK_W), F32),
        scratch_shapes=[pltpu.VMEM((LANES, LANES), F32)],
        compiler_params=_cparams(("arbitrary", "arbitrary", "arbitrary")),
        name="rwkv_scan",
    )(rt, at, bt, kt, bh, kh, v, pc)


def _rwkv_post_kernel(y_ref, bonus_ref, v_ref, g_ref, lg_ref, lb_ref, head_mean_ref, o_ref):
    y = y_ref[0]
    hm = head_mean_ref[...]

    def head_mean(t):
        hi, lo = _split2(t)
        return _dot(hi, hm) + _dot(lo, hm)

    yc = y - head_mean(y)
    var = head_mean(yc * yc)
    yn = yc * lax.rsqrt(var + GN_EPS) * lg_ref[...] + lb_ref[...]
    bonus = head_mean(bonus_ref[0]) * float(HEAD_DIM)
    o_ref[0] = ((yn + bonus * v_ref[0]) * g_ref[0]).astype(BF16)


def _rwkv_post(y, bonus, v, g, lnx_g, lnx_b, tm):
    B, S, _ = y.shape
    head = jnp.arange(TOK_W) // HEAD_DIM
    head_mean = ((head[:, None] == head[None, :]).astype(F32) / HEAD_DIM).astype(BF16)
    row_spec = pl.BlockSpec((1, tm, TOK_W), lambda b, i: (b, i, 0))
    return pl.pallas_call(
        _rwkv_post_kernel,
        grid=(B, S // tm),
        in_specs=[row_spec] * 4 + [_full_spec((1, TOK_W)), _full_spec((1, TOK_W)),
                                   _full_spec((TOK_W, TOK_W))],
        out_specs=row_spec,
        out_shape=jax.ShapeDtypeStruct((B, S, TOK_W), BF16),
        compiler_params=_cparams(("arbitrary", "arbitrary")),
        name="rwkv_post",
    )(y, bonus, v, g, lnx_g.reshape(1, -1), lnx_b.reshape(1, -1), head_mean)


def _mix_out_kernel(tok_ref, qm_ref, km_ref, vm_ref, x_ref, wo_tok_ref, wo_mem_ref, g_ref, b_ref,
                    wr_hi_ref, wr_lo_ref, rb_ref, x1_ref, x1b_ref, gate_ref):
    qm = qm_ref[0]
    mem_out = None
    for h in range(N_MEM_HEADS):
        s = _dot_nt(qm, km_ref[0, h])
        e = jnp.exp(s - jnp.max(s, axis=1, keepdims=True))
        o = _dot(e.astype(BF16), vm_ref[0, h]) / jnp.sum(e, axis=1, keepdims=True)
        mem_out = o if mem_out is None else mem_out + o
    mixed = _dot(tok_ref[0], wo_tok_ref[...]) + _dot(mem_out.astype(BF16), wo_mem_ref[...])
    x1 = _layer_norm(DEEPNORM_ALPHA * x_ref[0] + mixed, g_ref[...], b_ref[...])
    x1_ref[0] = x1
    x1b_ref[0] = x1.astype(BF16)
    scores = _sigmoid(_dot3(x1, wr_hi_ref[...], wr_lo_ref[...]))
    lane = _iota(scores.shape, 1)
    lane_f = lane.astype(F32)
    cand = jnp.where(lane < N_EXPERTS, scores + rb_ref[...], -jnp.inf)
    picked = jnp.zeros(scores.shape, jnp.bool_)
    for _ in range(TOP_K):
        best = jnp.max(cand, axis=1, keepdims=True)
        first = jnp.min(jnp.where(cand == best, lane_f, float(LANES)), axis=1, keepdims=True)
        hit = lane_f == first
        picked = jnp.logical_or(picked, hit)
        cand = jnp.where(hit, -jnp.inf, cand)
    chosen = jnp.where(picked, scores, 0.0)
    gate = chosen / jnp.sum(chosen, axis=1, keepdims=True) * ROUTED_SCALE
    gate_ref[0] = jnp.where(lane == N_EXPERTS, 1.0, gate)


def _mix_out(tok, qm, km, vm, x, w_out, ln_g, ln_b, w_router, router_bias, tm):
    B, S, _ = x.shape
    n_mem = km.shape[2]
    wr = jnp.pad(w_router, ((0, 0), (0, LANES - N_EXPERTS)))
    wr_hi = wr.astype(BF16)
    wr_lo = (wr - wr_hi.astype(F32)).astype(BF16)
    rb = jnp.pad(router_bias, (0, LANES - N_EXPERTS)).reshape(1, LANES)
    row_spec = lambda w: pl.BlockSpec((1, tm, w), lambda b, i: (b, i, 0))
    mem_spec = pl.BlockSpec((1, N_MEM_HEADS, n_mem, MEM_W), lambda b, i: (b, 0, 0, 0))
    return pl.pallas_call(
        _mix_out_kernel,
        grid=(B, S // tm),
        in_specs=[row_spec(TOK_W), row_spec(MEM_W), mem_spec, mem_spec, row_spec(D_MODEL),
                  _full_spec((TOK_W, D_MODEL)), _full_spec((MEM_W, D_MODEL)),
                  _full_spec((1, D_MODEL)), _full_spec((1, D_MODEL)),
                  _full_spec((D_MODEL, LANES)), _full_spec((D_MODEL, LANES)), _full_spec((1, LANES))],
        out_specs=[row_spec(D_MODEL), row_spec(D_MODEL), row_spec(LANES)],
        out_shape=[jax.ShapeDtypeStruct((B, S, D_MODEL), F32), jax.ShapeDtypeStruct((B, S, D_MODEL), BF16),
                   jax.ShapeDtypeStruct((B, S, LANES), F32)],
        compiler_params=_cparams(("arbitrary", "arbitrary")),
        name="mix_out",
    )(tok, qm, km, vm, x, w_out[:TOK_W].astype(BF16), w_out[TOK_W:].astype(BF16),
      ln_g.reshape(1, -1), ln_b.reshape(1, -1), wr_hi, wr_lo, rb)


def _moe_kernel(xb_ref, x_ref, gate_ref, wg_ref, wu_ref, wd_ref, g_ref, b_ref, o_ref, acc_ref):
    e = pl.program_id(1)

    @pl.when(e == 0)
    def _():
        acc_ref[...] = jnp.zeros_like(acc_ref)

    xb = xb_ref[...]
    gates = gate_ref[...]
    gate = jnp.sum(jnp.where(_iota(gates.shape, 1) == e, gates, 0.0), axis=1, keepdims=True)
    hg = _dot(xb, wg_ref[0])
    hu = _dot(xb, wu_ref[0])
    hidden = hg * _sigmoid(hg) * hu
    hidden = jnp.where(gate > 0.0, hidden * gate, 0.0)
    acc_ref[...] += _dot(hidden.astype(BF16), wd_ref[0])

    @pl.when(e == pl.num_programs(1) - 1)
    def _():
        o_ref[...] = _layer_norm(DEEPNORM_ALPHA * x_ref[...] + acc_ref[...], g_ref[...], b_ref[...])


def _moe(xb, x, gates, w_gate, w_up, w_down, ln_g, ln_b, tm):
    T = x.shape[0]
    n_e = w_gate.shape[0]
    row_spec = lambda w: pl.BlockSpec((tm, w), lambda i, e: (i, 0))
    return pl.pallas_call(
        _moe_kernel,
        grid=(T // tm, n_e),
        in_specs=[row_spec(D_MODEL), row_spec(D_MODEL), row_spec(LANES),
                  pl.BlockSpec((1, D_MODEL, D_EXPERT), lambda i, e: (e, 0, 0)),
                  pl.BlockSpec((1, D_MODEL, D_EXPERT), lambda i, e: (e, 0, 0)),
                  pl.BlockSpec((1, D_EXPERT, D_MODEL), lambda i, e: (e, 0, 0)),
                  _full_spec((1, D_MODEL)), _full_spec((1, D_MODEL))],
        out_specs=row_spec(D_MODEL),
        out_shape=jax.ShapeDtypeStruct((T, D_MODEL), F32),
        scratch_shapes=[pltpu.VMEM((tm, D_MODEL), F32)],
        compiler_params=_cparams(("arbitrary", "arbitrary")),
        name="moe",
    )(xb, x, gates, w_gate, w_up, w_down, ln_g.reshape(1, -1), ln_b.reshape(1, -1))


def _tile(n, want):
    t = min(n, want)
    assert n % t == 0, (n, t)
    return t


def kernel(x, mem, mem_ln_g, mem_ln_b, w_mem_kv, fox_w_in, fox_b_f, rwkv_w_in, rwkv_mu, rwkv_w0, rwkv_w2,
           rwkv_a0, rwkv_a2, rwkv_g2, rwkv_k_k, rwkv_k_a, rwkv_r_k, rwkv_lnx_g, rwkv_lnx_b, w_out, ln1_g,
           ln1_b, w_router, router_bias, w_exp_gate, w_exp_up, w_exp_down, w_sh_gate, w_sh_up, w_sh_down,
           ln2_g, ln2_b):
    B, S, D = x.shape
    assert D == D_MODEL and S % CHUNK == 0
    T = B * S
    t_proj = _tile(S, 512)
    t_attn = _tile(S, 256)
    t_rwkv = _tile(S, 256)
    t_moe = _tile(T, 1024)

    km, vm = _mem_kv(mem, mem_ln_g, mem_ln_b, w_mem_kv)
    for i in range(DEPTH):
        j = i // 2
        if i % 2 == 0:
            q_aug, k_aug, v, g, qm = _fox_in(x, fox_w_in[j], fox_b_f[j], t_proj)
            tok = _fox_attn(q_aug, k_aug, v, g, t_attn)
        else:
            (rt, at, bt, kt, bh, kh, v, pc, bonus, g, qm) = _rwkv_in(
                x, rwkv_w_in[j], rwkv_mu[j], rwkv_w0[j], rwkv_w2[j], rwkv_a0[j], rwkv_a2[j], rwkv_g2[j],
                rwkv_k_k[j], rwkv_k_a[j], rwkv_r_k[j].reshape(-1), t_rwkv)
            y = _rwkv_scan(rt, at, bt, kt, bh, kh, v, pc, t_rwkv)
            tok = _rwkv_post(y, bonus, v, g, rwkv_lnx_g[j], rwkv_lnx_b[j], t_proj)
        x1, x1b, gates = _mix_out(tok, qm, km, vm, x, w_out[i], ln1_g[i], ln1_b[i], w_router[i],
                                  router_bias[i], t_proj)
        wg = jnp.concatenate([w_exp_gate[i], w_sh_gate[i][None]], axis=0).astype(BF16)
        wu = jnp.concatenate([w_exp_up[i], w_sh_up[i][None]], axis=0).astype(BF16)
        wd = jnp.concatenate([w_exp_down[i], w_sh_down[i][None]], axis=0).astype(BF16)
        x = _moe(x1b.reshape(T, D), x1.reshape(T, D), gates.reshape(T, LANES), wg, wu, wd,
                 ln2_g[i], ln2_b[i], t_moe).reshape(B, S, D)
    return x
```

```python
import functools
import math

import jax
import jax.numpy as jnp
from jax import lax
from jax.experimental import pallas as pl
from jax.experimental.pallas import tpu as pltpu

F32 = jnp.float32
BF16 = jnp.bfloat16

D_MODEL = 1024
HEAD_DIM = 64
N_TOK_HEADS = 12
TOK_W = N_TOK_HEADS * HEAD_DIM
N_HEAD_PAIRS = N_TOK_HEADS // 2
N_MEM_HEADS = 4
MEM_W = N_MEM_HEADS * HEAD_DIM
DECAY_LORA = 64
AAA_LORA = 64
GATE_LORA = 128
RWKV_SHIFT_W = 3 * TOK_W + DECAY_LORA + AAA_LORA + GATE_LORA
N_EXPERTS = 64
TOP_K = 6
D_EXPERT = 256
ROUTED_SCALE = 2.5
DEPTH = 2
DEEPNORM_ALPHA = (2 * DEPTH) ** 0.25
LN_EPS = 1e-5
GN_EPS = 64e-5
LOG2E = math.log2(math.e)

LANES = 128
AUG_W = 2 * HEAD_DIM
CHUNK = 64
VMEM_LIMIT = 56 * 1024 * 1024


def _cparams(sem):
    return pltpu.CompilerParams(dimension_semantics=sem, vmem_limit_bytes=VMEM_LIMIT)


def _dot(a, b):
    return jnp.dot(a, b, preferred_element_type=F32)


def _dot_nt(a, b):
    return lax.dot_general(a, b, (((1,), (1,)), ((), ())), preferred_element_type=F32)


def _split2(x):
    hi = x.astype(BF16)
    lo = (x - hi.astype(F32)).astype(BF16)
    return hi, lo


def _split3(x):
    hi = x.astype(BF16)
    r1 = x - hi.astype(F32)
    mid = r1.astype(BF16)
    lo = (r1 - mid.astype(F32)).astype(BF16)
    return hi, mid, lo


def _dot_sel(sel, x):
    hi, mid, lo = _split3(x)
    return _dot(sel, hi) + _dot(sel, mid) + _dot(sel, lo)


def _dot3(x, w_hi, w_lo):
    x_hi, x_lo = _split2(x)
    return _dot(x_hi, w_hi) + _dot(x_lo, w_hi) + _dot(x_hi, w_lo)


def _sigmoid(x):
    return 1.0 / (1.0 + jnp.exp(-x))


def _layer_norm(y, g, b):
    mu = jnp.mean(y, axis=-1, keepdims=True)
    yc = y - mu
    var = jnp.mean(yc * yc, axis=-1, keepdims=True)
    return yc * lax.rsqrt(var + LN_EPS) * g + b


def _iota(shape, dim):
    return lax.broadcasted_iota(jnp.int32, shape, dim)


def _div_pow2(x, n):
    return jnp.right_shift(x, int(math.log2(n)))


def _mod_pow2(x, n):
    return jnp.bitwise_and(x, n - 1)


def _full_spec(shape):
    n = len(shape)
    return pl.BlockSpec(shape, lambda *_: (0,) * n)


def _mem_kv_kernel(mem_ref, g_ref, b_ref, w_ref, k_ref, v_ref):
    m = _layer_norm(mem_ref[0], g_ref[...], b_ref[...])
    kv = _dot(m.astype(BF16), w_ref[...])
    k = kv[:, :MEM_W]
    v = kv[:, MEM_W:]
    head = _div_pow2(_iota(k.shape, 1), HEAD_DIM)
    for h in range(N_MEM_HEADS):
        k_ref[0, h] = jnp.where(head == h, k, 0.0).astype(BF16)
        v_ref[0, h] = jnp.where(head == h, v, 0.0).astype(BF16)


def _mem_kv(mem, g, b, w):
    B, n_mem, _ = mem.shape
    out = jax.ShapeDtypeStruct((B, N_MEM_HEADS, n_mem, MEM_W), BF16)
    out_spec = pl.BlockSpec((1, N_MEM_HEADS, n_mem, MEM_W), lambda i: (i, 0, 0, 0))
    return pl.pallas_call(
        _mem_kv_kernel,
        grid=(B,),
        in_specs=[pl.BlockSpec((1, n_mem, D_MODEL), lambda i: (i, 0, 0)),
                  _full_spec((1, D_MODEL)), _full_spec((1, D_MODEL)),
                  _full_spec((D_MODEL, 2 * MEM_W))],
        out_specs=[out_spec, out_spec],
        out_shape=[out, out],
        compiler_params=_cparams(("arbitrary",)),
        name="mem_kv",
    )(mem, g.reshape(1, -1), b.reshape(1, -1), w.astype(BF16))


def _fox_in_kernel(x_ref, wq_ref, wk_ref, wv_ref, wg_ref, wfh_ref, wfl_ref, wm_ref, bf_ref,
                   qc_ref, place_ref, q_ref, k_ref, v_ref, g_ref, qm_ref, carry_ref):
    @pl.when(pl.program_id(1) == 0)
    def _():
        carry_ref[...] = jnp.zeros_like(carry_ref)

    x = x_ref[0]
    xb = x.astype(BF16)
    tm = x.shape[0]
    z = _dot3(x, wfh_ref[...], wfl_ref[...]) + bf_ref[...]
    log_f = jnp.minimum(z, 0.0) - jnp.log(1.0 + jnp.exp(-jnp.abs(z)))
    tril = (_iota((tm, tm), 1) <= _iota((tm, tm), 0)).astype(BF16)
    c = _dot_sel(tril, log_f) + carry_ref[...]
    carry_ref[...] = c[tm - 1:tm, :]
    c_hi, c_mid, c_lo = _split3(c * LOG2E)
    c_parts = jnp.concatenate([c_hi, c_mid, c_lo], axis=1)
    q_ref[0] = (_dot(xb, wq_ref[...]) + qc_ref[...]).astype(BF16)
    k_ref[0] = (_dot(xb, wk_ref[...]) + _dot(c_parts, place_ref[...])).astype(BF16)
    v_ref[0] = _dot(xb, wv_ref[...]).astype(BF16)
    g_ref[0] = _sigmoid(_dot(xb, wg_ref[...])).astype(BF16)
    qm_ref[0] = _dot(xb, wm_ref[...]).astype(BF16)


def _fox_in(x, w_in, b_f, tm):
    B, S, _ = x.shape
    scale = HEAD_DIM ** -0.5
    wq, wk, wv, wg, wf, wm = jnp.split(
        w_in, [TOK_W, 2 * TOK_W, 3 * TOK_W, 4 * TOK_W, 4 * TOK_W + N_TOK_HEADS], axis=1)

    def per_head_pad(w):
        w = w.reshape(D_MODEL, N_TOK_HEADS, HEAD_DIM)
        return jnp.pad(w, ((0, 0), (0, 0), (0, AUG_W - HEAD_DIM))).reshape(D_MODEL, N_TOK_HEADS * AUG_W)

    wq_aug = per_head_pad(wq * (scale * LOG2E)).astype(BF16)
    wk_aug = per_head_pad(wk).astype(BF16)
    wf_pad = jnp.pad(wf, ((0, 0), (0, LANES - N_TOK_HEADS)))
    wf_hi = wf_pad.astype(BF16)
    wf_lo = (wf_pad - wf_hi.astype(F32)).astype(BF16)
    bf_pad = jnp.pad(b_f, (0, LANES - N_TOK_HEADS)).reshape(1, LANES)
    col = jnp.arange(N_TOK_HEADS * AUG_W)
    is_c = (col % AUG_W >= HEAD_DIM) & (col % AUG_W < HEAD_DIM + 3)
    q_const = jnp.where(is_c, -1.0, 0.0).astype(F32).reshape(1, -1)
    row = jnp.arange(3 * LANES)
    place = ((row[:, None] % LANES == col[None, :] // AUG_W)
             & (col[None, :] % AUG_W == HEAD_DIM + row[:, None] // LANES)).astype(BF16)

    aug_w = N_TOK_HEADS * AUG_W
    row_spec = lambda w: pl.BlockSpec((1, tm, w), lambda b, i: (b, i, 0))
    return pl.pallas_call(
        _fox_in_kernel,
        grid=(B, S // tm),
        in_specs=[row_spec(D_MODEL),
                  _full_spec((D_MODEL, aug_w)), _full_spec((D_MODEL, aug_w)),
                  _full_spec((D_MODEL, TOK_W)), _full_spec((D_MODEL, TOK_W)),
                  _full_spec((D_MODEL, LANES)), _full_spec((D_MODEL, LANES)),
                  _full_spec((D_MODEL, MEM_W)), _full_spec((1, LANES)),
                  _full_spec((1, aug_w)), _full_spec((3 * LANES, aug_w))],
        out_specs=[row_spec(aug_w), row_spec(aug_w), row_spec(TOK_W), row_spec(TOK_W), row_spec(MEM_W)],
        out_shape=[jax.ShapeDtypeStruct((B, S, aug_w), BF16), jax.ShapeDtypeStruct((B, S, aug_w), BF16),
                   jax.ShapeDtypeStruct((B, S, TOK_W), BF16), jax.ShapeDtypeStruct((B, S, TOK_W), BF16),
                   jax.ShapeDtypeStruct((B, S, MEM_W), BF16)],
        scratch_shapes=[pltpu.VMEM((1, LANES), F32)],
        compiler_params=_cparams(("arbitrary", "arbitrary")),
        name="fox_in",
    )(x, wq_aug, wk_aug, wv.astype(BF16), wg.astype(BF16), wf_hi, wf_lo,
      (wm * scale).astype(BF16), bf_pad, q_const, place)


def _fox_attn_kernel(q_ref, k_ref, v_ref, g_ref, o_ref, *, tq):
    qi = pl.program_id(2)
    qs = [q_ref[0, :, h * AUG_W:(h + 1) * AUG_W] for h in range(2)]

    def block(kj, carry, masked):
        start = pl.multiple_of(kj * tq, tq)
        v = v_ref[0, pl.ds(start, tq), :]
        out = []
        for h in range(2):
            m, l, acc = carry[h]
            k = k_ref[0, pl.ds(start, tq), h * AUG_W:(h + 1) * AUG_W]
            s = _dot_nt(qs[h], k)
            if masked:
                s = jnp.where(_iota((tq, tq), 1) <= _iota((tq, tq), 0), s, -jnp.inf)
            m_new = jnp.maximum(m, jnp.max(s, axis=1, keepdims=True))
            p = jnp.exp2(s - m_new)
            alpha = jnp.exp2(m - m_new)
            p_lanes = functools.reduce(
                jnp.add, [p[:, i * LANES:(i + 1) * LANES] for i in range(tq // LANES)])
            l = alpha * l + p_lanes
            acc = alpha * acc + _dot(p.astype(BF16), v)
            out.append((m_new, l, acc))
        return tuple(out)

    init = (jnp.full((tq, 1), -jnp.inf, F32), jnp.zeros((tq, LANES), F32), jnp.zeros((tq, LANES), F32))
    carry = lax.fori_loop(0, qi, functools.partial(block, masked=False), (init, init))
    (_, l0, acc0), (_, l1, acc1) = block(qi, carry, masked=True)
    o0 = acc0 / jnp.sum(l0, axis=1, keepdims=True)
    o1 = acc1 / jnp.sum(l1, axis=1, keepdims=True)
    o = jnp.where(_iota((tq, LANES), 1) < HEAD_DIM, o0, o1)
    o_ref[0] = (o * g_ref[0].astype(F32)).astype(BF16)


def _fox_attn(q_aug, k_aug, v, g, tq):
    B, S, _ = v.shape
    return pl.pallas_call(
        functools.partial(_fox_attn_kernel, tq=tq),
        grid=(B, N_HEAD_PAIRS, S // tq),
        in_specs=[pl.BlockSpec((1, tq, 2 * AUG_W), lambda b, p, i: (b, i, p)),
                  pl.BlockSpec((1, S, 2 * AUG_W), lambda b, p, i: (b, 0, p)),
                  pl.BlockSpec((1, S, LANES), lambda b, p, i: (b, 0, p)),
                  pl.BlockSpec((1, tq, LANES), lambda b, p, i: (b, i, p))],
        out_specs=pl.BlockSpec((1, tq, LANES), lambda b, p, i: (b, i, p)),
        out_shape=jax.ShapeDtypeStruct((B, S, TOK_W), BF16),
        compiler_params=_cparams(("arbitrary", "arbitrary", "arbitrary")),
        name="fox_attn",
    )(q_aug, k_aug, v, g)


def _rwkv_in_kernel(x_ref, w_ref, mu_ref, w0_ref, w2_ref, a0_ref, a2_ref, g2_ref, kk_ref, ka_ref,
                    rk_ref, head_ones_ref,
                    rt_ref, at_ref, bt_ref, kt_ref, bh_ref, kh_ref, v_ref, pc_ref, bonus_ref, g_ref,
                    qm_ref, prev_ref):
    @pl.when(pl.program_id(1) == 0)
    def _():
        prev_ref[...] = jnp.zeros_like(prev_ref)

    tm = x_ref.shape[1]
    h = _dot(x_ref[0].astype(BF16), w_ref[...])
    qm_ref[0] = (h[:, RWKV_SHIFT_W:] * (HEAD_DIM ** -0.5)).astype(BF16)
    hs = h[:, :RWKV_SHIFT_W]
    row = _iota(hs.shape, 0)
    shifted = jnp.where(row == 0, prev_ref[...], pltpu.roll(hs, 1, 0))
    prev_ref[...] = hs[tm - 1:tm, :]
    hs = hs + mu_ref[...] * (shifted - hs)
    r = hs[:, :TOK_W]
    k = hs[:, TOK_W:2 * TOK_W]
    v = hs[:, 2 * TOK_W:3 * TOK_W]
    wa = hs[:, 3 * TOK_W:3 * TOK_W + LANES]
    gd = hs[:, 3 * TOK_W + LANES:]
    w = w0_ref[...] + _dot(jnp.tanh(wa).astype(BF16), w2_ref[...])
    w = -(jnp.maximum(-w, 0.0) + jnp.log(1.0 + jnp.exp(-jnp.abs(w)))) - 0.5
    log_decay = -jnp.exp(w)
    a = _sigmoid(a0_ref[...] + _dot(wa.astype(BF16), a2_ref[...]))
    g_ref[0] = _dot(_sigmoid(gd).astype(BF16), g2_ref[...])
    kk = k * kk_ref[...]
    ss_hi, ss_lo = _split2(kk * kk)
    norm2 = _dot(ss_hi, head_ones_ref[...]) + _dot(ss_lo, head_ones_ref[...])
    kk = kk / jnp.maximum(jnp.sqrt(norm2), 1e-12)
    k = k * (1.0 + (a - 1.0) * ka_ref[...])
    bonus_ref[0] = r * k * rk_ref[...]
    v_ref[0] = v
    a_s = -kk
    b_s = kk * a
    same_chunk = _div_pow2(_iota((tm, tm), 0), CHUNK) == _div_pow2(_iota((tm, tm), 1), CHUNK)
    incl = (same_chunk & (_iota((tm, tm), 1) <= _iota((tm, tm), 0))).astype(BF16)
    d_hi, d_mid, d_lo = _split3(log_decay)
    cw = _dot(incl, d_hi) + _dot(incl, d_mid) + _dot(incl, d_lo)
    same = same_chunk.astype(BF16)
    cw_end = _dot(same, d_hi) + _dot(same, d_mid) + _dot(same, d_lo)
    e_in = jnp.exp(cw)
    e_out = jnp.exp(-cw)
    e_end = jnp.exp(cw_end - cw)
    rt_ref[0] = r * e_in
    at_ref[0] = a_s * jnp.exp(cw - log_decay)
    bt_ref[0] = b_s * e_out
    kt_ref[0] = k * e_out
    bh_ref[0] = b_s * e_end
    kh_ref[0] = k * e_end
    pc_ref[0] = jnp.exp(cw_end)


def _rwkv_in(x, w_in, mu, w0, w2, a0, a2, g2, k_k, k_a, r_k, tm):
    B, S, _ = x.shape
    in_w = w_in.shape[1]
    w2_pad = jnp.concatenate([w2, jnp.zeros((AAA_LORA, TOK_W), F32)], axis=0).astype(BF16)
    a2_pad = jnp.concatenate([jnp.zeros((DECAY_LORA, TOK_W), F32), a2], axis=0).astype(BF16)
    head = jnp.arange(TOK_W) // HEAD_DIM
    head_ones = (head[:, None] == head[None, :]).astype(BF16)
    vec = lambda t: t.reshape(1, -1)
    row_spec = lambda w: pl.BlockSpec((1, tm, w), lambda b, i: (b, i, 0))
    tok = jax.ShapeDtypeStruct((B, S, TOK_W), F32)
    return pl.pallas_call(
        _rwkv_in_kernel,
        grid=(B, S // tm),
        in_specs=[row_spec(D_MODEL), _full_spec((D_MODEL, in_w)), _full_spec((1, RWKV_SHIFT_W)),
                  _full_spec((1, TOK_W)), _full_spec((LANES, TOK_W)), _full_spec((1, TOK_W)),
                  _full_spec((LANES, TOK_W)), _full_spec((GATE_LORA, TOK_W)), _full_spec((1, TOK_W)),
                  _full_spec((1, TOK_W)), _full_spec((1, TOK_W)), _full_spec((TOK_W, TOK_W))],
        out_specs=[row_spec(TOK_W)] * 10 + [row_spec(MEM_W)],
        out_shape=[tok] * 10 + [jax.ShapeDtypeStruct((B, S, MEM_W), BF16)],
        scratch_shapes=[pltpu.VMEM((1, RWKV_SHIFT_W), F32)],
        compiler_params=_cparams(("arbitrary", "arbitrary")),
        name="rwkv_in",
    )(x, w_in.astype(BF16), vec(mu), vec(w0), w2_pad, vec(a0), a2_pad, g2.astype(BF16), vec(k_k),
      vec(k_a), vec(r_k), head_ones)


def _rwkv_scan_kernel(rt_ref, at_ref, bt_ref, kt_ref, bh_ref, kh_ref, v_ref, pc_ref, y_ref, state_ref,
                      *, n_chunks):
    @pl.when(pl.program_id(2) == 0)
    def _():
        state_ref[...] = jnp.zeros_like(state_ref)

    C = CHUNK
    lane_head = _div_pow2(_iota((2 * C, LANES), 1), HEAD_DIM)
    row_head = _div_pow2(_iota((2 * C, LANES), 0), C)
    keep = lane_head == row_head
    pos_r = _mod_pow2(_iota((2 * C, 2 * C), 0), C)
    pos_c = _mod_pow2(_iota((2 * C, 2 * C), 1), C)
    strict = pos_c < pos_r
    incl = pos_c <= pos_r
    eye = (_iota((2 * C, 2 * C), 0) == _iota((2 * C, 2 * C), 1)).astype(F32)

    def stack(ref, c):
        t = ref[0, c * C:(c + 1) * C, :]
        return jnp.where(keep, jnp.concatenate([t, t], axis=0), 0.0).astype(BF16)

    for c in range(n_chunks):
        a2, r2, b2, k2 = stack(at_ref, c), stack(rt_ref, c), stack(bt_ref, c), stack(kt_ref, c)
        bh2, kh2, v2 = stack(bh_ref, c), stack(kh_ref, c), stack(v_ref, c)
        ar = jnp.concatenate([a2, r2], axis=0)
        gram = _dot_nt(ar, jnp.concatenate([b2, k2], axis=0))
        n_ab = jnp.where(strict, gram[:2 * C, :2 * C], 0.0)
        a_ak = jnp.where(strict, gram[:2 * C, 2 * C:], 0.0)
        a_rb = jnp.where(incl, gram[2 * C:, :2 * C], 0.0)
        a_rk = jnp.where(incl, gram[2 * C:, 2 * C:], 0.0)
        inv = eye + n_ab
        power = n_ab
        for _ in range(int(math.log2(C)) - 1):
            pb = power.astype(BF16)
            power = _dot(pb, pb)
            inv = inv + _dot(inv.astype(BF16), power.astype(BF16))
        state = state_ref[...]
        from_state = _dot_nt(ar, state.astype(BF16))
        from_v = _dot(jnp.concatenate([a_ak, a_rk], axis=0).astype(BF16), v2)
        u2 = _dot(inv.astype(BF16), (from_state[:2 * C] + from_v[:2 * C]).astype(BF16))
        y2 = from_state[2 * C:] + from_v[2 * C:] + _dot(a_rb.astype(BF16), u2.astype(BF16))
        y_ref[0, c * C:(c + 1) * C, :] = y2[:C] + y2[C:]
        uv_t = jnp.concatenate([u2, v2.astype(F32)], axis=0).T.astype(BF16)
        state_ref[...] = state * pc_ref[0, c * C:c * C + 1, :] + _dot(
            uv_t, jnp.concatenate([bh2, kh2], axis=0))


def _rwkv_scan(rt, at, bt, kt, bh, kh, v, pc, rows):
    B, S, _ = v.shape
    spec = pl.BlockSpec((1, rows, LANES), lambda b, p, i: (b, i, p))
    return pl.pallas_call(
        functools.partial(_rwkv_scan_kernel, n_chunks=rows // CHUNK),
        grid=(B, N_HEAD_PAIRS, S // rows),
        in_specs=[spec] * 8,
        out_specs=spec,
        out_shape=jax.ShapeDtypeStruct((B, S, TOK_W), F32),
        scratch_shapes=[pltpu.VMEM((LANES, LANES), F32)],
        compiler_params=_cparams(("arbitrary", "arbitrary", "arbitrary")),
        name="rwkv_scan",
    )(rt, at, bt, kt, bh, kh, v, pc)


def _rwkv_post_kernel(y_ref, bonus_ref, v_ref, g_ref, lg_ref, lb_ref, head_mean_ref, o_ref):
    y = y_ref[0]
    hm = head_mean_ref[...]

    def head_mean(t):
        hi, lo = _split2(t)
        return _dot(hi, hm) + _dot(lo, hm)

    yc = y - head_mean(y)
    var = head_mean(yc * yc)
    yn = yc * lax.rsqrt(var + GN_EPS) * lg_ref[...] + lb_ref[...]
    bonus = head_mean(bonus_ref[0]) * float(HEAD_DIM)
    o_ref[0] = ((yn + bonus * v_ref[0]) * g_ref[0]).astype(BF16)


def _rwkv_post(y, bonus, v, g, lnx_g, lnx_b, tm):
    B, S, _ = y.shape
    head = jnp.arange(TOK_W) // HEAD_DIM
    head_mean = ((head[:, None] == head[None, :]).astype(F32) / HEAD_DIM).astype(BF16)
    row_spec = pl.BlockSpec((1, tm, TOK_W), lambda b, i: (b, i, 0))
    return pl.pallas_call(
        _rwkv_post_kernel,
        grid=(B, S // tm),
        in_specs=[row_spec] * 4 + [_full_spec((1, TOK_W)), _full_spec((1, TOK_W)),
                                   _full_spec((TOK_W, TOK_W))],
        out_specs=row_spec,
        out_shape=jax.ShapeDtypeStruct((B, S, TOK_W), BF16),
        compiler_params=_cparams(("arbitrary", "arbitrary")),
        name="rwkv_post",
    )(y, bonus, v, g, lnx_g.reshape(1, -1), lnx_b.reshape(1, -1), head_mean)


def _mix_out_kernel(tok_ref, qm_ref, km_ref, vm_ref, x_ref, wo_tok_ref, wo_mem_ref, g_ref, b_ref,
                    wr_hi_ref, wr_lo_ref, rb_ref, x1_ref, x1b_ref, gate_ref):
    qm = qm_ref[0]
    mem_out = None
    for h in range(N_MEM_HEADS):
        s = _dot_nt(qm, km_ref[0, h])
        e = jnp.exp(s - jnp.max(s, axis=1, keepdims=True))
        o = _dot(e.astype(BF16), vm_ref[0, h]) / jnp.sum(e, axis=1, keepdims=True)
        mem_out = o if mem_out is None else mem_out + o
    mixed = _dot(tok_ref[0], wo_tok_ref[...]) + _dot(mem_out.astype(BF16), wo_mem_ref[...])
    x1 = _layer_norm(DEEPNORM_ALPHA * x_ref[0] + mixed, g_ref[...], b_ref[...])
    x1_ref[0] = x1
    x1b_ref[0] = x1.astype(BF16)
    scores = _sigmoid(_dot3(x1, wr_hi_ref[...], wr_lo_ref[...]))
    lane = _iota(scores.shape, 1)
    lane_f = lane.astype(F32)
    cand = jnp.where(lane < N_EXPERTS, scores + rb_ref[...], -jnp.inf)
    picked = jnp.zeros(scores.shape, jnp.bool_)
    for _ in range(TOP_K):
        best = jnp.max(cand, axis=1, keepdims=True)
        first = jnp.min(jnp.where(cand == best, lane_f, float(LANES)), axis=1, keepdims=True)
        hit = lane_f == first
        picked = jnp.logical_or(picked, hit)
        cand = jnp.where(hit, -jnp.inf, cand)
    chosen = jnp.where(picked, scores, 0.0)
    gate = chosen / jnp.sum(chosen, axis=1, keepdims=True) * ROUTED_SCALE
    gate_ref[0] = jnp.where(lane == N_EXPERTS, 1.0, gate)


def _mix_out(tok, qm, km, vm, x, w_out, ln_g, ln_b, w_router, router_bias, tm):
    B, S, _ = x.shape
    n_mem = km.shape[2]
    wr = jnp.pad(w_router, ((0, 0), (0, LANES - N_EXPERTS)))
    wr_hi = wr.astype(BF16)
    wr_lo = (wr - wr_hi.astype(F32)).astype(BF16)
    rb = jnp.pad(router_bias, (0, LANES - N_EXPERTS)).reshape(1, LANES)
    row_spec = lambda w: pl.BlockSpec((1, tm, w), lambda b, i: (b, i, 0))
    mem_spec = pl.BlockSpec((1, N_MEM_HEADS, n_mem, MEM_W), lambda b, i: (b, 0, 0, 0))
    return pl.pallas_call(
        _mix_out_kernel,
        grid=(B, S // tm),
        in_specs=[row_spec(TOK_W), row_spec(MEM_W), mem_spec, mem_spec, row_spec(D_MODEL),
                  _full_spec((TOK_W, D_MODEL)), _full_spec((MEM_W, D_MODEL)),
                  _full_spec((1, D_MODEL)), _full_spec((1, D_MODEL)),
                  _full_spec((D_MODEL, LANES)), _full_spec((D_MODEL, LANES)), _full_spec((1, LANES))],
        out_specs=[row_spec(D_MODEL), row_spec(D_MODEL), row_spec(LANES)],
        out_shape=[jax.ShapeDtypeStruct((B, S, D_MODEL), F32), jax.ShapeDtypeStruct((B, S, D_MODEL), BF16),
                   jax.ShapeDtypeStruct((B, S, LANES), F32)],
        compiler_params=_cparams(("arbitrary", "arbitrary")),
        name="mix_out",
    )(tok, qm, km, vm, x, w_out[:TOK_W].astype(BF16), w_out[TOK_W:].astype(BF16),
      ln_g.reshape(1, -1), ln_b.reshape(1, -1), wr_hi, wr_lo, rb)


def _moe_kernel(xb_ref, x_ref, gate_ref, wg_ref, wu_ref, wd_ref, g_ref, b_ref, o_ref, acc_ref):
    e = pl.program_id(1)

    @pl.when(e == 0)
    def _():
        acc_ref[...] = jnp.zeros_like(acc_ref)

    xb = xb_ref[...]
    gates = gate_ref[...]
    gate = jnp.sum(jnp.where(_iota(gates.shape, 1) == e, gates, 0.0), axis=1, keepdims=True)
    hg = _dot(xb, wg_ref[0])
    hu = _dot(xb, wu_ref[0])
    hidden = hg * _sigmoid(hg) * hu
    hidden = jnp.where(gate > 0.0, hidden * gate, 0.0)
    acc_ref[...] += _dot(hidden.astype(BF16), wd_ref[0])

    @pl.when(e == pl.num_programs(1) - 1)
    def _():
        o_ref[...] = _layer_norm(DEEPNORM_ALPHA * x_ref[...] + acc_ref[...], g_ref[...], b_ref[...])


def _moe(xb, x, gates, w_gate, w_up, w_down, ln_g, ln_b, tm):
    T = x.shape[0]
    n_e = w_gate.shape[0]
    row_spec = lambda w: pl.BlockSpec((tm, w), lambda i, e: (i, 0))
    return pl.pallas_call(
        _moe_kernel,
        grid=(T // tm, n_e),
        in_specs=[row_spec(D_MODEL), row_spec(D_MODEL), row_spec(LANES),
                  pl.BlockSpec((1, D_MODEL, D_EXPERT), lambda i, e: (e, 0, 0)),
                  pl.BlockSpec((1, D_MODEL, D_EXPERT), lambda i, e: (e, 0, 0)),
                  pl.BlockSpec((1, D_EXPERT, D_MODEL), lambda i, e: (e, 0, 0)),
                  _full_spec((1, D_MODEL)), _full_spec((1, D_MODEL))],
        out_specs=row_spec(D_MODEL),
        out_shape=jax.ShapeDtypeStruct((T, D_MODEL), F32),
        scratch_shapes=[pltpu.VMEM((tm, D_MODEL), F32)],
        compiler_params=_cparams(("arbitrary", "arbitrary")),
        name="moe",
    )(xb, x, gates, w_gate, w_up, w_down, ln_g.reshape(1, -1), ln_b.reshape(1, -1))


def _tile(n, want):
    t = min(n, want)
    assert n % t == 0, (n, t)
    return t


def kernel(x, mem, mem_ln_g, mem_ln_b, w_mem_kv, fox_w_in, fox_b_f, rwkv_w_in, rwkv_mu, rwkv_w0, rwkv_w2,
           rwkv_a0, rwkv_a2, rwkv_g2, rwkv_k_k, rwkv_k_a, rwkv_r_k, rwkv_lnx_g, rwkv_lnx_b, w_out, ln1_g,
           ln1_b, w_router, router_bias, w_exp_gate, w_exp_up, w_exp_down, w_sh_gate, w_sh_up, w_sh_down,
           ln2_g, ln2_b):
    B, S, D = x.shape
    assert D == D_MODEL and S % CHUNK == 0
    T = B * S
    t_proj = _tile(S, 512)
    t_attn = _tile(S, 512)
    t_rwkv = _tile(S, 256)
    t_moe = _tile(T, 1024)

    km, vm = _mem_kv(mem, mem_ln_g, mem_ln_b, w_mem_kv)
    for i in range(DEPTH):
        j = i // 2
        if i % 2 == 0:
            q_aug, k_aug, v, g, qm = _fox_in(x, fox_w_in[j], fox_b_f[j], t_proj)
            tok = _fox_attn(q_aug, k_aug, v, g, t_attn)
        else:
            (rt, at, bt, kt, bh, kh, v, pc, bonus, g, qm) = _rwkv_in(
                x, rwkv_w_in[j], rwkv_mu[j], rwkv_w0[j], rwkv_w2[j], rwkv_a0[j], rwkv_a2[j], rwkv_g2[j],
                rwkv_k_k[j], rwkv_k_a[j], rwkv_r_k[j].reshape(-1), t_rwkv)
            y = _rwkv_scan(rt, at, bt, kt, bh, kh, v, pc, t_rwkv)
            tok = _rwkv_post(y, bonus, v, g, rwkv_lnx_g[j], rwkv_lnx_b[j], t_proj)
        x1, x1b, gates = _mix_out(tok, qm, km, vm, x, w_out[i], ln1_g[i], ln1_b[i], w_router[i],
                                  router_bias[i], t_proj)
        wg = jnp.concatenate([w_exp_gate[i], w_sh_gate[i][None]], axis=0).astype(BF16)
        wu = jnp.concatenate([w_exp_up[i], w_sh_up[i][None]], axis=0).astype(BF16)
        wd = jnp.concatenate([w_exp_down[i], w_sh_down[i][None]], axis=0).astype(BF16)
        x = _moe(x1b.reshape(T, D), x1.reshape(T, D), gates.reshape(T, LANES), wg, wu, wd,
                 ln2_g[i], ln2_b[i], t_moe).reshape(B, S, D)
    return x
```

```python
import functools
import math

import jax
import jax.numpy as jnp
from jax import lax
from jax.experimental import pallas as pl
from jax.experimental.pallas import tpu as pltpu

F32 = jnp.float32
BF16 = jnp.bfloat16

D_MODEL = 1024
HEAD_DIM = 64
N_TOK_HEADS = 12
TOK_W = N_TOK_HEADS * HEAD_DIM
N_HEAD_PAIRS = N_TOK_HEADS // 2
N_MEM_HEADS = 4
MEM_W = N_MEM_HEADS * HEAD_DIM
DECAY_LORA = 64
AAA_LORA = 64
GATE_LORA = 128
RWKV_SHIFT_W = 3 * TOK_W + DECAY_LORA + AAA_LORA + GATE_LORA
N_EXPERTS = 64
TOP_K = 6
D_EXPERT = 256
ROUTED_SCALE = 2.5
DEPTH = 2
DEEPNORM_ALPHA = (2 * DEPTH) ** 0.25
LN_EPS = 1e-5
GN_EPS = 64e-5
LOG2E = math.log2(math.e)

LANES = 128
AUG_W = 2 * HEAD_DIM
CHUNK = 64
VMEM_LIMIT = 56 * 1024 * 1024


def _cparams(sem):
    return pltpu.CompilerParams(dimension_semantics=sem, vmem_limit_bytes=VMEM_LIMIT)


def _dot(a, b):
    return jnp.dot(a, b, preferred_element_type=F32)


def _dot_nt(a, b):
    return lax.dot_general(a, b, (((1,), (1,)), ((), ())), preferred_element_type=F32)


def _split2(x):
    hi = x.astype(BF16)
    lo = (x - hi.astype(F32)).astype(BF16)
    return hi, lo


def _split3(x):
    hi = x.astype(BF16)
    r1 = x - hi.astype(F32)
    mid = r1.astype(BF16)
    lo = (r1 - mid.astype(F32)).astype(BF16)
    return hi, mid, lo


def _dot_sel(sel, x):
    hi, mid, lo = _split3(x)
    return _dot(sel, hi) + _dot(sel, mid) + _dot(sel, lo)


def _dot3(x, w_hi, w_lo):
    x_hi, x_lo = _split2(x)
    return _dot(x_hi, w_hi) + _dot(x_lo, w_hi) + _dot(x_hi, w_lo)


def _sigmoid(x):
    return 1.0 / (1.0 + jnp.exp(-x))


def _layer_norm(y, g, b):
    mu = jnp.mean(y, axis=-1, keepdims=True)
    yc = y - mu
    var = jnp.mean(yc * yc, axis=-1, keepdims=True)
    return yc * lax.rsqrt(var + LN_EPS) * g + b


def _iota(shape, dim):
    return lax.broadcasted_iota(jnp.int32, shape, dim)


def _div_pow2(x, n):
    return jnp.right_shift(x, int(math.log2(n)))


def _mod_pow2(x, n):
    return jnp.bitwise_and(x, n - 1)


def _full_spec(shape):
    n = len(shape)
    return pl.BlockSpec(shape, lambda *_: (0,) * n)


def _mem_kv_kernel(mem_ref, g_ref, b_ref, w_ref, k_ref, v_ref):
    m = _layer_norm(mem_ref[0], g_ref[...], b_ref[...])
    kv = _dot(m.astype(BF16), w_ref[...])
    k = kv[:, :MEM_W]
    v = kv[:, MEM_W:]
    head = _div_pow2(_iota(k.shape, 1), HEAD_DIM)
    for h in range(N_MEM_HEADS):
        k_ref[0, h] = jnp.where(head == h, k, 0.0).astype(BF16)
        v_ref[0, h] = jnp.where(head == h, v, 0.0).astype(BF16)


def _mem_kv(mem, g, b, w):
    B, n_mem, _ = mem.shape
    out = jax.ShapeDtypeStruct((B, N_MEM_HEADS, n_mem, MEM_W), BF16)
    out_spec = pl.BlockSpec((1, N_MEM_HEADS, n_mem, MEM_W), lambda i: (i, 0, 0, 0))
    return pl.pallas_call(
        _mem_kv_kernel,
        grid=(B,),
        in_specs=[pl.BlockSpec((1, n_mem, D_MODEL), lambda i: (i, 0, 0)),
                  _full_spec((1, D_MODEL)), _full_spec((1, D_MODEL)),
                  _full_spec((D_MODEL, 2 * MEM_W))],
        out_specs=[out_spec, out_spec],
        out_shape=[out, out],
        compiler_params=_cparams(("arbitrary",)),
        name="mem_kv",
    )(mem, g.reshape(1, -1), b.reshape(1, -1), w.astype(BF16))


def _fox_in_kernel(x_ref, wq_ref, wk_ref, wv_ref, wg_ref, wfh_ref, wfl_ref, wm_ref, bf_ref,
                   qc_ref, place_ref, q_ref, k_ref, v_ref, g_ref, qm_ref, carry_ref):
    @pl.when(pl.program_id(1) == 0)
    def _():
        carry_ref[...] = jnp.zeros_like(carry_ref)

    x = x_ref[0]
    xb = x.astype(BF16)
    tm = x.shape[0]
    z = _dot3(x, wfh_ref[...], wfl_ref[...]) + bf_ref[...]
    log_f = jnp.minimum(z, 0.0) - jnp.log(1.0 + jnp.exp(-jnp.abs(z)))
    tril = (_iota((tm, tm), 1) <= _iota((tm, tm), 0)).astype(BF16)
    c = _dot_sel(tril, log_f) + carry_ref[...]
    carry_ref[...] = c[tm - 1:tm, :]
    c_hi, c_mid, c_lo = _split3(c * LOG2E)
    c_parts = jnp.concatenate([c_hi, c_mid, c_lo], axis=1)
    q_ref[0] = (_dot(xb, wq_ref[...]) + qc_ref[...]).astype(BF16)
    k_ref[0] = (_dot(xb, wk_ref[...]) + _dot(c_parts, place_ref[...])).astype(BF16)
    v_ref[0] = _dot(xb, wv_ref[...]).astype(BF16)
    g_ref[0] = _sigmoid(_dot(xb, wg_ref[...])).astype(BF16)
    qm_ref[0] = _dot(xb, wm_ref[...]).astype(BF16)


def _fox_in(x, w_in, b_f, tm):
    B, S, _ = x.shape
    scale = HEAD_DIM ** -0.5
    wq, wk, wv, wg, wf, wm = jnp.split(
        w_in, [TOK_W, 2 * TOK_W, 3 * TOK_W, 4 * TOK_W, 4 * TOK_W + N_TOK_HEADS], axis=1)

    def per_head_pad(w):
        w = w.reshape(D_MODEL, N_TOK_HEADS, HEAD_DIM)
        return jnp.pad(w, ((0, 0), (0, 0), (0, AUG_W - HEAD_DIM))).reshape(D_MODEL, N_TOK_HEADS * AUG_W)

    wq_aug = per_head_pad(wq * (scale * LOG2E)).astype(BF16)
    wk_aug = per_head_pad(wk).astype(BF16)
    wf_pad = jnp.pad(wf, ((0, 0), (0, LANES - N_TOK_HEADS)))
    wf_hi = wf_pad.astype(BF16)
    wf_lo = (wf_pad - wf_hi.astype(F32)).astype(BF16)
    bf_pad = jnp.pad(b_f, (0, LANES - N_TOK_HEADS)).reshape(1, LANES)
    col = jnp.arange(N_TOK_HEADS * AUG_W)
    is_c = (col % AUG_W >= HEAD_DIM) & (col % AUG_W < HEAD_DIM + 3)
    q_const = jnp.where(is_c, -1.0, 0.0).astype(F32).reshape(1, -1)
    row = jnp.arange(3 * LANES)
    place = ((row[:, None] % LANES == col[None, :] // AUG_W)
             & (col[None, :] % AUG_W == HEAD_DIM + row[:, None] // LANES)).astype(BF16)

    aug_w = N_TOK_HEADS * AUG_W
    row_spec = lambda w: pl.BlockSpec((1, tm, w), lambda b, i: (b, i, 0))
    return pl.pallas_call(
        _fox_in_kernel,
        grid=(B, S // tm),
        in_specs=[row_spec(D_MODEL),
                  _full_spec((D_MODEL, aug_w)), _full_spec((D_MODEL, aug_w)),
                  _full_spec((D_MODEL, TOK_W)), _full_spec((D_MODEL, TOK_W)),
                  _full_spec((D_MODEL, LANES)), _full_spec((D_MODEL, LANES)),
                  _full_spec((D_MODEL, MEM_W)), _full_spec((1, LANES)),
                  _full_spec((1, aug_w)), _full_spec((3 * LANES, aug_w))],
        out_specs=[row_spec(aug_w), row_spec(aug_w), row_spec(TOK_W), row_spec(TOK_W), row_spec(MEM_W)],
        out_shape=[jax.ShapeDtypeStruct((B, S, aug_w), BF16), jax.ShapeDtypeStruct((B, S, aug_w), BF16),
                   jax.ShapeDtypeStruct((B, S, TOK_W), BF16), jax.ShapeDtypeStruct((B, S, TOK_W), BF16),
                   jax.ShapeDtypeStruct((B, S, MEM_W), BF16)],
        scratch_shapes=[pltpu.VMEM((1, LANES), F32)],
        compiler_params=_cparams(("arbitrary", "arbitrary")),
        name="fox_in",
    )(x, wq_aug, wk_aug, wv.astype(BF16), wg.astype(BF16), wf_hi, wf_lo,
      (wm * scale).astype(BF16), bf_pad, q_const, place)


def _fox_attn_kernel(q_ref, k_ref, v_ref, g_ref, o_ref, sa_ref, sb_ref, *, tq):
    qi = pl.program_id(2)
    qs = [q_ref[0, :, h * AUG_W:(h + 1) * AUG_W] for h in range(2)]

    def scores(kj, s_ref):
        start = pl.multiple_of(kj * tq, tq)
        for h in range(2):
            s_ref[h] = _dot_nt(qs[h], k_ref[0, pl.ds(start, tq), h * AUG_W:(h + 1) * AUG_W])

    def absorb(kj, s_ref, carry, masked):
        start = pl.multiple_of(kj * tq, tq)
        v = v_ref[0, pl.ds(start, tq), :]
        out = []
        for h in range(2):
            m, l, acc = carry[h]
            s = s_ref[h]
            if masked:
                s = jnp.where(_iota((tq, tq), 1) <= _iota((tq, tq), 0), s, -jnp.inf)
            m_new = jnp.maximum(m, jnp.max(s, axis=1, keepdims=True))
            p = jnp.exp2(s - m_new)
            alpha = jnp.exp2(m - m_new)
            p_lanes = functools.reduce(
                jnp.add, [p[:, i * LANES:(i + 1) * LANES] for i in range(tq // LANES)])
            l = alpha * l + p_lanes
            acc = alpha * acc + _dot(p.astype(BF16), v)
            out.append((m_new, l, acc))
        return tuple(out)

    def pair(i, carry):
        scores(2 * i + 1, sb_ref)
        carry = absorb(2 * i, sa_ref, carry, masked=False)
        scores(2 * i + 2, sa_ref)
        return absorb(2 * i + 1, sb_ref, carry, masked=False)

    def odd_tail(carry):
        scores(qi, sb_ref)
        carry = absorb(qi - 1, sa_ref, carry, masked=False)
        return absorb(qi, sb_ref, carry, masked=True)

    def even_tail(carry):
        return absorb(qi, sa_ref, carry, masked=True)

    init = (jnp.full((tq, 1), -jnp.inf, F32), jnp.zeros((tq, LANES), F32), jnp.zeros((tq, LANES), F32))
    scores(0, sa_ref)
    carry = lax.fori_loop(0, jnp.right_shift(qi, 1), pair, (init, init))
    (_, l0, acc0), (_, l1, acc1) = lax.cond(jnp.bitwise_and(qi, 1) == 1, odd_tail, even_tail, carry)
    o0 = acc0 / jnp.sum(l0, axis=1, keepdims=True)
    o1 = acc1 / jnp.sum(l1, axis=1, keepdims=True)
    o = jnp.where(_iota((tq, LANES), 1) < HEAD_DIM, o0, o1)
    o_ref[0] = (o * g_ref[0].astype(F32)).astype(BF16)


def _fox_attn(q_aug, k_aug, v, g, tq):
    B, S, _ = v.shape
    return pl.pallas_call(
        functools.partial(_fox_attn_kernel, tq=tq),
        grid=(B, N_HEAD_PAIRS, S // tq),
        in_specs=[pl.BlockSpec((1, tq, 2 * AUG_W), lambda b, p, i: (b, i, p)),
                  pl.BlockSpec((1, S, 2 * AUG_W), lambda b, p, i: (b, 0, p)),
                  pl.BlockSpec((1, S, LANES), lambda b, p, i: (b, 0, p)),
                  pl.BlockSpec((1, tq, LANES), lambda b, p, i: (b, i, p))],
        out_specs=pl.BlockSpec((1, tq, LANES), lambda b, p, i: (b, i, p)),
        out_shape=jax.ShapeDtypeStruct((B, S, TOK_W), BF16),
        scratch_shapes=[pltpu.VMEM((2, tq, tq), F32), pltpu.VMEM((2, tq, tq), F32)],
        compiler_params=_cparams(("arbitrary", "arbitrary", "arbitrary")),
        name="fox_attn",
    )(q_aug, k_aug, v, g)


def _rwkv_in_kernel(x_ref, w_ref, mu_ref, w0_ref, w2_ref, a0_ref, a2_ref, g2_ref, kk_ref, ka_ref,
                    rk_ref, head_ones_ref,
                    rt_ref, at_ref, bt_ref, kt_ref, bh_ref, kh_ref, v_ref, pc_ref, bonus_ref, g_ref,
                    qm_ref, prev_ref):
    @pl.when(pl.program_id(1) == 0)
    def _():
        prev_ref[...] = jnp.zeros_like(prev_ref)

    tm = x_ref.shape[1]
    h = _dot(x_ref[0].astype(BF16), w_ref[...])
    qm_ref[0] = (h[:, RWKV_SHIFT_W:] * (HEAD_DIM ** -0.5)).astype(BF16)
    hs = h[:, :RWKV_SHIFT_W]
    row = _iota(hs.shape, 0)
    shifted = jnp.where(row == 0, prev_ref[...], pltpu.roll(hs, 1, 0))
    prev_ref[...] = hs[tm - 1:tm, :]
    hs = hs + mu_ref[...] * (shifted - hs)
    r = hs[:, :TOK_W]
    k = hs[:, TOK_W:2 * TOK_W]
    v = hs[:, 2 * TOK_W:3 * TOK_W]
    wa = hs[:, 3 * TOK_W:3 * TOK_W + LANES]
    gd = hs[:, 3 * TOK_W + LANES:]
    w = w0_ref[...] + _dot(jnp.tanh(wa).astype(BF16), w2_ref[...])
    w = -(jnp.maximum(-w, 0.0) + jnp.log(1.0 + jnp.exp(-jnp.abs(w)))) - 0.5
    log_decay = -jnp.exp(w)
    a = _sigmoid(a0_ref[...] + _dot(wa.astype(BF16), a2_ref[...]))
    g_ref[0] = _dot(_sigmoid(gd).astype(BF16), g2_ref[...])
    kk = k * kk_ref[...]
    ss_hi, ss_lo = _split2(kk * kk)
    norm2 = _dot(ss_hi, head_ones_ref[...]) + _dot(ss_lo, head_ones_ref[...])
    kk = kk / jnp.maximum(jnp.sqrt(norm2), 1e-12)
    k = k * (1.0 + (a - 1.0) * ka_ref[...])
    bonus_ref[0] = r * k * rk_ref[...]
    v_ref[0] = v
    a_s = -kk
    b_s = kk * a
    same_chunk = _div_pow2(_iota((tm, tm), 0), CHUNK) == _div_pow2(_iota((tm, tm), 1), CHUNK)
    incl = (same_chunk & (_iota((tm, tm), 1) <= _iota((tm, tm), 0))).astype(BF16)
    d_hi, d_mid, d_lo = _split3(log_decay)
    cw = _dot(incl, d_hi) + _dot(incl, d_mid) + _dot(incl, d_lo)
    same = same_chunk.astype(BF16)
    cw_end = _dot(same, d_hi) + _dot(same, d_mid) + _dot(same, d_lo)
    e_in = jnp.exp(cw)
    e_out = jnp.exp(-cw)
    e_end = jnp.exp(cw_end - cw)
    rt_ref[0] = r * e_in
    at_ref[0] = a_s * jnp.exp(cw - log_decay)
    bt_ref[0] = b_s * e_out
    kt_ref[0] = k * e_out
    bh_ref[0] = b_s * e_end
    kh_ref[0] = k * e_end
    pc_ref[0] = jnp.exp(cw_end)


def _rwkv_in(x, w_in, mu, w0, w2, a0, a2, g2, k_k, k_a, r_k, tm):
    B, S, _ = x.shape
    in_w = w_in.shape[1]
    w2_pad = jnp.concatenate([w2, jnp.zeros((AAA_LORA, TOK_W), F32)], axis=0).astype(BF16)
    a2_pad = jnp.concatenate([jnp.zeros((DECAY_LORA, TOK_W), F32), a2], axis=0).astype(BF16)
    head = jnp.arange(TOK_W) // HEAD_DIM
    head_ones = (head[:, None] == head[None, :]).astype(BF16)
    vec = lambda t: t.reshape(1, -1)
    row_spec = lambda w: pl.BlockSpec((1, tm, w), lambda b, i: (b, i, 0))
    tok = jax.ShapeDtypeStruct((B, S, TOK_W), F32)
    return pl.pallas_call(
        _rwkv_in_kernel,
        grid=(B, S // tm),
        in_specs=[row_spec(D_MODEL), _full_spec((D_MODEL, in_w)), _full_spec((1, RWKV_SHIFT_W)),
                  _full_spec((1, TOK_W)), _full_spec((LANES, TOK_W)), _full_spec((1, TOK_W)),
                  _full_spec((LANES, TOK_W)), _full_spec((GATE_LORA, TOK_W)), _full_spec((1, TOK_W)),
                  _full_spec((1, TOK_W)), _full_spec((1, TOK_W)), _full_spec((TOK_W, TOK_W))],
        out_specs=[row_spec(TOK_W)] * 10 + [row_spec(MEM_W)],
        out_shape=[tok] * 10 + [jax.ShapeDtypeStruct((B, S, MEM_W), BF16)],
        scratch_shapes=[pltpu.VMEM((1, RWKV_SHIFT_W), F32)],
        compiler_params=_cparams(("arbitrary", "arbitrary")),
        name="rwkv_in",
    )(x, w_in.astype(BF16), vec(mu), vec(w0), w2_pad, vec(a0), a2_pad, g2.astype(BF16), vec(k_k),
      vec(k_a), vec(r_k), head_ones)


def _rwkv_scan_kernel(rt_ref, at_ref, bt_ref, kt_ref, bh_ref, kh_ref, v_ref, pc_ref, y_ref, state_ref,
                      *, n_chunks):
    @pl.when(pl.program_id(2) == 0)
    def _():
        state_ref[...] = jnp.zeros_like(state_ref)

    C = CHUNK
    lane_head = _div_pow2(_iota((2 * C, LANES), 1), HEAD_DIM)
    row_head = _div_pow2(_iota((2 * C, LANES), 0), C)
    keep = lane_head == row_head
    pos_r = _mod_pow2(_iota((2 * C, 2 * C), 0), C)
    pos_c = _mod_pow2(_iota((2 * C, 2 * C), 1), C)
    strict = pos_c < pos_r
    incl = pos_c <= pos_r
    eye = (_iota((2 * C, 2 * C), 0) == _iota((2 * C, 2 * C), 1)).astype(F32)

    def stack(ref, c):
        t = ref[0, c * C:(c + 1) * C, :]
        return jnp.where(keep, jnp.concatenate([t, t], axis=0), 0.0).astype(BF16)

    chunks = range(n_chunks)
    a2 = [stack(at_ref, c) for c in chunks]
    r2 = [stack(rt_ref, c) for c in chunks]
    bh2 = [stack(bh_ref, c) for c in chunks]
    kh2 = [stack(kh_ref, c) for c in chunks]
    v2 = [stack(v_ref, c) for c in chunks]
    gram = [_dot_nt(jnp.concatenate([a2[c], r2[c]], axis=0),
                    jnp.concatenate([stack(bt_ref, c), stack(kt_ref, c)], axis=0)) for c in chunks]
    power = [jnp.where(strict, gram[c][:2 * C, :2 * C], 0.0) for c in chunks]
    inv = [eye + power[c] for c in chunks]
    for _ in range(int(math.log2(C)) - 1):
        power = [_dot(p.astype(BF16), p.astype(BF16)) for p in power]
        inv = [inv[c] + _dot(inv[c].astype(BF16), power[c].astype(BF16)) for c in chunks]
    from_v = [_dot(jnp.concatenate([jnp.where(strict, gram[c][:2 * C, 2 * C:], 0.0),
                                    jnp.where(incl, gram[c][2 * C:, 2 * C:], 0.0)], axis=0).astype(BF16),
                   v2[c]) for c in chunks]
    wu = [_dot(inv[c].astype(BF16),
               jnp.concatenate([a2[c], from_v[c][:2 * C].astype(BF16)], axis=1)) for c in chunks]
    wy = [_dot(jnp.where(incl, gram[c][2 * C:, :2 * C], 0.0).astype(BF16), wu[c].astype(BF16))
          for c in chunks]
    w_y = [(r2[c].astype(F32) + wy[c][:, :LANES]).astype(BF16) for c in chunks]
    y0 = [from_v[c][2 * C:] + wy[c][:, LANES:] for c in chunks]
    gc = [_dot(wu[c].T.astype(BF16), bh2[c]) for c in chunks]
    c0 = [gc[c][LANES:] + _dot(v2[c].astype(F32).T.astype(BF16), kh2[c]) for c in chunks]
    state = state_ref[...]
    for c in chunks:
        sb = state.astype(BF16)
        y2 = _dot_nt(w_y[c], sb) + y0[c]
        y_ref[0, c * C:(c + 1) * C, :] = y2[:C] + y2[C:]
        state = state * pc_ref[0, c * C:c * C + 1, :] + _dot(sb, gc[c][:LANES].astype(BF16)) + c0[c]
    state_ref[...] = state


def _rwkv_scan(rt, at, bt, kt, bh, kh, v, pc, rows):
    B, S, _ = v.shape
    spec = pl.BlockSpec((1, rows, LANES), lambda b, p, i: (b, i, p))
    return pl.pallas_call(
        functools.partial(_rwkv_scan_kernel, n_chunks=rows // CHUNK),
        grid=(B, N_HEAD_PAIRS, S // rows),
        in_specs=[spec] * 8,
        out_specs=spec,
        out_shape=jax.ShapeDtypeStruct((B, S, TOK_W), F32),
        scratch_shapes=[pltpu.VMEM((LANES, LANES), F32)],
        compiler_params=_cparams(("arbitrary", "arbitrary", "arbitrary")),
        name="rwkv_scan",
    )(rt, at, bt, kt, bh, kh, v, pc)


def _rwkv_post_kernel(y_ref, bonus_ref, v_ref, g_ref, lg_ref, lb_ref, head_mean_ref, o_ref):
    y = y_ref[0]
    hm = head_mean_ref[...]

    def head_mean(t):
        hi, lo = _split2(t)
        return _dot(hi, hm) + _dot(lo, hm)

    yc = y - head_mean(y)
    var = head_mean(yc * yc)
    yn = yc * lax.rsqrt(var + GN_EPS) * lg_ref[...] + lb_ref[...]
    bonus = head_mean(bonus_ref[0]) * float(HEAD_DIM)
    o_ref[0] = ((yn + bonus * v_ref[0]) * g_ref[0]).astype(BF16)


def _rwkv_post(y, bonus, v, g, lnx_g, lnx_b, tm):
    B, S, _ = y.shape
    head = jnp.arange(TOK_W) // HEAD_DIM
    head_mean = ((head[:, None] == head[None, :]).astype(F32) / HEAD_DIM).astype(BF16)
    row_spec = pl.BlockSpec((1, tm, TOK_W), lambda b, i: (b, i, 0))
    return pl.pallas_call(
        _rwkv_post_kernel,
        grid=(B, S // tm),
        in_specs=[row_spec] * 4 + [_full_spec((1, TOK_W)), _full_spec((1, TOK_W)),
                                   _full_spec((TOK_W, TOK_W))],
        out_specs=row_spec,
        out_shape=jax.ShapeDtypeStruct((B, S, TOK_W), BF16),
        compiler_params=_cparams(("arbitrary", "arbitrary")),
        name="rwkv_post",
    )(y, bonus, v, g, lnx_g.reshape(1, -1), lnx_b.reshape(1, -1), head_mean)


def _mix_out_kernel(tok_ref, qm_ref, km_ref, vm_ref, x_ref, wo_tok_ref, wo_mem_ref, g_ref, b_ref,
                    wr_hi_ref, wr_lo_ref, rb_ref, x1_ref, x1b_ref, gate_ref):
    qm = qm_ref[0]
    mem_out = None
    for h in range(N_MEM_HEADS):
        s = _dot_nt(qm, km_ref[0, h])
        e = jnp.exp(s - jnp.max(s, axis=1, keepdims=True))
        o = _dot(e.astype(BF16), vm_ref[0, h]) / jnp.sum(e, axis=1, keepdims=True)
        mem_out = o if mem_out is None else mem_out + o
    mixed = _dot(tok_ref[0], wo_tok_ref[...]) + _dot(mem_out.astype(BF16), wo_mem_ref[...])
    x1 = _layer_norm(DEEPNORM_ALPHA * x_ref[0] + mixed, g_ref[...], b_ref[...])
    x1_ref[0] = x1
    x1b_ref[0] = x1.astype(BF16)
    scores = _sigmoid(_dot3(x1, wr_hi_ref[...], wr_lo_ref[...]))
    lane = _iota(scores.shape, 1)
    lane_f = lane.astype(F32)
    cand = jnp.where(lane < N_EXPERTS, scores + rb_ref[...], -jnp.inf)
    picked = jnp.zeros(scores.shape, jnp.bool_)
    for _ in range(TOP_K):
        best = jnp.max(cand, axis=1, keepdims=True)
        first = jnp.min(jnp.where(cand == best, lane_f, float(LANES)), axis=1, keepdims=True)
        hit = lane_f == first
        picked = jnp.logical_or(picked, hit)
        cand = jnp.where(hit, -jnp.inf, cand)
    chosen = jnp.where(picked, scores, 0.0)
    gate = chosen / jnp.sum(chosen, axis=1, keepdims=True) * ROUTED_SCALE
    gate_ref[0] = jnp.where(lane == N_EXPERTS, 1.0, gate)


def _mix_out(tok, qm, km, vm, x, w_out, ln_g, ln_b, w_router, router_bias, tm):
    B, S, _ = x.shape
    n_mem = km.shape[2]
    wr = jnp.pad(w_router, ((0, 0), (0, LANES - N_EXPERTS)))
    wr_hi = wr.astype(BF16)
    wr_lo = (wr - wr_hi.astype(F32)).astype(BF16)
    rb = jnp.pad(router_bias, (0, LANES - N_EXPERTS)).reshape(1, LANES)
    row_spec = lambda w: pl.BlockSpec((1, tm, w), lambda b, i: (b, i, 0))
    mem_spec = pl.BlockSpec((1, N_MEM_HEADS, n_mem, MEM_W), lambda b, i: (b, 0, 0, 0))
    return pl.pallas_call(
        _mix_out_kernel,
        grid=(B, S // tm),
        in_specs=[row_spec(TOK_W), row_spec(MEM_W), mem_spec, mem_spec, row_spec(D_MODEL),
                  _full_spec((TOK_W, D_MODEL)), _full_spec((MEM_W, D_MODEL)),
                  _full_spec((1, D_MODEL)), _full_spec((1, D_MODEL)),
                  _full_spec((D_MODEL, LANES)), _full_spec((D_MODEL, LANES)), _full_spec((1, LANES))],
        out_specs=[row_spec(D_MODEL), row_spec(D_MODEL), row_spec(LANES)],
        out_shape=[jax.ShapeDtypeStruct((B, S, D_MODEL), F32), jax.ShapeDtypeStruct((B, S, D_MODEL), BF16),
                   jax.ShapeDtypeStruct((B, S, LANES), F32)],
        compiler_params=_cparams(("arbitrary", "arbitrary")),
        name="mix_out",
    )(tok, qm, km, vm, x, w_out[:TOK_W].astype(BF16), w_out[TOK_W:].astype(BF16),
      ln_g.reshape(1, -1), ln_b.reshape(1, -1), wr_hi, wr_lo, rb)


def _moe_kernel(xb_ref, x_ref, gate_ref, wg_ref, wu_ref, wd_ref, g_ref, b_ref, o_ref, acc_ref):
    e = pl.program_id(1)

    @pl.when(e == 0)
    def _():
        acc_ref[...] = jnp.zeros_like(acc_ref)

    xb = xb_ref[...]
    gates = gate_ref[...]
    gate = jnp.sum(jnp.where(_iota(gates.shape, 1) == e, gates, 0.0), axis=1, keepdims=True)
    hg = _dot(xb, wg_ref[0])
    hu = _dot(xb, wu_ref[0])
    hidden = hg * _sigmoid(hg) * hu
    hidden = jnp.where(gate > 0.0, hidden * gate, 0.0)
    acc_ref[...] += _dot(hidden.astype(BF16), wd_ref[0])

    @pl.when(e == pl.num_programs(1) - 1)
    def _():
        o_ref[...] = _layer_norm(DEEPNORM_ALPHA * x_ref[...] + acc_ref[...], g_ref[...], b_ref[...])


def _moe(xb, x, gates, w_gate, w_up, w_down, ln_g, ln_b, tm):
    T = x.shape[0]
    n_e = w_gate.shape[0]
    row_spec = lambda w: pl.BlockSpec((tm, w), lambda i, e: (i, 0))
    return pl.pallas_call(
        _moe_kernel,
        grid=(T // tm, n_e),
        in_specs=[row_spec(D_MODEL), row_spec(D_MODEL), row_spec(LANES),
                  pl.BlockSpec((1, D_MODEL, D_EXPERT), lambda i, e: (e, 0, 0)),
                  pl.BlockSpec((1, D_MODEL, D_EXPERT), lambda i, e: (e, 0, 0)),
                  pl.BlockSpec((1, D_EXPERT, D_MODEL), lambda i, e: (e, 0, 0)),
                  _full_spec((1, D_MODEL)), _full_spec((1, D_MODEL))],
        out_specs=row_spec(D_MODEL),
        out_shape=jax.ShapeDtypeStruct((T, D_MODEL), F32),
        scratch_shapes=[pltpu.VMEM((tm, D_MODEL), F32)],
        compiler_params=_cparams(("arbitrary", "arbitrary")),
        name="moe",
    )(xb, x, gates, w_gate, w_up, w_down, ln_g.reshape(1, -1), ln_b.reshape(1, -1))


def _tile(n, want):
    t = min(n, want)
    assert n % t == 0, (n, t)
    return t


def kernel(x, mem, mem_ln_g, mem_ln_b, w_mem_kv, fox_w_in, fox_b_f, rwkv_w_in, rwkv_mu, rwkv_w0, rwkv_w2,
           rwkv_a0, rwkv_a2, rwkv_g2, rwkv_k_k, rwkv_k_a, rwkv_r_k, rwkv_lnx_g, rwkv_lnx_b, w_out, ln1_g,
           ln1_b, w_router, router_bias, w_exp_gate, w_exp_up, w_exp_down, w_sh_gate, w_sh_up, w_sh_down,
           ln2_g, ln2_b):
    B, S, D = x.shape
    assert D == D_MODEL and S % CHUNK == 0
    T = B * S
    t_proj = _tile(S, 512)
    t_attn = _tile(S, 512)
    t_rwkv = _tile(S, 256)
    t_scan = _tile(S, 512)
    t_moe = _tile(T, 1024)

    km, vm = _mem_kv(mem, mem_ln_g, mem_ln_b, w_mem_kv)
    for i in range(DEPTH):
        j = i // 2
        if i % 2 == 0:
            q_aug, k_aug, v, g, qm = _fox_in(x, fox_w_in[j], fox_b_f[j], t_proj)
            tok = _fox_attn(q_aug, k_aug, v, g, t_attn)
        else:
            (rt, at, bt, kt, bh, kh, v, pc, bonus, g, qm) = _rwkv_in(
                x, rwkv_w_in[j], rwkv_mu[j], rwkv_w0[j], rwkv_w2[j], rwkv_a0[j], rwkv_a2[j], rwkv_g2[j],
                rwkv_k_k[j], rwkv_k_a[j], rwkv_r_k[j].reshape(-1), t_rwkv)
            y = _rwkv_scan(rt, at, bt, kt, bh, kh, v, pc, t_scan)
            tok = _rwkv_post(y, bonus, v, g, rwkv_lnx_g[j], rwkv_lnx_b[j], t_proj)
        x1, x1b, gates = _mix_out(tok, qm, km, vm, x, w_out[i], ln1_g[i], ln1_b[i], w_router[i],
                                  router_bias[i], t_proj)
        wg = jnp.concatenate([w_exp_gate[i], w_sh_gate[i][None]], axis=0).astype(BF16)
        wu = jnp.concatenate([w_exp_up[i], w_sh_up[i][None]], axis=0).astype(BF16)
        wd = jnp.concatenate([w_exp_down[i], w_sh_down[i][None]], axis=0).astype(BF16)
        x = _moe(x1b.reshape(T, D), x1.reshape(T, D), gates.reshape(T, LANES), wg, wu, wd,
                 ln2_g[i], ln2_b[i], t_moe).reshape(B, S, D)
    return x
```

```python
import functools
import math

import jax
import jax.numpy as jnp
from jax import lax
from jax.experimental import pallas as pl
from jax.experimental.pallas import tpu as pltpu

F32 = jnp.float32
BF16 = jnp.bfloat16

D_MODEL = 1024
HEAD_DIM = 64
N_TOK_HEADS = 12
TOK_W = N_TOK_HEADS * HEAD_DIM
N_HEAD_PAIRS = N_TOK_HEADS // 2
N_MEM_HEADS = 4
MEM_W = N_MEM_HEADS * HEAD_DIM
DECAY_LORA = 64
AAA_LORA = 64
GATE_LORA = 128
RWKV_SHIFT_W = 3 * TOK_W + DECAY_LORA + AAA_LORA + GATE_LORA
N_EXPERTS = 64
TOP_K = 6
D_EXPERT = 256
ROUTED_SCALE = 2.5
DEPTH = 2
DEEPNORM_ALPHA = (2 * DEPTH) ** 0.25
LN_EPS = 1e-5
GN_EPS = 64e-5
LOG2E = math.log2(math.e)

LANES = 128
AUG_W = 2 * HEAD_DIM
CHUNK = 64
VMEM_LIMIT = 56 * 1024 * 1024


def _cparams(sem):
    return pltpu.CompilerParams(dimension_semantics=sem, vmem_limit_bytes=VMEM_LIMIT)


def _dot(a, b):
    return jnp.dot(a, b, preferred_element_type=F32)


def _dot_nt(a, b):
    return lax.dot_general(a, b, (((1,), (1,)), ((), ())), preferred_element_type=F32)


def _split2(x):
    hi = x.astype(BF16)
    lo = (x - hi.astype(F32)).astype(BF16)
    return hi, lo


def _split3(x):
    hi = x.astype(BF16)
    r1 = x - hi.astype(F32)
    mid = r1.astype(BF16)
    lo = (r1 - mid.astype(F32)).astype(BF16)
    return hi, mid, lo


def _dot_sel(sel, x):
    hi, mid, lo = _split3(x)
    return _dot(sel, hi) + _dot(sel, mid) + _dot(sel, lo)


def _dot3(x, w_hi, w_lo):
    x_hi, x_lo = _split2(x)
    return _dot(x_hi, w_hi) + _dot(x_lo, w_hi) + _dot(x_hi, w_lo)


def _sigmoid(x):
    return 1.0 / (1.0 + jnp.exp(-x))


def _layer_norm(y, g, b):
    mu = jnp.mean(y, axis=-1, keepdims=True)
    yc = y - mu
    var = jnp.mean(yc * yc, axis=-1, keepdims=True)
    return yc * lax.rsqrt(var + LN_EPS) * g + b


def _iota(shape, dim):
    return lax.broadcasted_iota(jnp.int32, shape, dim)


def _div_pow2(x, n):
    return jnp.right_shift(x, int(math.log2(n)))


def _mod_pow2(x, n):
    return jnp.bitwise_and(x, n - 1)


def _full_spec(shape):
    n = len(shape)
    return pl.BlockSpec(shape, lambda *_: (0,) * n)


def _mem_kv_kernel(mem_ref, g_ref, b_ref, w_ref, k_ref, v_ref):
    m = _layer_norm(mem_ref[0], g_ref[...], b_ref[...])
    kv = _dot(m.astype(BF16), w_ref[...])
    k = kv[:, :MEM_W]
    v = kv[:, MEM_W:]
    head = _div_pow2(_iota(k.shape, 1), HEAD_DIM)
    for h in range(N_MEM_HEADS):
        k_ref[0, h] = jnp.where(head == h, k, 0.0).astype(BF16)
        v_ref[0, h] = jnp.where(head == h, v, 0.0).astype(BF16)


def _mem_kv(mem, g, b, w):
    B, n_mem, _ = mem.shape
    out = jax.ShapeDtypeStruct((B, N_MEM_HEADS, n_mem, MEM_W), BF16)
    out_spec = pl.BlockSpec((1, N_MEM_HEADS, n_mem, MEM_W), lambda i: (i, 0, 0, 0))
    return pl.pallas_call(
        _mem_kv_kernel,
        grid=(B,),
        in_specs=[pl.BlockSpec((1, n_mem, D_MODEL), lambda i: (i, 0, 0)),
                  _full_spec((1, D_MODEL)), _full_spec((1, D_MODEL)),
                  _full_spec((D_MODEL, 2 * MEM_W))],
        out_specs=[out_spec, out_spec],
        out_shape=[out, out],
        compiler_params=_cparams(("arbitrary",)),
        name="mem_kv",
    )(mem, g.reshape(1, -1), b.reshape(1, -1), w.astype(BF16))


def _fox_in_kernel(x_ref, wq_ref, wk_ref, wv_ref, wg_ref, wfh_ref, wfl_ref, wm_ref, bf_ref,
                   qc_ref, place_ref, q_ref, k_ref, v_ref, g_ref, qm_ref, carry_ref):
    @pl.when(pl.program_id(1) == 0)
    def _():
        carry_ref[...] = jnp.zeros_like(carry_ref)

    x = x_ref[0]
    xb = x.astype(BF16)
    tm = x.shape[0]
    z = _dot3(x, wfh_ref[...], wfl_ref[...]) + bf_ref[...]
    log_f = jnp.minimum(z, 0.0) - jnp.log(1.0 + jnp.exp(-jnp.abs(z)))
    tril = (_iota((tm, tm), 1) <= _iota((tm, tm), 0)).astype(BF16)
    c = _dot_sel(tril, log_f) + carry_ref[...]
    carry_ref[...] = c[tm - 1:tm, :]
    c_hi, c_mid, c_lo = _split3(c * LOG2E)
    c_parts = jnp.concatenate([c_hi, c_mid, c_lo], axis=1)
    q_ref[0] = (_dot(xb, wq_ref[...]) + qc_ref[...]).astype(BF16)
    k_ref[0] = (_dot(xb, wk_ref[...]) + _dot(c_parts, place_ref[...])).astype(BF16)
    v_ref[0] = _dot(xb, wv_ref[...]).astype(BF16)
    g_ref[0] = _sigmoid(_dot(xb, wg_ref[...])).astype(BF16)
    qm_ref[0] = _dot(xb, wm_ref[...]).astype(BF16)


def _fox_in(x, w_in, b_f, tm):
    B, S, _ = x.shape
    scale = HEAD_DIM ** -0.5
    wq, wk, wv, wg, wf, wm = jnp.split(
        w_in, [TOK_W, 2 * TOK_W, 3 * TOK_W, 4 * TOK_W, 4 * TOK_W + N_TOK_HEADS], axis=1)

    def per_head_pad(w):
        w = w.reshape(D_MODEL, N_TOK_HEADS, HEAD_DIM)
        return jnp.pad(w, ((0, 0), (0, 0), (0, AUG_W - HEAD_DIM))).reshape(D_MODEL, N_TOK_HEADS * AUG_W)

    wq_aug = per_head_pad(wq * (scale * LOG2E)).astype(BF16)
    wk_aug = per_head_pad(wk).astype(BF16)
    wf_pad = jnp.pad(wf, ((0, 0), (0, LANES - N_TOK_HEADS)))
    wf_hi = wf_pad.astype(BF16)
    wf_lo = (wf_pad - wf_hi.astype(F32)).astype(BF16)
    bf_pad = jnp.pad(b_f, (0, LANES - N_TOK_HEADS)).reshape(1, LANES)
    col = jnp.arange(N_TOK_HEADS * AUG_W)
    is_c = (col % AUG_W >= HEAD_DIM) & (col % AUG_W < HEAD_DIM + 3)
    q_const = jnp.where(is_c, -1.0, 0.0).astype(F32).reshape(1, -1)
    row = jnp.arange(3 * LANES)
    place = ((row[:, None] % LANES == col[None, :] // AUG_W)
             & (col[None, :] % AUG_W == HEAD_DIM + row[:, None] // LANES)).astype(BF16)

    aug_w = N_TOK_HEADS * AUG_W
    row_spec = lambda w: pl.BlockSpec((1, tm, w), lambda b, i: (b, i, 0))
    return pl.pallas_call(
        _fox_in_kernel,
        grid=(B, S // tm),
        in_specs=[row_spec(D_MODEL),
                  _full_spec((D_MODEL, aug_w)), _full_spec((D_MODEL, aug_w)),
                  _full_spec((D_MODEL, TOK_W)), _full_spec((D_MODEL, TOK_W)),
                  _full_spec((D_MODEL, LANES)), _full_spec((D_MODEL, LANES)),
                  _full_spec((D_MODEL, MEM_W)), _full_spec((1, LANES)),
                  _full_spec((1, aug_w)), _full_spec((3 * LANES, aug_w))],
        out_specs=[row_spec(aug_w), row_spec(aug_w), row_spec(TOK_W), row_spec(TOK_W), row_spec(MEM_W)],
        out_shape=[jax.ShapeDtypeStruct((B, S, aug_w), BF16), jax.ShapeDtypeStruct((B, S, aug_w), BF16),
                   jax.ShapeDtypeStruct((B, S, TOK_W), BF16), jax.ShapeDtypeStruct((B, S, TOK_W), BF16),
                   jax.ShapeDtypeStruct((B, S, MEM_W), BF16)],
        scratch_shapes=[pltpu.VMEM((1, LANES), F32)],
        compiler_params=_cparams(("arbitrary", "arbitrary")),
        name="fox_in",
    )(x, wq_aug, wk_aug, wv.astype(BF16), wg.astype(BF16), wf_hi, wf_lo,
      (wm * scale).astype(BF16), bf_pad, q_const, place)


def _fox_attn_kernel(q_ref, k_ref, v_ref, g_ref, o_ref, sa_ref, sb_ref, *, tq):
    qi = pl.program_id(2)
    qs = [q_ref[0, :, h * AUG_W:(h + 1) * AUG_W] for h in range(2)]

    def scores(kj, s_ref):
        start = pl.multiple_of(kj * tq, tq)
        for h in range(2):
            s_ref[h] = _dot_nt(qs[h], k_ref[0, pl.ds(start, tq), h * AUG_W:(h + 1) * AUG_W])

    def absorb(kj, s_ref, carry, masked):
        start = pl.multiple_of(kj * tq, tq)
        v = v_ref[0, pl.ds(start, tq), :]
        out = []
        for h in range(2):
            m, l, acc = carry[h]
            s = s_ref[h]
            if masked:
                s = jnp.where(_iota((tq, tq), 1) <= _iota((tq, tq), 0), s, -jnp.inf)
            m_new = jnp.maximum(m, jnp.max(s, axis=1, keepdims=True))
            p = jnp.exp2(s - m_new)
            alpha = jnp.exp2(m - m_new)
            p_lanes = functools.reduce(
                jnp.add, [p[:, i * LANES:(i + 1) * LANES] for i in range(tq // LANES)])
            l = alpha * l + p_lanes
            acc = alpha * acc + _dot(p.astype(BF16), v)
            out.append((m_new, l, acc))
        return tuple(out)

    def pair(i, carry):
        scores(2 * i + 1, sb_ref)
        carry = absorb(2 * i, sa_ref, carry, masked=False)
        scores(2 * i + 2, sa_ref)
        return absorb(2 * i + 1, sb_ref, carry, masked=False)

    def odd_tail(carry):
        scores(qi, sb_ref)
        carry = absorb(qi - 1, sa_ref, carry, masked=False)
        return absorb(qi, sb_ref, carry, masked=True)

    def even_tail(carry):
        return absorb(qi, sa_ref, carry, masked=True)

    init = (jnp.full((tq, 1), -jnp.inf, F32), jnp.zeros((tq, LANES), F32), jnp.zeros((tq, LANES), F32))
    scores(0, sa_ref)
    carry = lax.fori_loop(0, jnp.right_shift(qi, 1), pair, (init, init))
    (_, l0, acc0), (_, l1, acc1) = lax.cond(jnp.bitwise_and(qi, 1) == 1, odd_tail, even_tail, carry)
    o0 = acc0 / jnp.sum(l0, axis=1, keepdims=True)
    o1 = acc1 / jnp.sum(l1, axis=1, keepdims=True)
    o = jnp.where(_iota((tq, LANES), 1) < HEAD_DIM, o0, o1)
    o_ref[0] = (o * g_ref[0].astype(F32)).astype(BF16)


def _fox_attn(q_aug, k_aug, v, g, tq):
    B, S, _ = v.shape
    return pl.pallas_call(
        functools.partial(_fox_attn_kernel, tq=tq),
        grid=(B, N_HEAD_PAIRS, S // tq),
        in_specs=[pl.BlockSpec((1, tq, 2 * AUG_W), lambda b, p, i: (b, i, p)),
                  pl.BlockSpec((1, S, 2 * AUG_W), lambda b, p, i: (b, 0, p)),
                  pl.BlockSpec((1, S, LANES), lambda b, p, i: (b, 0, p)),
                  pl.BlockSpec((1, tq, LANES), lambda b, p, i: (b, i, p))],
        out_specs=pl.BlockSpec((1, tq, LANES), lambda b, p, i: (b, i, p)),
        out_shape=jax.ShapeDtypeStruct((B, S, TOK_W), BF16),
        scratch_shapes=[pltpu.VMEM((2, tq, tq), F32), pltpu.VMEM((2, tq, tq), F32)],
        compiler_params=_cparams(("arbitrary", "arbitrary", "arbitrary")),
        name="fox_attn",
    )(q_aug, k_aug, v, g)


def _rwkv_in_kernel(x_ref, w_ref, mu_ref, w0_ref, w2_ref, a0_ref, a2_ref, g2_ref, kk_ref, ka_ref,
                    rk_ref, head_ones_ref,
                    rt_ref, at_ref, bt_ref, kt_ref, bh_ref, kh_ref, v_ref, pc_ref, bonus_ref, g_ref,
                    qm_ref, prev_ref):
    @pl.when(pl.program_id(1) == 0)
    def _():
        prev_ref[...] = jnp.zeros_like(prev_ref)

    tm = x_ref.shape[1]
    h = _dot(x_ref[0].astype(BF16), w_ref[...])
    qm_ref[0] = (h[:, RWKV_SHIFT_W:] * (HEAD_DIM ** -0.5)).astype(BF16)
    hs = h[:, :RWKV_SHIFT_W]
    row = _iota(hs.shape, 0)
    shifted = jnp.where(row == 0, prev_ref[...], pltpu.roll(hs, 1, 0))
    prev_ref[...] = hs[tm - 1:tm, :]
    hs = hs + mu_ref[...] * (shifted - hs)
    r = hs[:, :TOK_W]
    k = hs[:, TOK_W:2 * TOK_W]
    v = hs[:, 2 * TOK_W:3 * TOK_W]
    wa = hs[:, 3 * TOK_W:3 * TOK_W + LANES]
    gd = hs[:, 3 * TOK_W + LANES:]
    w = w0_ref[...] + _dot(jnp.tanh(wa).astype(BF16), w2_ref[...])
    w = -(jnp.maximum(-w, 0.0) + jnp.log(1.0 + jnp.exp(-jnp.abs(w)))) - 0.5
    log_decay = -jnp.exp(w)
    a = _sigmoid(a0_ref[...] + _dot(wa.astype(BF16), a2_ref[...]))
    g_ref[0] = _dot(_sigmoid(gd).astype(BF16), g2_ref[...])
    kk = k * kk_ref[...]
    ss_hi, ss_lo = _split2(kk * kk)
    norm2 = _dot(ss_hi, head_ones_ref[...]) + _dot(ss_lo, head_ones_ref[...])
    kk = kk / jnp.maximum(jnp.sqrt(norm2), 1e-12)
    k = k * (1.0 + (a - 1.0) * ka_ref[...])
    bonus_ref[0] = r * k * rk_ref[...]
    v_ref[0] = v
    a_s = -kk
    b_s = kk * a
    same_chunk = _div_pow2(_iota((tm, tm), 0), CHUNK) == _div_pow2(_iota((tm, tm), 1), CHUNK)
    incl = (same_chunk & (_iota((tm, tm), 1) <= _iota((tm, tm), 0))).astype(BF16)
    d_hi, d_mid, d_lo = _split3(log_decay)
    cw = _dot(incl, d_hi) + _dot(incl, d_mid) + _dot(incl, d_lo)
    same = same_chunk.astype(BF16)
    cw_end = _dot(same, d_hi) + _dot(same, d_mid) + _dot(same, d_lo)
    e_in = jnp.exp(cw)
    e_out = jnp.exp(-cw)
    e_end = jnp.exp(cw_end - cw)
    rt_ref[0] = r * e_in
    at_ref[0] = a_s * jnp.exp(cw - log_decay)
    bt_ref[0] = b_s * e_out
    kt_ref[0] = k * e_out
    bh_ref[0] = b_s * e_end
    kh_ref[0] = k * e_end
    pc_ref[0] = jnp.exp(cw_end)


def _rwkv_in(x, w_in, mu, w0, w2, a0, a2, g2, k_k, k_a, r_k, tm):
    B, S, _ = x.shape
    in_w = w_in.shape[1]
    w2_pad = jnp.concatenate([w2, jnp.zeros((AAA_LORA, TOK_W), F32)], axis=0).astype(BF16)
    a2_pad = jnp.concatenate([jnp.zeros((DECAY_LORA, TOK_W), F32), a2], axis=0).astype(BF16)
    head = jnp.arange(TOK_W) // HEAD_DIM
    head_ones = (head[:, None] == head[None, :]).astype(BF16)
    vec = lambda t: t.reshape(1, -1)
    row_spec = lambda w: pl.BlockSpec((1, tm, w), lambda b, i: (b, i, 0))
    tok = jax.ShapeDtypeStruct((B, S, TOK_W), F32)
    return pl.pallas_call(
        _rwkv_in_kernel,
        grid=(B, S // tm),
        in_specs=[row_spec(D_MODEL), _full_spec((D_MODEL, in_w)), _full_spec((1, RWKV_SHIFT_W)),
                  _full_spec((1, TOK_W)), _full_spec((LANES, TOK_W)), _full_spec((1, TOK_W)),
                  _full_spec((LANES, TOK_W)), _full_spec((GATE_LORA, TOK_W)), _full_spec((1, TOK_W)),
                  _full_spec((1, TOK_W)), _full_spec((1, TOK_W)), _full_spec((TOK_W, TOK_W))],
        out_specs=[row_spec(TOK_W)] * 10 + [row_spec(MEM_W)],
        out_shape=[tok] * 10 + [jax.ShapeDtypeStruct((B, S, MEM_W), BF16)],
        scratch_shapes=[pltpu.VMEM((1, RWKV_SHIFT_W), F32)],
        compiler_params=_cparams(("arbitrary", "arbitrary")),
        name="rwkv_in",
    )(x, w_in.astype(BF16), vec(mu), vec(w0), w2_pad, vec(a0), a2_pad, g2.astype(BF16), vec(k_k),
      vec(k_a), vec(r_k), head_ones)


def _rwkv_scan_kernel(rt_ref, at_ref, bt_ref, kt_ref, bh_ref, kh_ref, v_ref, pc_ref, y_ref, state_ref,
                      *, n_chunks):
    @pl.when(pl.program_id(2) == 0)
    def _():
        state_ref[...] = jnp.zeros_like(state_ref)

    C = CHUNK
    lane_head = _div_pow2(_iota((2 * C, LANES), 1), HEAD_DIM)
    row_head = _div_pow2(_iota((2 * C, LANES), 0), C)
    keep = lane_head == row_head
    pos_r = _mod_pow2(_iota((2 * C, 2 * C), 0), C)
    pos_c = _mod_pow2(_iota((2 * C, 2 * C), 1), C)
    strict = pos_c < pos_r
    incl = pos_c <= pos_r
    eye = (_iota((2 * C, 2 * C), 0) == _iota((2 * C, 2 * C), 1)).astype(F32)

    def stack(ref, c):
        t = ref[0, c * C:(c + 1) * C, :]
        return jnp.where(keep, jnp.concatenate([t, t], axis=0), 0.0).astype(BF16)

    chunks = range(n_chunks)
    a2 = [stack(at_ref, c) for c in chunks]
    r2 = [stack(rt_ref, c) for c in chunks]
    bh2 = [stack(bh_ref, c) for c in chunks]
    kh2 = [stack(kh_ref, c) for c in chunks]
    v2 = [stack(v_ref, c) for c in chunks]
    gram = [_dot_nt(jnp.concatenate([a2[c], r2[c]], axis=0),
                    jnp.concatenate([stack(bt_ref, c), stack(kt_ref, c)], axis=0)) for c in chunks]
    power = [jnp.where(strict, gram[c][:2 * C, :2 * C], 0.0) for c in chunks]
    inv = [eye + power[c] for c in chunks]
    for _ in range(int(math.log2(C)) - 1):
        power = [_dot(p.astype(BF16), p.astype(BF16)) for p in power]
        inv = [inv[c] + _dot(inv[c].astype(BF16), power[c].astype(BF16)) for c in chunks]
    from_v = [_dot(jnp.concatenate([jnp.where(strict, gram[c][:2 * C, 2 * C:], 0.0),
                                    jnp.where(incl, gram[c][2 * C:, 2 * C:], 0.0)], axis=0).astype(BF16),
                   v2[c]) for c in chunks]
    wu = [_dot(inv[c].astype(BF16),
               jnp.concatenate([a2[c], from_v[c][:2 * C].astype(BF16)], axis=1)) for c in chunks]
    wy = [_dot(jnp.where(incl, gram[c][2 * C:, :2 * C], 0.0).astype(BF16), wu[c].astype(BF16))
          for c in chunks]
    w_y = [(r2[c].astype(F32) + wy[c][:, :LANES]).astype(BF16) for c in chunks]
    y0 = [from_v[c][2 * C:] + wy[c][:, LANES:] for c in chunks]
    gc = [_dot(wu[c].T.astype(BF16), bh2[c]) for c in chunks]
    c0 = [gc[c][LANES:] + _dot(v2[c].astype(F32).T.astype(BF16), kh2[c]) for c in chunks]
    state = state_ref[...]
    for c in chunks:
        sb = state.astype(BF16)
        y2 = _dot_nt(w_y[c], sb) + y0[c]
        y_ref[0, c * C:(c + 1) * C, :] = y2[:C] + y2[C:]
        state = state * pc_ref[0, c * C:c * C + 1, :] + _dot(sb, gc[c][:LANES].astype(BF16)) + c0[c]
    state_ref[...] = state


def _rwkv_scan(rt, at, bt, kt, bh, kh, v, pc, rows):
    B, S, _ = v.shape
    spec = pl.BlockSpec((1, rows, LANES), lambda b, p, i: (b, i, p))
    return pl.pallas_call(
        functools.partial(_rwkv_scan_kernel, n_chunks=rows // CHUNK),
        grid=(B, N_HEAD_PAIRS, S // rows),
        in_specs=[spec] * 8,
        out_specs=spec,
        out_shape=jax.ShapeDtypeStruct((B, S, TOK_W), F32),
        scratch_shapes=[pltpu.VMEM((LANES, LANES), F32)],
        compiler_params=_cparams(("arbitrary", "arbitrary", "arbitrary")),
        name="rwkv_scan",
    )(rt, at, bt, kt, bh, kh, v, pc)


def _rwkv_post_kernel(y_ref, bonus_ref, v_ref, g_ref, lg_ref, lb_ref, head_mean_ref, o_ref):
    y = y_ref[0]
    hm = head_mean_ref[...]

    def head_mean(t):
        hi, lo = _split2(t)
        return _dot(hi, hm) + _dot(lo, hm)

    yc = y - head_mean(y)
    var = head_mean(yc * yc)
    yn = yc * lax.rsqrt(var + GN_EPS) * lg_ref[...] + lb_ref[...]
    bonus = head_mean(bonus_ref[0]) * float(HEAD_DIM)
    o_ref[0] = ((yn + bonus * v_ref[0]) * g_ref[0]).astype(BF16)


def _rwkv_post(y, bonus, v, g, lnx_g, lnx_b, tm):
    B, S, _ = y.shape
    head = jnp.arange(TOK_W) // HEAD_DIM
    head_mean = ((head[:, None] == head[None, :]).astype(F32) / HEAD_DIM).astype(BF16)
    row_spec = pl.BlockSpec((1, tm, TOK_W), lambda b, i: (b, i, 0))
    return pl.pallas_call(
        _rwkv_post_kernel,
        grid=(B, S // tm),
        in_specs=[row_spec] * 4 + [_full_spec((1, TOK_W)), _full_spec((1, TOK_W)),
                                   _full_spec((TOK_W, TOK_W))],
        out_specs=row_spec,
        out_shape=jax.ShapeDtypeStruct((B, S, TOK_W), BF16),
        compiler_params=_cparams(("arbitrary", "arbitrary")),
        name="rwkv_post",
    )(y, bonus, v, g, lnx_g.reshape(1, -1), lnx_b.reshape(1, -1), head_mean)


def _mix_out_kernel(tok_ref, qm_ref, km_ref, vm_ref, x_ref, wo_tok_ref, wo_mem_ref, g_ref, b_ref,
                    wr_hi_ref, wr_lo_ref, rb_ref, x1_ref, x1b_ref, gate_ref):
    qm = qm_ref[0]
    mem_out = None
    for h in range(N_MEM_HEADS):
        s = _dot_nt(qm, km_ref[0, h])
        e = jnp.exp(s - jnp.max(s, axis=1, keepdims=True))
        o = _dot(e.astype(BF16), vm_ref[0, h]) / jnp.sum(e, axis=1, keepdims=True)
        mem_out = o if mem_out is None else mem_out + o
    mixed = _dot(tok_ref[0], wo_tok_ref[...]) + _dot(mem_out.astype(BF16), wo_mem_ref[...])
    x1 = _layer_norm(DEEPNORM_ALPHA * x_ref[0] + mixed, g_ref[...], b_ref[...])
    x1_ref[0] = x1
    x1b_ref[0] = x1.astype(BF16)
    scores = _sigmoid(_dot3(x1, wr_hi_ref[...], wr_lo_ref[...]))
    lane = _iota(scores.shape, 1)
    lane_f = lane.astype(F32)
    cand = jnp.where(lane < N_EXPERTS, scores + rb_ref[...], -jnp.inf)
    picked = jnp.zeros(scores.shape, jnp.bool_)
    for _ in range(TOP_K):
        best = jnp.max(cand, axis=1, keepdims=True)
        first = jnp.min(jnp.where(cand == best, lane_f, float(LANES)), axis=1, keepdims=True)
        hit = lane_f == first
        picked = jnp.logical_or(picked, hit)
        cand = jnp.where(hit, -jnp.inf, cand)
    chosen = jnp.where(picked, scores, 0.0)
    gate = chosen / jnp.sum(chosen, axis=1, keepdims=True) * ROUTED_SCALE
    gate_ref[0] = jnp.where(lane == N_EXPERTS, 1.0, gate)


def _mix_out(tok, qm, km, vm, x, w_out, ln_g, ln_b, w_router, router_bias, tm):
    B, S, _ = x.shape
    n_mem = km.shape[2]
    wr = jnp.pad(w_router, ((0, 0), (0, LANES - N_EXPERTS)))
    wr_hi = wr.astype(BF16)
    wr_lo = (wr - wr_hi.astype(F32)).astype(BF16)
    rb = jnp.pad(router_bias, (0, LANES - N_EXPERTS)).reshape(1, LANES)
    row_spec = lambda w: pl.BlockSpec((1, tm, w), lambda b, i: (b, i, 0))
    mem_spec = pl.BlockSpec((1, N_MEM_HEADS, n_mem, MEM_W), lambda b, i: (b, 0, 0, 0))
    return pl.pallas_call(
        _mix_out_kernel,
        grid=(B, S // tm),
        in_specs=[row_spec(TOK_W), row_spec(MEM_W), mem_spec, mem_spec, row_spec(D_MODEL),
                  _full_spec((TOK_W, D_MODEL)), _full_spec((MEM_W, D_MODEL)),
                  _full_spec((1, D_MODEL)), _full_spec((1, D_MODEL)),
                  _full_spec((D_MODEL, LANES)), _full_spec((D_MODEL, LANES)), _full_spec((1, LANES))],
        out_specs=[row_spec(D_MODEL), row_spec(D_MODEL), row_spec(LANES)],
        out_shape=[jax.ShapeDtypeStruct((B, S, D_MODEL), F32), jax.ShapeDtypeStruct((B, S, D_MODEL), BF16),
                   jax.ShapeDtypeStruct((B, S, LANES), F32)],
        compiler_params=_cparams(("arbitrary", "arbitrary")),
        name="mix_out",
    )(tok, qm, km, vm, x, w_out[:TOK_W].astype(BF16), w_out[TOK_W:].astype(BF16),
      ln_g.reshape(1, -1), ln_b.reshape(1, -1), wr_hi, wr_lo, rb)


MOE_TM = 512
UNIT = 16
SLAB_ROWS = 4096
SLAB_UNITS = SLAB_ROWS // UNIT
BLK_UNITS = 32
BLK_ROWS = BLK_UNITS * UNIT
XS_W = D_MODEL + 2 * LANES
ROW_CHUNK = 1024
assert TOP_K * MOE_TM + N_EXPERTS * (UNIT - 1) <= SLAB_ROWS - UNIT
ZERO_UNIT = SLAB_UNITS - 1


def _dot_x_sel(x, sel):
    hi, mid, lo = _split3(x)
    return _dot(hi, sel) + _dot(mid, sel) + _dot(lo, sel)


def _route_tile(gates):
    tm = gates.shape[0]
    sel = jnp.logical_and(gates > 0.0, _iota(gates.shape, 1) < N_EXPERTS)
    sel_b = jnp.where(sel, 1.0, 0.0).astype(BF16)
    earlier = (_iota((tm, tm), 1) < _iota((tm, tm), 0)).astype(BF16)
    rank1 = jnp.where(sel, _dot(earlier, sel_b) + 1.0, 0.0)
    count = _dot(jnp.ones((8, tm), BF16), sel_b)[0:1]
    n16 = jnp.floor((count + float(UNIT - 1)) * (1.0 / UNIT))
    before = (_iota((LANES, LANES), 0) < _iota((LANES, LANES), 1)).astype(BF16)
    off16 = _dot(jnp.broadcast_to(n16, (8, LANES)).astype(BF16), before)[0:1]
    return sel, rank1, n16, off16


def _slab_experts(n16, off16, r0, rows):
    r = (_iota((rows, LANES), 0) + r0).astype(F32)
    lo = off16 * float(UNIT)
    return jnp.logical_and(r >= lo, r < lo + n16 * float(UNIT))


def _moe_dispatch_kernel(xb_ref, gate_ref, xs_ref, n_ref):
    gates = gate_ref[...]
    _, rank1, n16, off16 = _route_tile(gates)
    n_ref[0] = jnp.broadcast_to(n16, (8, LANES))
    r_hi, r_lo = _split2(rank1)
    g_hi, g_lo = _split2(gates)
    src = jnp.concatenate([xb_ref[...], g_hi, g_lo], axis=1)
    for c in range(SLAB_ROWS // ROW_CHUNK):
        r0 = c * ROW_CHUNK
        seg = _slab_experts(n16, off16, r0, ROW_CHUNK)
        seg_b = jnp.where(seg, 1.0, 0.0).astype(BF16)
        rank_at = _dot_nt(seg_b, r_hi) + _dot_nt(seg_b, r_lo)
        r = (_iota((ROW_CHUNK, 1), 0) + r0).astype(F32)
        pos1 = r + 1.0 - jnp.sum(jnp.where(seg, off16 * float(UNIT), 0.0), axis=1, keepdims=True)
        pick = jnp.where(rank_at == pos1, 1.0, 0.0).astype(BF16)
        xs_ref[r0:r0 + ROW_CHUNK, :] = _dot(pick, src).astype(BF16)


def _moe_dispatch(xb, gates):
    T = xb.shape[0]
    nt = T // MOE_TM
    return pl.pallas_call(
        _moe_dispatch_kernel,
        grid=(nt,),
        in_specs=[pl.BlockSpec((MOE_TM, D_MODEL), lambda i: (i, 0)),
                  pl.BlockSpec((MOE_TM, LANES), lambda i: (i, 0))],
        out_specs=[pl.BlockSpec((SLAB_ROWS, XS_W), lambda i: (i, 0)),
                   pl.BlockSpec((1, 8, LANES), lambda i: (i, 0, 0))],
        out_shape=[jax.ShapeDtypeStruct((nt * SLAB_ROWS, XS_W), BF16),
                   jax.ShapeDtypeStruct((nt, 8, LANES), F32)],
        compiler_params=_cparams(("arbitrary",)),
        name="moe_dispatch",
    )(xb, gates)


def _moe_plan_kernel(n_ref, unit_ref, expert_ref, count_ref, *, nt, nb):
    n16 = n_ref[...]
    lane = _iota((nt, LANES), 1)
    before = (_iota((LANES, LANES), 0) < _iota((LANES, LANES), 1)).astype(BF16)
    off16 = _dot(n16.astype(BF16), before)
    used = jnp.sum(n16, axis=1, keepdims=True)
    n16 = n16 + jnp.where(lane == N_EXPERTS - 1, float(SLAB_UNITS) - used, 0.0)
    earlier = (_iota((nt, nt), 1) < _iota((nt, nt), 0)).astype(BF16)
    cum_ex = _dot_sel(earlier, n16)
    cum_in = cum_ex + n16
    total = cum_in[nt - 1:nt, :]
    n_blk = jnp.floor((total + float(BLK_UNITS - 1)) * (1.0 / BLK_UNITS))
    blk_start = _dot_x_sel(jnp.broadcast_to(n_blk, (8, LANES)), before)[0:1]
    blk_end = blk_start + n_blk
    count_ref[...] = jnp.broadcast_to(jnp.sum(n_blk, axis=1, keepdims=True), (8, LANES)).astype(jnp.int32)

    lane_b = _iota((nb, LANES), 1)
    blk = _iota((nb, LANES), 0).astype(F32)
    expert = jnp.sum(jnp.where(jnp.logical_and(lane_b < N_EXPERTS, blk_end <= blk), 1.0, 0.0),
                     axis=1, keepdims=True)
    live = expert < float(N_EXPERTS)
    expert = jnp.minimum(expert, float(N_EXPERTS - 1))
    expert_ref[...] = jnp.broadcast_to(expert, (nb, LANES)).astype(jnp.int32)
    mine = lane_b.astype(F32) == expert
    mine_b = jnp.where(mine, 1.0, 0.0).astype(BF16)
    my_start = jnp.sum(jnp.where(mine, blk_start, 0.0), axis=1, keepdims=True)
    my_total = jnp.sum(jnp.where(mine, total, 0.0), axis=1, keepdims=True)
    q = (blk - my_start) * float(BLK_UNITS) + lane_b.astype(F32)
    valid = jnp.logical_and(jnp.logical_and(q < my_total, lane_b < BLK_UNITS), live)

    def per_slab(table):
        hi, mid, lo = _split3(table)
        return _dot_nt(mine_b, hi) + _dot_nt(mine_b, mid) + _dot_nt(mine_b, lo)

    ex, inc, first = per_slab(cum_ex), per_slab(cum_in), per_slab(off16)
    base = jnp.zeros((nb, LANES), F32)
    for i in range(nt):
        hit = jnp.logical_and(q >= ex[:, i:i + 1], q < inc[:, i:i + 1])
        base = base + jnp.where(hit, float(i * SLAB_UNITS) + first[:, i:i + 1] - ex[:, i:i + 1], 0.0)
    unit_ref[...] = jnp.where(valid, base + q, -1.0).astype(jnp.int32)


def _moe_plan(n16, nb):
    nt = n16.shape[0]
    tbl = jax.ShapeDtypeStruct((nb, LANES), jnp.int32)
    units, experts, count = pl.pallas_call(
        functools.partial(_moe_plan_kernel, nt=nt, nb=nb),
        out_shape=[tbl, tbl, jax.ShapeDtypeStruct((8, LANES), jnp.int32)],
        compiler_params=pltpu.CompilerParams(vmem_limit_bytes=VMEM_LIMIT),
        name="moe_plan",
    )(n16)
    return units[:, :BLK_UNITS].reshape(-1), experts[:, 0], count[0, :1]


def _moe_ffn_kernel(unit_ref, expert_ref, count_ref, xs_hbm, wg_ref, wu_ref, wd_ref, ys_hbm,
                    ibuf, obuf, in_sem, out_sem, *, nb):
    b = pl.program_id(0)
    n_live = count_ref[0]
    slot = jnp.bitwise_and(b, 1)

    def in_copy(blk, buf, s):
        u = unit_ref[blk * BLK_UNITS + s]
        u = jnp.where(u < 0, ZERO_UNIT, u)
        return pltpu.make_async_copy(xs_hbm.at[pl.ds(pl.multiple_of(u * UNIT, UNIT), UNIT), :],
                                     ibuf.at[buf, pl.ds(s * UNIT, UNIT), :], in_sem.at[buf])

    def for_each_out(blk, buf, act):
        for s in range(BLK_UNITS):
            u = unit_ref[blk * BLK_UNITS + s]

            @pl.when(u >= 0)
            def _():
                act(pltpu.make_async_copy(
                    obuf.at[buf, pl.ds(s * UNIT, UNIT), :],
                    ys_hbm.at[pl.ds(pl.multiple_of(u * UNIT, UNIT), UNIT), :], out_sem.at[buf]))

    def start_in(blk, buf):
        for s in range(BLK_UNITS):
            in_copy(blk, buf, s).start()

    @pl.when(b == 0)
    def _():
        start_in(0, 0)

    @pl.when(b <= n_live)
    def _():
        for s in range(BLK_UNITS):
            in_copy(b, slot, s).wait()

    @pl.when(jnp.logical_and(b >= 2, b - 2 < n_live))
    def _():
        for_each_out(b - 2, slot, lambda c: c.wait())

    @pl.when(b < n_live)
    def _():
        @pl.when(b + 1 < nb)
        def _():
            start_in(b + 1, 1 - slot)

        rows = ibuf[slot]
        x = rows[:, :D_MODEL]
        gates = rows[:, D_MODEL:D_MODEL + LANES].astype(F32) + rows[:, D_MODEL + LANES:].astype(F32)
        e = expert_ref[b]
        gate = jnp.sum(jnp.where(_iota(gates.shape, 1) == e, gates, 0.0), axis=1, keepdims=True)
        hg = _dot(x, wg_ref[0, 0].astype(BF16))
        hu = _dot(x, wu_ref[0, 0].astype(BF16))
        hidden = hg * _sigmoid(hg) * hu * gate
        obuf[slot] = _dot(hidden.astype(BF16), wd_ref[0, 0].astype(BF16)).astype(BF16)
        for_each_out(b, slot, lambda c: c.start())

    @pl.when(b == nb - 1)
    def _():
        @pl.when(nb - 2 < n_live)
        def _():
            for_each_out(nb - 2, 1 - slot, lambda c: c.wait())

        @pl.when(nb - 1 < n_live)
        def _():
            for_each_out(nb - 1, slot, lambda c: c.wait())


def _moe_ffn(xs, units, experts, count, w_gate, w_up, w_down, layer, nb):
    n_rows = xs.shape[0]
    w_spec = lambda r, c: pl.BlockSpec((1, 1, r, c), lambda b, u, e, n: (layer, e[b], 0, 0))
    grid_spec = pltpu.PrefetchScalarGridSpec(
        num_scalar_prefetch=3,
        grid=(nb,),
        in_specs=[pl.BlockSpec(memory_space=pl.ANY),
                  w_spec(D_MODEL, D_EXPERT), w_spec(D_MODEL, D_EXPERT), w_spec(D_EXPERT, D_MODEL)],
        out_specs=pl.BlockSpec(memory_space=pl.ANY),
        scratch_shapes=[pltpu.VMEM((2, BLK_ROWS, XS_W), BF16), pltpu.VMEM((2, BLK_ROWS, D_MODEL), BF16),
                        pltpu.SemaphoreType.DMA((2,)), pltpu.SemaphoreType.DMA((2,))],
    )
    return pl.pallas_call(
        functools.partial(_moe_ffn_kernel, nb=nb),
        grid_spec=grid_spec,
        out_shape=jax.ShapeDtypeStruct((n_rows, D_MODEL), BF16),
        compiler_params=_cparams(("arbitrary",)),
        name="moe_ffn",
    )(units, experts, count, xs, w_gate, w_up, w_down)


def _moe_combine_kernel(ys_ref, gate_ref, x_ref, xb_ref, wsg_ref, wsu_ref, wsd_ref, g_ref, b_ref, o_ref):
    _, rank1, n16, off16 = _route_tile(gate_ref[...])
    r_hi, r_lo = _split2(rank1)
    off_b = jnp.broadcast_to(off16, (8, LANES)).astype(BF16)
    xb = xb_ref[...]
    hg = _dot(xb, wsg_ref[...])
    acc = _dot((hg * _sigmoid(hg) * _dot(xb, wsu_ref[...])).astype(BF16), wsd_ref[...])
    for c in range(SLAB_ROWS // ROW_CHUNK):
        r0 = c * ROW_CHUNK
        seg_b = jnp.where(_slab_experts(n16, off16, r0, ROW_CHUNK), 1.0, 0.0).astype(BF16)
        rank_at = _dot_nt(r_hi, seg_b) + _dot_nt(r_lo, seg_b)
        r = (_iota((1, ROW_CHUNK), 1) + r0).astype(F32)
        pos1 = r + 1.0 - _dot_nt(off_b, seg_b)[0:1] * float(UNIT)
        pick = jnp.where(rank_at == pos1, 1.0, 0.0).astype(BF16)
        acc = acc + _dot(pick, ys_ref[r0:r0 + ROW_CHUNK, :])
    o_ref[...] = _layer_norm(DEEPNORM_ALPHA * x_ref[...] + acc, g_ref[...], b_ref[...])


def _moe_combine(ys, gates, x, xb, w_sh_gate, w_sh_up, w_sh_down, ln_g, ln_b):
    T = x.shape[0]
    row_spec = lambda w: pl.BlockSpec((MOE_TM, w), lambda i: (i, 0))
    return pl.pallas_call(
        _moe_combine_kernel,
        grid=(T // MOE_TM,),
        in_specs=[pl.BlockSpec((SLAB_ROWS, D_MODEL), lambda i: (i, 0)),
                  row_spec(LANES), row_spec(D_MODEL), row_spec(D_MODEL),
                  _full_spec((D_MODEL, D_EXPERT)), _full_spec((D_MODEL, D_EXPERT)),
                  _full_spec((D_EXPERT, D_MODEL)), _full_spec((1, D_MODEL)), _full_spec((1, D_MODEL))],
        out_specs=row_spec(D_MODEL),
        out_shape=jax.ShapeDtypeStruct((T, D_MODEL), F32),
        compiler_params=_cparams(("arbitrary",)),
        name="moe_combine",
    )(ys, gates, x, xb, w_sh_gate.astype(BF16), w_sh_up.astype(BF16), w_sh_down.astype(BF16),
      ln_g.reshape(1, -1), ln_b.reshape(1, -1))


def _moe(xb, x, gates, w_gate, w_up, w_down, layer, w_sh_gate, w_sh_up, w_sh_down, ln_g, ln_b):
    T = x.shape[0]
    nt = T // MOE_TM
    nb = nt * SLAB_UNITS // BLK_UNITS + N_EXPERTS
    xs, n16 = _moe_dispatch(xb, gates)
    units, experts, count = _moe_plan(n16[:, 0, :], nb)
    ys = _moe_ffn(xs, units, experts, count, w_gate, w_up, w_down, layer, nb)
    return _moe_combine(ys, gates, x, xb, w_sh_gate, w_sh_up, w_sh_down, ln_g, ln_b)


def _tile(n, want):
    t = min(n, want)
    assert n % t == 0, (n, t)
    return t


def kernel(x, mem, mem_ln_g, mem_ln_b, w_mem_kv, fox_w_in, fox_b_f, rwkv_w_in, rwkv_mu, rwkv_w0, rwkv_w2,
           rwkv_a0, rwkv_a2, rwkv_g2, rwkv_k_k, rwkv_k_a, rwkv_r_k, rwkv_lnx_g, rwkv_lnx_b, w_out, ln1_g,
           ln1_b, w_router, router_bias, w_exp_gate, w_exp_up, w_exp_down, w_sh_gate, w_sh_up, w_sh_down,
           ln2_g, ln2_b):
    B, S, D = x.shape
    T = B * S
    assert D == D_MODEL and S % CHUNK == 0 and T % MOE_TM == 0
    t_proj = _tile(S, 512)
    t_attn = _tile(S, 512)
    t_rwkv = _tile(S, 256)
    t_scan = _tile(S, 512)

    km, vm = _mem_kv(mem, mem_ln_g, mem_ln_b, w_mem_kv)
    for i in range(DEPTH):
        j = i // 2
        if i % 2 == 0:
            q_aug, k_aug, v, g, qm = _fox_in(x, fox_w_in[j], fox_b_f[j], t_proj)
            tok = _fox_attn(q_aug, k_aug, v, g, t_attn)
        else:
            (rt, at, bt, kt, bh, kh, v, pc, bonus, g, qm) = _rwkv_in(
                x, rwkv_w_in[j], rwkv_mu[j], rwkv_w0[j], rwkv_w2[j], rwkv_a0[j], rwkv_a2[j], rwkv_g2[j],
                rwkv_k_k[j], rwkv_k_a[j], rwkv_r_k[j].reshape(-1), t_rwkv)
            y = _rwkv_scan(rt, at, bt, kt, bh, kh, v, pc, t_scan)
            tok = _rwkv_post(y, bonus, v, g, rwkv_lnx_g[j], rwkv_lnx_b[j], t_proj)
        x1, x1b, gates = _mix_out(tok, qm, km, vm, x, w_out[i], ln1_g[i], ln1_b[i], w_router[i],
                                  router_bias[i], t_proj)
        x = _moe(x1b.reshape(T, D), x1.reshape(T, D), gates.reshape(T, LANES), w_exp_gate, w_exp_up,
                 w_exp_down, i, w_sh_gate[i], w_sh_up[i], w_sh_down[i], ln2_g[i], ln2_b[i]).reshape(B, S, D)
    return x
```

```python
import functools
import math

import jax
import jax.numpy as jnp
from jax import lax
from jax.experimental import pallas as pl
from jax.experimental.pallas import tpu as pltpu

F32 = jnp.float32
BF16 = jnp.bfloat16

D_MODEL = 1024
HEAD_DIM = 64
N_TOK_HEADS = 12
TOK_W = N_TOK_HEADS * HEAD_DIM
N_HEAD_PAIRS = N_TOK_HEADS // 2
N_MEM_HEADS = 4
MEM_W = N_MEM_HEADS * HEAD_DIM
DECAY_LORA = 64
AAA_LORA = 64
GATE_LORA = 128
RWKV_SHIFT_W = 3 * TOK_W + DECAY_LORA + AAA_LORA + GATE_LORA
N_EXPERTS = 64
TOP_K = 6
D_EXPERT = 256
ROUTED_SCALE = 2.5
DEPTH = 2
DEEPNORM_ALPHA = (2 * DEPTH) ** 0.25
LN_EPS = 1e-5
GN_EPS = 64e-5
LOG2E = math.log2(math.e)

LANES = 128
AUG_W = 2 * HEAD_DIM
CHUNK = 64
MIX_GROUPS = 2
VMEM_LIMIT = 56 * 1024 * 1024


def _cparams(sem):
    return pltpu.CompilerParams(dimension_semantics=sem, vmem_limit_bytes=VMEM_LIMIT)


def _dot(a, b):
    return jnp.dot(a, b, preferred_element_type=F32)


def _dot_nt(a, b):
    return lax.dot_general(a, b, (((1,), (1,)), ((), ())), preferred_element_type=F32)


def _split2(x):
    hi = x.astype(BF16)
    lo = (x - hi.astype(F32)).astype(BF16)
    return hi, lo


def _split3(x):
    hi = x.astype(BF16)
    r1 = x - hi.astype(F32)
    mid = r1.astype(BF16)
    lo = (r1 - mid.astype(F32)).astype(BF16)
    return hi, mid, lo


def _dot_sel(sel, x):
    hi, mid, lo = _split3(x)
    return _dot(sel, hi) + _dot(sel, mid) + _dot(sel, lo)


def _dot3(x, w_hi, w_lo):
    x_hi, x_lo = _split2(x)
    return _dot(x_hi, w_hi) + _dot(x_lo, w_hi) + _dot(x_hi, w_lo)


MXU_W = 256


def _head_blocks(value):
    head = jnp.arange(MXU_W) // HEAD_DIM
    return jnp.where(head[:, None] == head[None, :], value, 0.0).astype(BF16)


def _head_sums(t, blocks):
    cols = []
    for c in range(0, t.shape[1], MXU_W):
        hi, lo = _split2(t[:, c:c + MXU_W])
        cols.append(_dot(hi, blocks) + _dot(lo, blocks))
    return jnp.concatenate(cols, axis=1)


def _sigmoid(x):
    return 1.0 / (1.0 + jnp.exp(-x))


def _layer_norm(y, g, b):
    mu = jnp.mean(y, axis=-1, keepdims=True)
    yc = y - mu
    var = jnp.mean(yc * yc, axis=-1, keepdims=True)
    return yc * lax.rsqrt(var + LN_EPS) * g + b


def _iota(shape, dim):
    return lax.broadcasted_iota(jnp.int32, shape, dim)


def _div_pow2(x, n):
    return jnp.right_shift(x, int(math.log2(n)))


def _mod_pow2(x, n):
    return jnp.bitwise_and(x, n - 1)


def _full_spec(shape):
    n = len(shape)
    return pl.BlockSpec(shape, lambda *_: (0,) * n)


def _mem_kv_kernel(mem_ref, g_ref, b_ref, w_ref, k_ref, v_ref):
    m = _layer_norm(mem_ref[0], g_ref[...], b_ref[...])
    kv = _dot(m.astype(BF16), w_ref[...])
    k = kv[:, :MEM_W]
    v = kv[:, MEM_W:]
    head = _div_pow2(_iota(k.shape, 1), HEAD_DIM)
    for h in range(N_MEM_HEADS):
        k_ref[0, h] = jnp.where(head == h, k, 0.0).astype(BF16)
        v_ref[0, h] = jnp.where(head == h, v, 0.0).astype(BF16)


def _mem_kv(mem, g, b, w):
    B, n_mem, _ = mem.shape
    out = jax.ShapeDtypeStruct((B, N_MEM_HEADS, n_mem, MEM_W), BF16)
    out_spec = pl.BlockSpec((1, N_MEM_HEADS, n_mem, MEM_W), lambda i: (i, 0, 0, 0))
    return pl.pallas_call(
        _mem_kv_kernel,
        grid=(B,),
        in_specs=[pl.BlockSpec((1, n_mem, D_MODEL), lambda i: (i, 0, 0)),
                  _full_spec((1, D_MODEL)), _full_spec((1, D_MODEL)),
                  _full_spec((D_MODEL, 2 * MEM_W))],
        out_specs=[out_spec, out_spec],
        out_shape=[out, out],
        compiler_params=_cparams(("arbitrary",)),
        name="mem_kv",
    )(mem, g.reshape(1, -1), b.reshape(1, -1), w.astype(BF16))


def _fox_in_kernel(x_ref, wq_ref, wk_ref, wv_ref, wg_ref, wfh_ref, wfl_ref, wm_ref, bf_ref,
                   qc_ref, place_ref, q_ref, k_ref, v_ref, g_ref, qm_ref, carry_ref):
    @pl.when(pl.program_id(1) == 0)
    def _():
        carry_ref[...] = jnp.zeros_like(carry_ref)

    x = x_ref[0]
    xb = x.astype(BF16)
    tm = x.shape[0]
    z = _dot3(x, wfh_ref[...], wfl_ref[...]) + bf_ref[...]
    log_f = jnp.minimum(z, 0.0) - jnp.log(1.0 + jnp.exp(-jnp.abs(z)))
    tril = (_iota((tm, tm), 1) <= _iota((tm, tm), 0)).astype(BF16)
    c = _dot_sel(tril, log_f) + carry_ref[...]
    carry_ref[...] = c[tm - 1:tm, :]
    c_hi, c_mid, c_lo = _split3(c * LOG2E)
    c_parts = jnp.concatenate([c_hi, c_mid, c_lo], axis=1)
    q_ref[0] = (_dot(xb, wq_ref[...]) + qc_ref[...]).astype(BF16)
    k_ref[0] = (_dot(xb, wk_ref[...]) + _dot(c_parts, place_ref[...])).astype(BF16)
    v_ref[0] = _dot(xb, wv_ref[...]).astype(BF16)
    g_ref[0] = _sigmoid(_dot(xb, wg_ref[...])).astype(BF16)
    qm_ref[0] = _dot(xb, wm_ref[...]).astype(BF16)


def _fox_in(x, w_in, b_f, tm):
    B, S, _ = x.shape
    scale = HEAD_DIM ** -0.5
    wq, wk, wv, wg, wf, wm = jnp.split(
        w_in, [TOK_W, 2 * TOK_W, 3 * TOK_W, 4 * TOK_W, 4 * TOK_W + N_TOK_HEADS], axis=1)

    def per_head_pad(w):
        w = w.reshape(D_MODEL, N_TOK_HEADS, HEAD_DIM)
        return jnp.pad(w, ((0, 0), (0, 0), (0, AUG_W - HEAD_DIM))).reshape(D_MODEL, N_TOK_HEADS * AUG_W)

    wq_aug = per_head_pad(wq * (scale * LOG2E)).astype(BF16)
    wk_aug = per_head_pad(wk).astype(BF16)
    wf_pad = jnp.pad(wf, ((0, 0), (0, LANES - N_TOK_HEADS)))
    wf_hi = wf_pad.astype(BF16)
    wf_lo = (wf_pad - wf_hi.astype(F32)).astype(BF16)
    bf_pad = jnp.pad(b_f, (0, LANES - N_TOK_HEADS)).reshape(1, LANES)
    col = jnp.arange(N_TOK_HEADS * AUG_W)
    is_c = (col % AUG_W >= HEAD_DIM) & (col % AUG_W < HEAD_DIM + 3)
    q_const = jnp.where(is_c, -1.0, 0.0).astype(F32).reshape(1, -1)
    row = jnp.arange(3 * LANES)
    place = ((row[:, None] % LANES == col[None, :] // AUG_W)
             & (col[None, :] % AUG_W == HEAD_DIM + row[:, None] // LANES)).astype(BF16)

    aug_w = N_TOK_HEADS * AUG_W
    row_spec = lambda w: pl.BlockSpec((1, tm, w), lambda b, i: (b, i, 0))
    return pl.pallas_call(
        _fox_in_kernel,
        grid=(B, S // tm),
        in_specs=[row_spec(D_MODEL),
                  _full_spec((D_MODEL, aug_w)), _full_spec((D_MODEL, aug_w)),
                  _full_spec((D_MODEL, TOK_W)), _full_spec((D_MODEL, TOK_W)),
                  _full_spec((D_MODEL, LANES)), _full_spec((D_MODEL, LANES)),
                  _full_spec((D_MODEL, MEM_W)), _full_spec((1, LANES)),
                  _full_spec((1, aug_w)), _full_spec((3 * LANES, aug_w))],
        out_specs=[row_spec(aug_w), row_spec(aug_w), row_spec(TOK_W), row_spec(TOK_W), row_spec(MEM_W)],
        out_shape=[jax.ShapeDtypeStruct((B, S, aug_w), BF16), jax.ShapeDtypeStruct((B, S, aug_w), BF16),
                   jax.ShapeDtypeStruct((B, S, TOK_W), BF16), jax.ShapeDtypeStruct((B, S, TOK_W), BF16),
                   jax.ShapeDtypeStruct((B, S, MEM_W), BF16)],
        scratch_shapes=[pltpu.VMEM((1, LANES), F32)],
        compiler_params=_cparams(("arbitrary", "arbitrary")),
        name="fox_in",
    )(x, wq_aug, wk_aug, wv.astype(BF16), wg.astype(BF16), wf_hi, wf_lo,
      (wm * scale).astype(BF16), bf_pad, q_const, place)


def _fox_attn_kernel(q_ref, k_ref, v_ref, g_ref, o_ref, sa_ref, sb_ref, *, tq):
    qi = pl.program_id(2)
    qs = [q_ref[0, :, h * AUG_W:(h + 1) * AUG_W] for h in range(2)]

    def scores(kj, s_ref):
        start = pl.multiple_of(kj * tq, tq)
        for h in range(2):
            s_ref[h] = _dot_nt(qs[h], k_ref[0, pl.ds(start, tq), h * AUG_W:(h + 1) * AUG_W])

    def absorb(kj, s_ref, carry, masked):
        start = pl.multiple_of(kj * tq, tq)
        v = v_ref[0, pl.ds(start, tq), :]
        out = []
        for h in range(2):
            m, l, acc = carry[h]
            s = s_ref[h]
            if masked:
                s = jnp.where(_iota((tq, tq), 1) <= _iota((tq, tq), 0), s, -jnp.inf)
            m_new = jnp.maximum(m, jnp.max(s, axis=1, keepdims=True))
            p = jnp.exp2(s - m_new)
            alpha = jnp.exp2(m - m_new)
            p_lanes = functools.reduce(
                jnp.add, [p[:, i * LANES:(i + 1) * LANES] for i in range(tq // LANES)])
            l = alpha * l + p_lanes
            acc = alpha * acc + _dot(p.astype(BF16), v)
            out.append((m_new, l, acc))
        return tuple(out)

    def pair(i, carry):
        scores(2 * i + 1, sb_ref)
        carry = absorb(2 * i, sa_ref, carry, masked=False)
        scores(2 * i + 2, sa_ref)
        return absorb(2 * i + 1, sb_ref, carry, masked=False)

    def odd_tail(carry):
        scores(qi, sb_ref)
        carry = absorb(qi - 1, sa_ref, carry, masked=False)
        return absorb(qi, sb_ref, carry, masked=True)

    def even_tail(carry):
        return absorb(qi, sa_ref, carry, masked=True)

    init = (jnp.full((tq, 1), -jnp.inf, F32), jnp.zeros((tq, LANES), F32), jnp.zeros((tq, LANES), F32))
    scores(0, sa_ref)
    carry = lax.fori_loop(0, jnp.right_shift(qi, 1), pair, (init, init))
    (_, l0, acc0), (_, l1, acc1) = lax.cond(jnp.bitwise_and(qi, 1) == 1, odd_tail, even_tail, carry)
    o0 = acc0 / jnp.sum(l0, axis=1, keepdims=True)
    o1 = acc1 / jnp.sum(l1, axis=1, keepdims=True)
    o = jnp.where(_iota((tq, LANES), 1) < HEAD_DIM, o0, o1)
    o_ref[0] = (o * g_ref[0].astype(F32)).astype(BF16)


def _fox_attn(q_aug, k_aug, v, g, tq):
    B, S, _ = v.shape
    return pl.pallas_call(
        functools.partial(_fox_attn_kernel, tq=tq),
        grid=(B, N_HEAD_PAIRS, S // tq),
        in_specs=[pl.BlockSpec((1, tq, 2 * AUG_W), lambda b, p, i: (b, i, p)),
                  pl.BlockSpec((1, S, 2 * AUG_W), lambda b, p, i: (b, 0, p)),
                  pl.BlockSpec((1, S, LANES), lambda b, p, i: (b, 0, p)),
                  pl.BlockSpec((1, tq, LANES), lambda b, p, i: (b, i, p))],
        out_specs=pl.BlockSpec((1, tq, LANES), lambda b, p, i: (b, i, p)),
        out_shape=jax.ShapeDtypeStruct((B, S, TOK_W), BF16),
        scratch_shapes=[pltpu.VMEM((2, tq, tq), F32), pltpu.VMEM((2, tq, tq), F32)],
        compiler_params=_cparams(("arbitrary", "arbitrary", "arbitrary")),
        name="fox_attn",
    )(q_aug, k_aug, v, g)


def _rwkv_in_kernel(x_ref, w_ref, mu_ref, w0_ref, w2_ref, a0_ref, a2_ref, g2_ref, kk_ref, ka_ref,
                    rk_ref, head_ones_ref,
                    rt_ref, at_ref, bt_ref, kt_ref, bh_ref, kh_ref, v_ref, pc_ref, bonus_ref, g_ref,
                    qm_ref, prev_ref):
    @pl.when(pl.program_id(1) == 0)
    def _():
        prev_ref[...] = jnp.zeros_like(prev_ref)

    tm = x_ref.shape[1]
    h = _dot(x_ref[0].astype(BF16), w_ref[...])
    qm_ref[0] = (h[:, RWKV_SHIFT_W:] * (HEAD_DIM ** -0.5)).astype(BF16)
    hs = h[:, :RWKV_SHIFT_W]
    row = _iota(hs.shape, 0)
    shifted = jnp.where(row == 0, prev_ref[...], pltpu.roll(hs, 1, 0))
    prev_ref[...] = hs[tm - 1:tm, :]
    hs = hs + mu_ref[...] * (shifted - hs)
    r = hs[:, :TOK_W]
    k = hs[:, TOK_W:2 * TOK_W]
    v = hs[:, 2 * TOK_W:3 * TOK_W]
    wa = hs[:, 3 * TOK_W:3 * TOK_W + LANES]
    gd = hs[:, 3 * TOK_W + LANES:]
    w = w0_ref[...] + _dot(jnp.tanh(wa).astype(BF16), w2_ref[...])
    w = -(jnp.maximum(-w, 0.0) + jnp.log(1.0 + jnp.exp(-jnp.abs(w)))) - 0.5
    log_decay = -jnp.exp(w)
    a = _sigmoid(a0_ref[...] + _dot(wa.astype(BF16), a2_ref[...]))
    g_ref[0] = _dot(_sigmoid(gd).astype(BF16), g2_ref[...])
    kk = k * kk_ref[...]
    kk = kk / jnp.maximum(jnp.sqrt(_head_sums(kk * kk, head_ones_ref[...])), 1e-12)
    k = k * (1.0 + (a - 1.0) * ka_ref[...])
    bonus_ref[0] = r * k * rk_ref[...]
    v_ref[0] = v
    a_s = -kk
    b_s = kk * a
    same_chunk = _div_pow2(_iota((tm, tm), 0), CHUNK) == _div_pow2(_iota((tm, tm), 1), CHUNK)
    incl = (same_chunk & (_iota((tm, tm), 1) <= _iota((tm, tm), 0))).astype(BF16)
    d_hi, d_mid, d_lo = _split3(log_decay)
    cw = _dot(incl, d_hi) + _dot(incl, d_mid) + _dot(incl, d_lo)
    same = same_chunk.astype(BF16)
    cw_end = _dot(same, d_hi) + _dot(same, d_mid) + _dot(same, d_lo)
    e_in = jnp.exp(cw)
    e_out = jnp.exp(-cw)
    e_end = jnp.exp(cw_end - cw)
    rt_ref[0] = (r * e_in).astype(BF16)
    at_ref[0] = (a_s * jnp.exp(cw - log_decay)).astype(BF16)
    bt_ref[0] = (b_s * e_out).astype(BF16)
    kt_ref[0] = (k * e_out).astype(BF16)
    bh_ref[0] = (b_s * e_end).astype(BF16)
    kh_ref[0] = (k * e_end).astype(BF16)
    pc_ref[0] = jnp.exp(cw_end)


def _rwkv_in(x, w_in, mu, w0, w2, a0, a2, g2, k_k, k_a, r_k, tm):
    B, S, _ = x.shape
    in_w = w_in.shape[1]
    w2_pad = jnp.concatenate([w2, jnp.zeros((AAA_LORA, TOK_W), F32)], axis=0).astype(BF16)
    a2_pad = jnp.concatenate([jnp.zeros((DECAY_LORA, TOK_W), F32), a2], axis=0).astype(BF16)
    head_ones = _head_blocks(1.0)
    vec = lambda t: t.reshape(1, -1)
    row_spec = lambda w: pl.BlockSpec((1, tm, w), lambda b, i: (b, i, 0))
    tok = jax.ShapeDtypeStruct((B, S, TOK_W), F32)
    return pl.pallas_call(
        _rwkv_in_kernel,
        grid=(B, S // tm),
        in_specs=[row_spec(D_MODEL), _full_spec((D_MODEL, in_w)), _full_spec((1, RWKV_SHIFT_W)),
                  _full_spec((1, TOK_W)), _full_spec((LANES, TOK_W)), _full_spec((1, TOK_W)),
                  _full_spec((LANES, TOK_W)), _full_spec((GATE_LORA, TOK_W)), _full_spec((1, TOK_W)),
                  _full_spec((1, TOK_W)), _full_spec((1, TOK_W)), _full_spec((MXU_W, MXU_W))],
        out_specs=[row_spec(TOK_W)] * 10 + [row_spec(MEM_W)],
        out_shape=[jax.ShapeDtypeStruct((B, S, TOK_W), BF16)] * 6 + [tok] * 4
        + [jax.ShapeDtypeStruct((B, S, MEM_W), BF16)],
        scratch_shapes=[pltpu.VMEM((1, RWKV_SHIFT_W), F32)],
        compiler_params=_cparams(("arbitrary", "arbitrary")),
        name="rwkv_in",
    )(x, w_in.astype(BF16), vec(mu), vec(w0), w2_pad, vec(a0), a2_pad, g2.astype(BF16), vec(k_k),
      vec(k_a), vec(r_k), head_ones)


def _rwkv_scan_kernel(rt_ref, at_ref, bt_ref, kt_ref, bh_ref, kh_ref, v_ref, pc_ref, y_ref, state_ref,
                      *, n_chunks):
    @pl.when(pl.program_id(2) == 0)
    def _():
        state_ref[...] = jnp.zeros_like(state_ref)

    C = CHUNK
    lane_head = _div_pow2(_iota((2 * C, LANES), 1), HEAD_DIM)
    row_head = _div_pow2(_iota((2 * C, LANES), 0), C)
    keep = lane_head == row_head
    pos_r = _mod_pow2(_iota((2 * C, 2 * C), 0), C)
    pos_c = _mod_pow2(_iota((2 * C, 2 * C), 1), C)
    strict = pos_c < pos_r
    incl = pos_c <= pos_r
    eye = (_iota((2 * C, 2 * C), 0) == _iota((2 * C, 2 * C), 1)).astype(F32)

    def stack(ref, c):
        t = ref[0, c * C:(c + 1) * C, :]
        return jnp.where(keep, jnp.concatenate([t, t], axis=0), 0.0).astype(BF16)

    chunks = range(n_chunks)
    a2 = [stack(at_ref, c) for c in chunks]
    r2 = [stack(rt_ref, c) for c in chunks]
    bh2 = [stack(bh_ref, c) for c in chunks]
    kh2 = [stack(kh_ref, c) for c in chunks]
    v2 = [stack(v_ref, c) for c in chunks]
    gram = [_dot_nt(jnp.concatenate([a2[c], r2[c]], axis=0),
                    jnp.concatenate([stack(bt_ref, c), stack(kt_ref, c)], axis=0)) for c in chunks]
    power = [jnp.where(strict, gram[c][:2 * C, :2 * C], 0.0) for c in chunks]
    inv = [eye + power[c] for c in chunks]
    for _ in range(int(math.log2(C)) - 1):
        power = [_dot(p.astype(BF16), p.astype(BF16)) for p in power]
        inv = [inv[c] + _dot(inv[c].astype(BF16), power[c].astype(BF16)) for c in chunks]
    from_v = [_dot(jnp.concatenate([jnp.where(strict, gram[c][:2 * C, 2 * C:], 0.0),
                                    jnp.where(incl, gram[c][2 * C:, 2 * C:], 0.0)], axis=0).astype(BF16),
                   v2[c]) for c in chunks]
    wu = [_dot(inv[c].astype(BF16),
               jnp.concatenate([a2[c], from_v[c][:2 * C].astype(BF16)], axis=1)) for c in chunks]
    wy = [_dot(jnp.where(incl, gram[c][2 * C:, :2 * C], 0.0).astype(BF16), wu[c].astype(BF16))
          for c in chunks]
    w_y = [(r2[c].astype(F32) + wy[c][:, :LANES]).astype(BF16) for c in chunks]
    y0 = [from_v[c][2 * C:] + wy[c][:, LANES:] for c in chunks]
    gc = [_dot(wu[c].T.astype(BF16), bh2[c]) for c in chunks]
    c0 = [gc[c][LANES:] + _dot(v2[c].astype(F32).T.astype(BF16), kh2[c]) for c in chunks]
    state = state_ref[...]
    for c in chunks:
        sb = state.astype(BF16)
        y2 = _dot_nt(w_y[c], sb) + y0[c]
        y_ref[0, c * C:(c + 1) * C, :] = y2[:C] + y2[C:]
        state = state * pc_ref[0, c * C:c * C + 1, :] + _dot(sb, gc[c][:LANES].astype(BF16)) + c0[c]
    state_ref[...] = state


def _rwkv_scan(rt, at, bt, kt, bh, kh, v, pc, rows):
    B, S, _ = v.shape
    spec = pl.BlockSpec((1, rows, LANES), lambda b, p, i: (b, i, p))
    return pl.pallas_call(
        functools.partial(_rwkv_scan_kernel, n_chunks=rows // CHUNK),
        grid=(B, N_HEAD_PAIRS, S // rows),
        in_specs=[spec] * 8,
        out_specs=spec,
        out_shape=jax.ShapeDtypeStruct((B, S, TOK_W), F32),
        scratch_shapes=[pltpu.VMEM((LANES, LANES), F32)],
        compiler_params=_cparams(("arbitrary", "arbitrary", "arbitrary")),
        name="rwkv_scan",
    )(rt, at, bt, kt, bh, kh, v, pc)


def _rwkv_post_kernel(y_ref, bonus_ref, v_ref, g_ref, lg_ref, lb_ref, head_mean_ref, o_ref):
    y = y_ref[0]
    head_mean = functools.partial(_head_sums, blocks=head_mean_ref[...])

    yc = y - head_mean(y)
    var = head_mean(yc * yc)
    yn = yc * lax.rsqrt(var + GN_EPS) * lg_ref[...] + lb_ref[...]
    bonus = head_mean(bonus_ref[0]) * float(HEAD_DIM)
    o_ref[0] = ((yn + bonus * v_ref[0]) * g_ref[0]).astype(BF16)


def _rwkv_post(y, bonus, v, g, lnx_g, lnx_b, tm):
    B, S, _ = y.shape
    head_mean = _head_blocks(1.0 / HEAD_DIM)
    row_spec = pl.BlockSpec((1, tm, TOK_W), lambda b, i: (b, i, 0))
    return pl.pallas_call(
        _rwkv_post_kernel,
        grid=(B, S // tm),
        in_specs=[row_spec] * 4 + [_full_spec((1, TOK_W)), _full_spec((1, TOK_W)),
                                   _full_spec((MXU_W, MXU_W))],
        out_specs=row_spec,
        out_shape=jax.ShapeDtypeStruct((B, S, TOK_W), BF16),
        compiler_params=_cparams(("arbitrary", "arbitrary")),
        name="rwkv_post",
    )(y, bonus, v, g, lnx_g.reshape(1, -1), lnx_b.reshape(1, -1), head_mean)


def _mix_out_kernel(tok_ref, qm_ref, km_ref, vm_ref, x_ref, wo_tok_ref, wo_mem_ref, g_ref, b_ref,
                    wr_hi_ref, wr_lo_ref, rb_ref, x1_ref, x1b_ref, gate_ref):
    tm = x_ref.shape[1]
    groups = [slice(r, r + tm // MIX_GROUPS) for r in range(0, tm, tm // MIX_GROUPS)]
    mem_out = [None] * MIX_GROUPS
    for h in range(N_MEM_HEADS):
        for i, rows in enumerate(groups):
            s = _dot_nt(qm_ref[0, rows, :], km_ref[0, h])
            e = jnp.exp(s - jnp.max(s, axis=1, keepdims=True))
            o = _dot(e.astype(BF16), vm_ref[0, h]) / jnp.sum(e, axis=1, keepdims=True)
            mem_out[i] = o if mem_out[i] is None else mem_out[i] + o
    x1 = []
    for i, rows in enumerate(groups):
        mixed = _dot(tok_ref[0, rows, :], wo_tok_ref[...]) + _dot(mem_out[i].astype(BF16), wo_mem_ref[...])
        x1.append(_layer_norm(DEEPNORM_ALPHA * x_ref[0, rows, :] + mixed, g_ref[...], b_ref[...]))
        x1_ref[0, rows, :] = x1[i]
        x1b_ref[0, rows, :] = x1[i].astype(BF16)
    scores = [_sigmoid(_dot3(x1[i], wr_hi_ref[...], wr_lo_ref[...])) for i in range(MIX_GROUPS)]
    lane = _iota(scores[0].shape, 1)
    lane_f = lane.astype(F32)
    cand = [jnp.where(lane < N_EXPERTS, sc + rb_ref[...], -jnp.inf) for sc in scores]
    picked = [jnp.zeros(scores[0].shape, jnp.bool_)] * MIX_GROUPS
    for _ in range(TOP_K):
        for i in range(MIX_GROUPS):
            best = jnp.max(cand[i], axis=1, keepdims=True)
            first = jnp.min(jnp.where(cand[i] == best, lane_f, float(LANES)), axis=1, keepdims=True)
            hit = lane_f == first
            picked[i] = jnp.logical_or(picked[i], hit)
            cand[i] = jnp.where(hit, -jnp.inf, cand[i])
    for i, rows in enumerate(groups):
        chosen = jnp.where(picked[i], scores[i], 0.0)
        gate = chosen / jnp.sum(chosen, axis=1, keepdims=True) * ROUTED_SCALE
        gate_ref[0, rows, :] = jnp.where(lane == N_EXPERTS, 1.0, gate)


def _mix_out(tok, qm, km, vm, x, w_out, ln_g, ln_b, w_router, router_bias, tm):
    B, S, _ = x.shape
    n_mem = km.shape[2]
    wr = jnp.pad(w_router, ((0, 0), (0, LANES - N_EXPERTS)))
    wr_hi = wr.astype(BF16)
    wr_lo = (wr - wr_hi.astype(F32)).astype(BF16)
    rb = jnp.pad(router_bias, (0, LANES - N_EXPERTS)).reshape(1, LANES)
    row_spec = lambda w: pl.BlockSpec((1, tm, w), lambda b, i: (b, i, 0))
    mem_spec = pl.BlockSpec((1, N_MEM_HEADS, n_mem, MEM_W), lambda b, i: (b, 0, 0, 0))
    return pl.pallas_call(
        _mix_out_kernel,
        grid=(B, S // tm),
        in_specs=[row_spec(TOK_W), row_spec(MEM_W), mem_spec, mem_spec, row_spec(D_MODEL),
                  _full_spec((TOK_W, D_MODEL)), _full_spec((MEM_W, D_MODEL)),
                  _full_spec((1, D_MODEL)), _full_spec((1, D_MODEL)),
                  _full_spec((D_MODEL, LANES)), _full_spec((D_MODEL, LANES)), _full_spec((1, LANES))],
        out_specs=[row_spec(D_MODEL), row_spec(D_MODEL), row_spec(LANES)],
        out_shape=[jax.ShapeDtypeStruct((B, S, D_MODEL), F32), jax.ShapeDtypeStruct((B, S, D_MODEL), BF16),
                   jax.ShapeDtypeStruct((B, S, LANES), F32)],
        compiler_params=_cparams(("arbitrary", "arbitrary")),
        name="mix_out",
    )(tok, qm, km, vm, x, w_out[:TOK_W].astype(BF16), w_out[TOK_W:].astype(BF16),
      ln_g.reshape(1, -1), ln_b.reshape(1, -1), wr_hi, wr_lo, rb)


MOE_TM = 512
UNIT = 16
SLAB_ROWS = 4096
SLAB_UNITS = SLAB_ROWS // UNIT
BLK_UNITS = 64
BLK_ROWS = BLK_UNITS * UNIT
XS_W = D_MODEL + 2 * LANES
ROW_CHUNK = 1024
assert TOP_K * MOE_TM + N_EXPERTS * (UNIT - 1) <= SLAB_ROWS - UNIT
ZERO_UNIT = SLAB_UNITS - 1


def _dot_x_sel(x, sel):
    hi, mid, lo = _split3(x)
    return _dot(hi, sel) + _dot(mid, sel) + _dot(lo, sel)


def _route_tile(gates):
    tm = gates.shape[0]
    sel = jnp.logical_and(gates > 0.0, _iota(gates.shape, 1) < N_EXPERTS)
    sel_b = jnp.where(sel, 1.0, 0.0).astype(BF16)
    earlier = (_iota((tm, tm), 1) < _iota((tm, tm), 0)).astype(BF16)
    rank1 = jnp.where(sel, _dot(earlier, sel_b) + 1.0, 0.0)
    count = _dot(jnp.ones((8, tm), BF16), sel_b)[0:1]
    n16 = jnp.floor((count + float(UNIT - 1)) * (1.0 / UNIT))
    before = (_iota((LANES, LANES), 0) < _iota((LANES, LANES), 1)).astype(BF16)
    off16 = _dot(jnp.broadcast_to(n16, (8, LANES)).astype(BF16), before)[0:1]
    return sel, rank1, n16, off16


def _slab_experts(n16, off16, r0, rows):
    r = (_iota((rows, LANES), 0) + r0).astype(F32)
    lo = off16 * float(UNIT)
    return jnp.logical_and(r >= lo, r < lo + n16 * float(UNIT))


def _moe_dispatch_kernel(xb_ref, gate_ref, xs_ref, n_ref):
    gates = gate_ref[...]
    _, rank1, n16, off16 = _route_tile(gates)
    n_ref[0] = jnp.broadcast_to(n16, (8, LANES))
    r_hi, r_lo = _split2(rank1)
    g_hi, g_lo = _split2(gates)
    src = jnp.concatenate([xb_ref[...], g_hi, g_lo], axis=1)
    for c in range(SLAB_ROWS // ROW_CHUNK):
        r0 = c * ROW_CHUNK
        seg = _slab_experts(n16, off16, r0, ROW_CHUNK)
        seg_b = jnp.where(seg, 1.0, 0.0).astype(BF16)
        rank_at = _dot_nt(seg_b, r_hi) + _dot_nt(seg_b, r_lo)
        r = (_iota((ROW_CHUNK, 1), 0) + r0).astype(F32)
        pos1 = r + 1.0 - jnp.sum(jnp.where(seg, off16 * float(UNIT), 0.0), axis=1, keepdims=True)
        pick = jnp.where(rank_at == pos1, 1.0, 0.0).astype(BF16)
        xs_ref[r0:r0 + ROW_CHUNK, :] = _dot(pick, src).astype(BF16)


def _moe_dispatch(xb, gates):
    T = xb.shape[0]
    nt = T // MOE_TM
    return pl.pallas_call(
        _moe_dispatch_kernel,
        grid=(nt,),
        in_specs=[pl.BlockSpec((MOE_TM, D_MODEL), lambda i: (i, 0)),
                  pl.BlockSpec((MOE_TM, LANES), lambda i: (i, 0))],
        out_specs=[pl.BlockSpec((SLAB_ROWS, XS_W), lambda i: (i, 0)),
                   pl.BlockSpec((1, 8, LANES), lambda i: (i, 0, 0))],
        out_shape=[jax.ShapeDtypeStruct((nt * SLAB_ROWS, XS_W), BF16),
                   jax.ShapeDtypeStruct((nt, 8, LANES), F32)],
        compiler_params=_cparams(("arbitrary",)),
        name="moe_dispatch",
    )(xb, gates)


def _moe_plan_kernel(n_ref, unit_ref, expert_ref, count_ref, *, nt, nb):
    n16 = n_ref[...]
    lane = _iota((nt, LANES), 1)
    before = (_iota((LANES, LANES), 0) < _iota((LANES, LANES), 1)).astype(BF16)
    off16 = _dot(n16.astype(BF16), before)
    used = jnp.sum(n16, axis=1, keepdims=True)
    n16 = n16 + jnp.where(lane == N_EXPERTS - 1, float(SLAB_UNITS) - used, 0.0)
    earlier = (_iota((nt, nt), 1) < _iota((nt, nt), 0)).astype(BF16)
    cum_ex = _dot_sel(earlier, n16)
    cum_in = cum_ex + n16
    total = cum_in[nt - 1:nt, :]
    n_blk = jnp.floor((total + float(BLK_UNITS - 1)) * (1.0 / BLK_UNITS))
    blk_start = _dot_x_sel(jnp.broadcast_to(n_blk, (8, LANES)), before)[0:1]
    blk_end = blk_start + n_blk
    count_ref[...] = jnp.broadcast_to(jnp.sum(n_blk, axis=1, keepdims=True), (8, LANES)).astype(jnp.int32)

    lane_b = _iota((nb, LANES), 1)
    blk = _iota((nb, LANES), 0).astype(F32)
    expert = jnp.sum(jnp.where(jnp.logical_and(lane_b < N_EXPERTS, blk_end <= blk), 1.0, 0.0),
                     axis=1, keepdims=True)
    live = expert < float(N_EXPERTS)
    expert = jnp.minimum(expert, float(N_EXPERTS - 1))
    expert_ref[...] = jnp.broadcast_to(expert, (nb, LANES)).astype(jnp.int32)
    mine = lane_b.astype(F32) == expert
    mine_b = jnp.where(mine, 1.0, 0.0).astype(BF16)
    my_start = jnp.sum(jnp.where(mine, blk_start, 0.0), axis=1, keepdims=True)
    my_total = jnp.sum(jnp.where(mine, total, 0.0), axis=1, keepdims=True)
    q = (blk - my_start) * float(BLK_UNITS) + lane_b.astype(F32)
    valid = jnp.logical_and(jnp.logical_and(q < my_total, lane_b < BLK_UNITS), live)

    def per_slab(table):
        hi, mid, lo = _split3(table)
        return _dot_nt(mine_b, hi) + _dot_nt(mine_b, mid) + _dot_nt(mine_b, lo)

    ex, inc, first = per_slab(cum_ex), per_slab(cum_in), per_slab(off16)
    base = jnp.zeros((nb, LANES), F32)
    for i in range(nt):
        hit = jnp.logical_and(q >= ex[:, i:i + 1], q < inc[:, i:i + 1])
        base = base + jnp.where(hit, float(i * SLAB_UNITS) + first[:, i:i + 1] - ex[:, i:i + 1], 0.0)
    unit_ref[...] = jnp.where(valid, base + q, -1.0).astype(jnp.int32)


def _moe_plan(n16, nb):
    nt = n16.shape[0]
    tbl = jax.ShapeDtypeStruct((nb, LANES), jnp.int32)
    units, experts, count = pl.pallas_call(
        functools.partial(_moe_plan_kernel, nt=nt, nb=nb),
        out_shape=[tbl, tbl, jax.ShapeDtypeStruct((8, LANES), jnp.int32)],
        compiler_params=pltpu.CompilerParams(vmem_limit_bytes=VMEM_LIMIT),
        name="moe_plan",
    )(n16)
    return units[:, :BLK_UNITS].reshape(-1), experts[:, 0], count[0, :1]


def _moe_ffn_kernel(unit_ref, expert_ref, count_ref, xs_hbm, wg_ref, wu_ref, wd_ref, ys_hbm,
                    ibuf, obuf, in_sem, out_sem, *, nb):
    b = pl.program_id(0)
    n_live = count_ref[0]
    slot = jnp.bitwise_and(b, 1)

    def in_copy(blk, buf, s):
        u = unit_ref[blk * BLK_UNITS + s]
        u = jnp.where(u < 0, ZERO_UNIT, u)
        return pltpu.make_async_copy(xs_hbm.at[pl.ds(pl.multiple_of(u * UNIT, UNIT), UNIT), :],
                                     ibuf.at[buf, pl.ds(s * UNIT, UNIT), :], in_sem.at[buf])

    def for_each_out(blk, buf, act):
        for s in range(BLK_UNITS):
            u = unit_ref[blk * BLK_UNITS + s]

            @pl.when(u >= 0)
            def _():
                act(pltpu.make_async_copy(
                    obuf.at[buf, pl.ds(s * UNIT, UNIT), :],
                    ys_hbm.at[pl.ds(pl.multiple_of(u * UNIT, UNIT), UNIT), :], out_sem.at[buf]))

    def start_in(blk, buf):
        for s in range(BLK_UNITS):
            in_copy(blk, buf, s).start()

    @pl.when(b == 0)
    def _():
        start_in(0, 0)

    @pl.when(b <= n_live)
    def _():
        for s in range(BLK_UNITS):
            in_copy(b, slot, s).wait()

    @pl.when(jnp.logical_and(b >= 2, b - 2 < n_live))
    def _():
        for_each_out(b - 2, slot, lambda c: c.wait())

    @pl.when(b < n_live)
    def _():
        @pl.when(b + 1 < nb)
        def _():
            start_in(b + 1, 1 - slot)

        rows = ibuf[slot]
        x = rows[:, :D_MODEL]
        gates = rows[:, D_MODEL:D_MODEL + LANES].astype(F32) + rows[:, D_MODEL + LANES:].astype(F32)
        e = expert_ref[b]
        gate = jnp.sum(jnp.where(_iota(gates.shape, 1) == e, gates, 0.0), axis=1, keepdims=True)
        hg = _dot(x, wg_ref[0, 0].astype(BF16))
        hu = _dot(x, wu_ref[0, 0].astype(BF16))
        hidden = hg * _sigmoid(hg) * hu * gate
        obuf[slot] = _dot(hidden.astype(BF16), wd_ref[0, 0].astype(BF16)).astype(BF16)
        for_each_out(b, slot, lambda c: c.start())

    @pl.when(b == nb - 1)
    def _():
        @pl.when(nb - 2 < n_live)
        def _():
            for_each_out(nb - 2, 1 - slot, lambda c: c.wait())

        @pl.when(nb - 1 < n_live)
        def _():
            for_each_out(nb - 1, slot, lambda c: c.wait())


def _moe_ffn(xs, units, experts, count, w_gate, w_up, w_down, layer, nb):
    n_rows = xs.shape[0]
    w_spec = lambda r, c: pl.BlockSpec((1, 1, r, c), lambda b, u, e, n: (layer, e[b], 0, 0))
    grid_spec = pltpu.PrefetchScalarGridSpec(
        num_scalar_prefetch=3,
        grid=(nb,),
        in_specs=[pl.BlockSpec(memory_space=pl.ANY),
                  w_spec(D_MODEL, D_EXPERT), w_spec(D_MODEL, D_EXPERT), w_spec(D_EXPERT, D_MODEL)],
        out_specs=pl.BlockSpec(memory_space=pl.ANY),
        scratch_shapes=[pltpu.VMEM((2, BLK_ROWS, XS_W), BF16), pltpu.VMEM((2, BLK_ROWS, D_MODEL), BF16),
                        pltpu.SemaphoreType.DMA((2,)), pltpu.SemaphoreType.DMA((2,))],
    )
    return pl.pallas_call(
        functools.partial(_moe_ffn_kernel, nb=nb),
        grid_spec=grid_spec,
        out_shape=jax.ShapeDtypeStruct((n_rows, D_MODEL), BF16),
        compiler_params=_cparams(("arbitrary",)),
        name="moe_ffn",
    )(units, experts, count, xs, w_gate, w_up, w_down)


def _moe_combine_kernel(ys_ref, gate_ref, x_ref, xb_ref, wsg_ref, wsu_ref, wsd_ref, g_ref, b_ref, o_ref):
    _, rank1, n16, off16 = _route_tile(gate_ref[...])
    r_hi, r_lo = _split2(rank1)
    off_b = jnp.broadcast_to(off16, (8, LANES)).astype(BF16)
    xb = xb_ref[...]
    hg = _dot(xb, wsg_ref[...])
    acc = _dot((hg * _sigmoid(hg) * _dot(xb, wsu_ref[...])).astype(BF16), wsd_ref[...])
    for c in range(SLAB_ROWS // ROW_CHUNK):
        r0 = c * ROW_CHUNK
        seg_b = jnp.where(_slab_experts(n16, off16, r0, ROW_CHUNK), 1.0, 0.0).astype(BF16)
        rank_at = _dot_nt(r_hi, seg_b) + _dot_nt(r_lo, seg_b)
        r = (_iota((1, ROW_CHUNK), 1) + r0).astype(F32)
        pos1 = r + 1.0 - _dot_nt(off_b, seg_b)[0:1] * float(UNIT)
        pick = jnp.where(rank_at == pos1, 1.0, 0.0).astype(BF16)
        acc = acc + _dot(pick, ys_ref[r0:r0 + ROW_CHUNK, :])
    o_ref[...] = _layer_norm(DEEPNORM_ALPHA * x_ref[...] + acc, g_ref[...], b_ref[...])


def _moe_combine(ys, gates, x, xb, w_sh_gate, w_sh_up, w_sh_down, ln_g, ln_b):
    T = x.shape[0]
    row_spec = lambda w: pl.BlockSpec((MOE_TM, w), lambda i: (i, 0))
    return pl.pallas_call(
        _moe_combine_kernel,
        grid=(T // MOE_TM,),
        in_specs=[pl.BlockSpec((SLAB_ROWS, D_MODEL), lambda i: (i, 0)),
                  row_spec(LANES), row_spec(D_MODEL), row_spec(D_MODEL),
                  _full_spec((D_MODEL, D_EXPERT)), _full_spec((D_MODEL, D_EXPERT)),
                  _full_spec((D_EXPERT, D_MODEL)), _full_spec((1, D_MODEL)), _full_spec((1, D_MODEL))],
        out_specs=row_spec(D_MODEL),
        out_shape=jax.ShapeDtypeStruct((T, D_MODEL), F32),
        compiler_params=_cparams(("arbitrary",)),
        name="moe_combine",
    )(ys, gates, x, xb, w_sh_gate.astype(BF16), w_sh_up.astype(BF16), w_sh_down.astype(BF16),
      ln_g.reshape(1, -1), ln_b.reshape(1, -1))


def _moe(xb, x, gates, w_gate, w_up, w_down, layer, w_sh_gate, w_sh_up, w_sh_down, ln_g, ln_b):
    T = x.shape[0]
    nt = T // MOE_TM
    nb = nt * SLAB_UNITS // BLK_UNITS + N_EXPERTS
    xs, n16 = _moe_dispatch(xb, gates)
    units, experts, count = _moe_plan(n16[:, 0, :], nb)
    ys = _moe_ffn(xs, units, experts, count, w_gate, w_up, w_down, layer, nb)
    return _moe_combine(ys, gates, x, xb, w_sh_gate, w_sh_up, w_sh_down, ln_g, ln_b)


def _tile(n, want):
    t = min(n, want)
    assert n % t == 0, (n, t)
    return t


def kernel(x, mem, mem_ln_g, mem_ln_b, w_mem_kv, fox_w_in, fox_b_f, rwkv_w_in, rwkv_mu, rwkv_w0, rwkv_w2,
           rwkv_a0, rwkv_a2, rwkv_g2, rwkv_k_k, rwkv_k_a, rwkv_r_k, rwkv_lnx_g, rwkv_lnx_b, w_out, ln1_g,
           ln1_b, w_router, router_bias, w_exp_gate, w_exp_up, w_exp_down, w_sh_gate, w_sh_up, w_sh_down,
           ln2_g, ln2_b):
    B, S, D = x.shape
    T = B * S
    assert D == D_MODEL and S % CHUNK == 0 and T % MOE_TM == 0
    t_proj = _tile(S, 512)
    t_attn = _tile(S, 512)
    t_rwkv = _tile(S, 256)
    t_scan = _tile(S, 512)

    km, vm = _mem_kv(mem, mem_ln_g, mem_ln_b, w_mem_kv)
    for i in range(DEPTH):
        j = i // 2
        if i % 2 == 0:
            q_aug, k_aug, v, g, qm = _fox_in(x, fox_w_in[j], fox_b_f[j], t_proj)
            tok = _fox_attn(q_aug, k_aug, v, g, t_attn)
        else:
            (rt, at, bt, kt, bh, kh, v, pc, bonus, g, qm) = _rwkv_in(
                x, rwkv_w_in[j], rwkv_mu[j], rwkv_w0[j], rwkv_w2[j], rwkv_a0[j], rwkv_a2[j], rwkv_g2[j],
                rwkv_k_k[j], rwkv_k_a[j], rwkv_r_k[j].reshape(-1), t_rwkv)
            y = _rwkv_scan(rt, at, bt, kt, bh, kh, v, pc, t_scan)
            tok = _rwkv_post(y, bonus, v, g, rwkv_lnx_g[j], rwkv_lnx_b[j], t_proj)
        x1, x1b, gates = _mix_out(tok, qm, km, vm, x, w_out[i], ln1_g[i], ln1_b[i], w_router[i],
                                  router_bias[i], t_proj)
        x = _moe(x1b.reshape(T, D), x1.reshape(T, D), gates.reshape(T, LANES), w_exp_gate, w_exp_up,
                 w_exp_down, i, w_sh_gate[i], w_sh_up[i], w_sh_down[i], ln2_g[i], ln2_b[i]).reshape(B, S, D)
    return x
```

```python
import functools
import math

import jax
import jax.numpy as jnp
from jax import lax
from jax.experimental import pallas as pl
from jax.experimental.pallas import tpu as pltpu

F32 = jnp.float32
BF16 = jnp.bfloat16

D_MODEL = 1024
HEAD_DIM = 64
N_TOK_HEADS = 12
TOK_W = N_TOK_HEADS * HEAD_DIM
N_HEAD_PAIRS = N_TOK_HEADS // 2
N_MEM_HEADS = 4
MEM_W = N_MEM_HEADS * HEAD_DIM
DECAY_LORA = 64
AAA_LORA = 64
GATE_LORA = 128
RWKV_SHIFT_W = 3 * TOK_W + DECAY_LORA + AAA_LORA + GATE_LORA
N_EXPERTS = 64
TOP_K = 6
D_EXPERT = 256
ROUTED_SCALE = 2.5
DEPTH = 2
DEEPNORM_ALPHA = (2 * DEPTH) ** 0.25
LN_EPS = 1e-5
GN_EPS = 64e-5
LOG2E = math.log2(math.e)

LANES = 128
ROW_STRIP = 64
C_PARTS = 3
CHUNK = 64
MIX_GROUPS = 2
VMEM_LIMIT = 56 * 1024 * 1024


def _cparams(sem):
    return pltpu.CompilerParams(dimension_semantics=sem, vmem_limit_bytes=VMEM_LIMIT)


def _dot(a, b):
    return jnp.dot(a, b, preferred_element_type=F32)


def _dot_nt(a, b):
    return lax.dot_general(a, b, (((1,), (1,)), ((), ())), preferred_element_type=F32)


def _split2(x):
    hi = x.astype(BF16)
    lo = (x - hi.astype(F32)).astype(BF16)
    return hi, lo


def _split3(x):
    hi = x.astype(BF16)
    r1 = x - hi.astype(F32)
    mid = r1.astype(BF16)
    lo = (r1 - mid.astype(F32)).astype(BF16)
    return hi, mid, lo


def _dot_sel(sel, x):
    hi, mid, lo = _split3(x)
    return _dot(sel, hi) + _dot(sel, mid) + _dot(sel, lo)


def _dot3(x, w_hi, w_lo):
    x_hi, x_lo = _split2(x)
    return _dot(x_hi, w_hi) + _dot(x_lo, w_hi) + _dot(x_hi, w_lo)


MXU_W = 256


def _head_blocks(value):
    head = jnp.arange(MXU_W) // HEAD_DIM
    return jnp.where(head[:, None] == head[None, :], value, 0.0).astype(BF16)


def _head_sums(t, blocks):
    cols = []
    for c in range(0, t.shape[1], MXU_W):
        hi, lo = _split2(t[:, c:c + MXU_W])
        cols.append(_dot(hi, blocks) + _dot(lo, blocks))
    return jnp.concatenate(cols, axis=1)


def _sigmoid(x):
    return 1.0 / (1.0 + jnp.exp(-x))


def _layer_norm(y, g, b):
    mu = jnp.mean(y, axis=-1, keepdims=True)
    yc = y - mu
    var = jnp.mean(yc * yc, axis=-1, keepdims=True)
    return yc * lax.rsqrt(var + LN_EPS) * g + b


def _iota(shape, dim):
    return lax.broadcasted_iota(jnp.int32, shape, dim)


def _div_pow2(x, n):
    return jnp.right_shift(x, int(math.log2(n)))


def _mod_pow2(x, n):
    return jnp.bitwise_and(x, n - 1)


def _full_spec(shape):
    n = len(shape)
    return pl.BlockSpec(shape, lambda *_: (0,) * n)


def _mem_kv_kernel(mem_ref, g_ref, b_ref, w_ref, k_ref, v_ref):
    m = _layer_norm(mem_ref[0], g_ref[...], b_ref[...])
    kv = _dot(m.astype(BF16), w_ref[...])
    k = kv[:, :MEM_W]
    v = kv[:, MEM_W:]
    head = _div_pow2(_iota(k.shape, 1), HEAD_DIM)
    for h in range(N_MEM_HEADS):
        k_ref[0, h] = jnp.where(head == h, k, 0.0).astype(BF16)
        v_ref[0, h] = jnp.where(head == h, v, 0.0).astype(BF16)


def _mem_kv(mem, g, b, w):
    B, n_mem, _ = mem.shape
    out = jax.ShapeDtypeStruct((B, N_MEM_HEADS, n_mem, MEM_W), BF16)
    out_spec = pl.BlockSpec((1, N_MEM_HEADS, n_mem, MEM_W), lambda i: (i, 0, 0, 0))
    return pl.pallas_call(
        _mem_kv_kernel,
        grid=(B,),
        in_specs=[pl.BlockSpec((1, n_mem, D_MODEL), lambda i: (i, 0, 0)),
                  _full_spec((1, D_MODEL)), _full_spec((1, D_MODEL)),
                  _full_spec((D_MODEL, 2 * MEM_W))],
        out_specs=[out_spec, out_spec],
        out_shape=[out, out],
        compiler_params=_cparams(("arbitrary",)),
        name="mem_kv",
    )(mem, g.reshape(1, -1), b.reshape(1, -1), w.astype(BF16))


def _fox_in_kernel(x_ref, wq_ref, wk_ref, wv_ref, wg_ref, wfh_ref, wfl_ref, wm_ref, bf_ref,
                   place_ref, q_ref, k_ref, kc_ref, v_ref, g_ref, qm_ref, carry_ref):
    @pl.when(pl.program_id(1) == 0)
    def _():
        carry_ref[...] = jnp.zeros_like(carry_ref)

    x = x_ref[0]
    xb = x.astype(BF16)
    tm = x.shape[0]
    z = _dot3(x, wfh_ref[...], wfl_ref[...]) + bf_ref[...]
    log_f = jnp.minimum(z, 0.0) - jnp.log(1.0 + jnp.exp(-jnp.abs(z)))
    tril = (_iota((tm, tm), 1) <= _iota((tm, tm), 0)).astype(BF16)
    c = _dot_sel(tril, log_f) + carry_ref[...]
    carry_ref[...] = c[tm - 1:tm, :]
    c_hi, c_mid, c_lo = _split3(c * LOG2E)
    c_parts = jnp.concatenate([c_hi, c_mid, c_lo], axis=1)
    q_ref[0] = _dot(xb, wq_ref[...]).astype(BF16)
    k_ref[0] = _dot(xb, wk_ref[...]).astype(BF16)
    kc_ref[0] = _dot(c_parts, place_ref[...]).astype(BF16)
    v_ref[0] = _dot(xb, wv_ref[...]).astype(BF16)
    g_ref[0] = _sigmoid(_dot(xb, wg_ref[...])).astype(BF16)
    qm_ref[0] = _dot(xb, wm_ref[...]).astype(BF16)


def _fox_in(x, w_in, b_f, tm):
    B, S, _ = x.shape
    scale = HEAD_DIM ** -0.5
    wq, wk, wv, wg, wf, wm = jnp.split(
        w_in, [TOK_W, 2 * TOK_W, 3 * TOK_W, 4 * TOK_W, 4 * TOK_W + N_TOK_HEADS], axis=1)
    wf_pad = jnp.pad(wf, ((0, 0), (0, LANES - N_TOK_HEADS)))
    wf_hi = wf_pad.astype(BF16)
    wf_lo = (wf_pad - wf_hi.astype(F32)).astype(BF16)
    bf_pad = jnp.pad(b_f, (0, LANES - N_TOK_HEADS)).reshape(1, LANES)
    row = jnp.arange(C_PARTS * LANES)
    col = jnp.arange(N_HEAD_PAIRS * LANES)
    head, part = row[:, None] % LANES, row[:, None] // LANES
    place = ((head // 2 == col[None, :] // LANES)
             & (col[None, :] % LANES == C_PARTS * (head % 2) + part)).astype(BF16)

    row_spec = lambda w: pl.BlockSpec((1, tm, w), lambda b, i: (b, i, 0))
    tok = jax.ShapeDtypeStruct((B, S, TOK_W), BF16)
    return pl.pallas_call(
        _fox_in_kernel,
        grid=(B, S // tm),
        in_specs=[row_spec(D_MODEL),
                  _full_spec((D_MODEL, TOK_W)), _full_spec((D_MODEL, TOK_W)),
                  _full_spec((D_MODEL, TOK_W)), _full_spec((D_MODEL, TOK_W)),
                  _full_spec((D_MODEL, LANES)), _full_spec((D_MODEL, LANES)),
                  _full_spec((D_MODEL, MEM_W)), _full_spec((1, LANES)),
                  _full_spec((C_PARTS * LANES, N_HEAD_PAIRS * LANES))],
        out_specs=[row_spec(TOK_W)] * 5 + [row_spec(MEM_W)],
        out_shape=[tok] * 5 + [jax.ShapeDtypeStruct((B, S, MEM_W), BF16)],
        scratch_shapes=[pltpu.VMEM((1, LANES), F32)],
        compiler_params=_cparams(("arbitrary", "arbitrary")),
        name="fox_in",
    )(x, (wq * (scale * LOG2E)).astype(BF16), wk.astype(BF16), wv.astype(BF16), wg.astype(BF16), wf_hi,
      wf_lo, (wm * scale).astype(BF16), bf_pad, place)


def _fox_attn_kernel(q_ref, k_ref, kc_ref, v_ref, g_ref, o_ref, sa_ref, sb_ref, *, tq):
    qi = pl.program_id(2)
    lane = _iota((tq, LANES), 1)
    q_pair = q_ref[0]
    q_rows = []
    for h in range(2):
        own = jnp.where(_div_pow2(lane, HEAD_DIM) == h, q_pair, jnp.zeros_like(q_pair))
        minus = jnp.where(jnp.logical_and(lane >= C_PARTS * h, lane < C_PARTS * (h + 1)), -1.0, 0.0)
        q_rows.append(jnp.concatenate([own, minus.astype(BF16)], axis=1))
    q_both = jnp.concatenate(q_rows, axis=0)

    def scores(kj, s_ref):
        start = pl.multiple_of(kj * tq, tq)
        k_aug = jnp.concatenate([k_ref[0, pl.ds(start, tq), :], kc_ref[0, pl.ds(start, tq), :]], axis=1)
        s_ref[...] = _dot_nt(q_both, k_aug)

    def absorb(kj, s_ref, carry, masked):
        start = pl.multiple_of(kj * tq, tq)
        v = v_ref[0, pl.ds(start, tq), :]
        out = []
        for h in range(2):
            m, l, acc = carry[h]
            m_rows, p_rows, sum_rows = [], [], []
            for r0 in range(0, tq, ROW_STRIP):
                s = s_ref[h * tq + r0:h * tq + r0 + ROW_STRIP, :]
                if masked:
                    s = jnp.where(_iota(s.shape, 1) <= _iota(s.shape, 0) + r0, s, -jnp.inf)
                m_new = jnp.maximum(m[r0:r0 + ROW_STRIP], jnp.max(s, axis=1, keepdims=True))
                p = jnp.exp2(s - m_new)
                sum_rows.append(functools.reduce(
                    jnp.add, [p[:, i * LANES:(i + 1) * LANES] for i in range(tq // LANES)]))
                p_rows.append(p.astype(BF16))
                m_rows.append(m_new)
            m_new = jnp.concatenate(m_rows, axis=0)
            alpha = jnp.exp2(m - m_new)
            l = alpha * l + jnp.concatenate(sum_rows, axis=0)
            acc = alpha * acc + _dot(jnp.concatenate(p_rows, axis=0), v)
            out.append((m_new, l, acc))
        return tuple(out)

    def pair(i, carry):
        scores(2 * i + 1, sb_ref)
        carry = absorb(2 * i, sa_ref, carry, masked=False)
        scores(2 * i + 2, sa_ref)
        return absorb(2 * i + 1, sb_ref, carry, masked=False)

    def odd_tail(carry):
        scores(qi, sb_ref)
        carry = absorb(qi - 1, sa_ref, carry, masked=False)
        return absorb(qi, sb_ref, carry, masked=True)

    def even_tail(carry):
        return absorb(qi, sa_ref, carry, masked=True)

    init = (jnp.full((tq, 1), -jnp.inf, F32), jnp.zeros((tq, LANES), F32), jnp.zeros((tq, LANES), F32))
    scores(0, sa_ref)
    carry = lax.fori_loop(0, jnp.right_shift(qi, 1), pair, (init, init))
    (_, l0, acc0), (_, l1, acc1) = lax.cond(jnp.bitwise_and(qi, 1) == 1, odd_tail, even_tail, carry)
    o0 = acc0 / jnp.sum(l0, axis=1, keepdims=True)
    o1 = acc1 / jnp.sum(l1, axis=1, keepdims=True)
    o = jnp.where(lane < HEAD_DIM, o0, o1)
    o_ref[0] = (o * g_ref[0].astype(F32)).astype(BF16)


def _fox_attn(q, k, kc, v, g, tq):
    B, S, _ = v.shape
    tile = pl.BlockSpec((1, tq, LANES), lambda b, p, i: (b, i, p))
    whole = pl.BlockSpec((1, S, LANES), lambda b, p, i: (b, 0, p))
    return pl.pallas_call(
        functools.partial(_fox_attn_kernel, tq=tq),
        grid=(B, N_HEAD_PAIRS, S // tq),
        in_specs=[tile, whole, whole, whole, tile],
        out_specs=tile,
        out_shape=jax.ShapeDtypeStruct((B, S, TOK_W), BF16),
        scratch_shapes=[pltpu.VMEM((2 * tq, tq), F32), pltpu.VMEM((2 * tq, tq), F32)],
        compiler_params=_cparams(("arbitrary", "arbitrary", "arbitrary")),
        name="fox_attn",
    )(q, k, kc, v, g)


def _rwkv_in_kernel(x_ref, w_ref, mu_ref, w0_ref, w2_ref, a0_ref, a2_ref, g2_ref, kk_ref, ka_ref,
                    rk_ref, head_ones_ref,
                    rt_ref, at_ref, bt_ref, kt_ref, bh_ref, kh_ref, v_ref, pc_ref, bonus_ref, g_ref,
                    qm_ref, prev_ref):
    @pl.when(pl.program_id(1) == 0)
    def _():
        prev_ref[...] = jnp.zeros_like(prev_ref)

    tm = x_ref.shape[1]
    h = _dot(x_ref[0].astype(BF16), w_ref[...])
    qm_ref[0] = (h[:, RWKV_SHIFT_W:] * (HEAD_DIM ** -0.5)).astype(BF16)
    hs = h[:, :RWKV_SHIFT_W]
    row = _iota(hs.shape, 0)
    shifted = jnp.where(row == 0, prev_ref[...], pltpu.roll(hs, 1, 0))
    prev_ref[...] = hs[tm - 1:tm, :]
    hs = hs + mu_ref[...] * (shifted - hs)
    r = hs[:, :TOK_W]
    k = hs[:, TOK_W:2 * TOK_W]
    v = hs[:, 2 * TOK_W:3 * TOK_W]
    wa = hs[:, 3 * TOK_W:3 * TOK_W + LANES]
    gd = hs[:, 3 * TOK_W + LANES:]
    w = w0_ref[...] + _dot(jnp.tanh(wa).astype(BF16), w2_ref[...])
    w = -(jnp.maximum(-w, 0.0) + jnp.log(1.0 + jnp.exp(-jnp.abs(w)))) - 0.5
    log_decay = -jnp.exp(w)
    a = _sigmoid(a0_ref[...] + _dot(wa.astype(BF16), a2_ref[...]))
    g_ref[0] = _dot(_sigmoid(gd).astype(BF16), g2_ref[...])
    kk = k * kk_ref[...]
    kk = kk / jnp.maximum(jnp.sqrt(_head_sums(kk * kk, head_ones_ref[...])), 1e-12)
    k = k * (1.0 + (a - 1.0) * ka_ref[...])
    bonus_ref[0] = r * k * rk_ref[...]
    v_ref[0] = v
    a_s = -kk
    b_s = kk * a
    same_chunk = _div_pow2(_iota((tm, tm), 0), CHUNK) == _div_pow2(_iota((tm, tm), 1), CHUNK)
    incl = (same_chunk & (_iota((tm, tm), 1) <= _iota((tm, tm), 0))).astype(BF16)
    d_hi, d_mid, d_lo = _split3(log_decay)
    cw = _dot(incl, d_hi) + _dot(incl, d_mid) + _dot(incl, d_lo)
    same = same_chunk.astype(BF16)
    cw_end = _dot(same, d_hi) + _dot(same, d_mid) + _dot(same, d_lo)
    e_in = jnp.exp(cw)
    e_out = jnp.exp(-cw)
    e_end = jnp.exp(cw_end - cw)
    rt_ref[0] = (r * e_in).astype(BF16)
    at_ref[0] = (a_s * jnp.exp(cw - log_decay)).astype(BF16)
    bt_ref[0] = (b_s * e_out).astype(BF16)
    kt_ref[0] = (k * e_out).astype(BF16)
    bh_ref[0] = (b_s * e_end).astype(BF16)
    kh_ref[0] = (k * e_end).astype(BF16)
    pc_ref[0] = jnp.exp(cw_end)


def _rwkv_in(x, w_in, mu, w0, w2, a0, a2, g2, k_k, k_a, r_k, tm):
    B, S, _ = x.shape
    in_w = w_in.shape[1]
    w2_pad = jnp.concatenate([w2, jnp.zeros((AAA_LORA, TOK_W), F32)], axis=0).astype(BF16)
    a2_pad = jnp.concatenate([jnp.zeros((DECAY_LORA, TOK_W), F32), a2], axis=0).astype(BF16)
    head_ones = _head_blocks(1.0)
    vec = lambda t: t.reshape(1, -1)
    row_spec = lambda w: pl.BlockSpec((1, tm, w), lambda b, i: (b, i, 0))
    tok = jax.ShapeDtypeStruct((B, S, TOK_W), F32)
    return pl.pallas_call(
        _rwkv_in_kernel,
        grid=(B, S // tm),
        in_specs=[row_spec(D_MODEL), _full_spec((D_MODEL, in_w)), _full_spec((1, RWKV_SHIFT_W)),
                  _full_spec((1, TOK_W)), _full_spec((LANES, TOK_W)), _full_spec((1, TOK_W)),
                  _full_spec((LANES, TOK_W)), _full_spec((GATE_LORA, TOK_W)), _full_spec((1, TOK_W)),
                  _full_spec((1, TOK_W)), _full_spec((1, TOK_W)), _full_spec((MXU_W, MXU_W))],
        out_specs=[row_spec(TOK_W)] * 10 + [row_spec(MEM_W)],
        out_shape=[jax.ShapeDtypeStruct((B, S, TOK_W), BF16)] * 6 + [tok] * 4
        + [jax.ShapeDtypeStruct((B, S, MEM_W), BF16)],
        scratch_shapes=[pltpu.VMEM((1, RWKV_SHIFT_W), F32)],
        compiler_params=_cparams(("arbitrary", "arbitrary")),
        name="rwkv_in",
    )(x, w_in.astype(BF16), vec(mu), vec(w0), w2_pad, vec(a0), a2_pad, g2.astype(BF16), vec(k_k),
      vec(k_a), vec(r_k), head_ones)


def _rwkv_scan_kernel(rt_ref, at_ref, bt_ref, kt_ref, bh_ref, kh_ref, v_ref, pc_ref, y_ref, state_ref,
                      *, n_chunks):
    @pl.when(pl.program_id(2) == 0)
    def _():
        state_ref[...] = jnp.zeros_like(state_ref)

    C = CHUNK
    lane_head = _div_pow2(_iota((2 * C, LANES), 1), HEAD_DIM)
    row_head = _div_pow2(_iota((2 * C, LANES), 0), C)
    keep = lane_head == row_head
    pos_r = _mod_pow2(_iota((2 * C, 2 * C), 0), C)
    pos_c = _mod_pow2(_iota((2 * C, 2 * C), 1), C)
    strict = pos_c < pos_r
    incl = pos_c <= pos_r
    eye = (_iota((2 * C, 2 * C), 0) == _iota((2 * C, 2 * C), 1)).astype(F32)

    def stack(ref, c):
        t = ref[0, c * C:(c + 1) * C, :]
        return jnp.where(keep, jnp.concatenate([t, t], axis=0), 0.0).astype(BF16)

    chunks = range(n_chunks)
    a2 = [stack(at_ref, c) for c in chunks]
    r2 = [stack(rt_ref, c) for c in chunks]
    bh2 = [stack(bh_ref, c) for c in chunks]
    kh2 = [stack(kh_ref, c) for c in chunks]
    v2 = [stack(v_ref, c) for c in chunks]
    gram = [_dot_nt(jnp.concatenate([a2[c], r2[c]], axis=0),
                    jnp.concatenate([stack(bt_ref, c), stack(kt_ref, c)], axis=0)) for c in chunks]
    power = [jnp.where(strict, gram[c][:2 * C, :2 * C], 0.0) for c in chunks]
    inv = [eye + power[c] for c in chunks]
    for _ in range(int(math.log2(C)) - 1):
        power = [_dot(p.astype(BF16), p.astype(BF16)) for p in power]
        inv = [inv[c] + _dot(inv[c].astype(BF16), power[c].astype(BF16)) for c in chunks]
    from_v = [_dot(jnp.concatenate([jnp.where(strict, gram[c][:2 * C, 2 * C:], 0.0),
                                    jnp.where(incl, gram[c][2 * C:, 2 * C:], 0.0)], axis=0).astype(BF16),
                   v2[c]) for c in chunks]
    wu = [_dot(inv[c].astype(BF16),
               jnp.concatenate([a2[c], from_v[c][:2 * C].astype(BF16)], axis=1)) for c in chunks]
    wy = [_dot(jnp.where(incl, gram[c][2 * C:, :2 * C], 0.0).astype(BF16), wu[c].astype(BF16))
          for c in chunks]
    w_y = [(r2[c].astype(F32) + wy[c][:, :LANES]).astype(BF16) for c in chunks]
    y0 = [from_v[c][2 * C:] + wy[c][:, LANES:] for c in chunks]
    gc = [_dot(wu[c].T.astype(BF16), bh2[c]) for c in chunks]
    c0 = [gc[c][LANES:] + _dot(v2[c].astype(F32).T.astype(BF16), kh2[c]) for c in chunks]
    state = state_ref[...]
    for c in chunks:
        sb = state.astype(BF16)
        y2 = _dot_nt(w_y[c], sb) + y0[c]
        y_ref[0, c * C:(c + 1) * C, :] = y2[:C] + y2[C:]
        state = state * pc_ref[0, c * C:c * C + 1, :] + _dot(sb, gc[c][:LANES].astype(BF16)) + c0[c]
    state_ref[...] = state


def _rwkv_scan(rt, at, bt, kt, bh, kh, v, pc, rows):
    B, S, _ = v.shape
    spec = pl.BlockSpec((1, rows, LANES), lambda b, p, i: (b, i, p))
    return pl.pallas_call(
        functools.partial(_rwkv_scan_kernel, n_chunks=rows // CHUNK),
        grid=(B, N_HEAD_PAIRS, S // rows),
        in_specs=[spec] * 8,
        out_specs=spec,
        out_shape=jax.ShapeDtypeStruct((B, S, TOK_W), F32),
        scratch_shapes=[pltpu.VMEM((LANES, LANES), F32)],
        compiler_params=_cparams(("arbitrary", "arbitrary", "arbitrary")),
        name="rwkv_scan",
    )(rt, at, bt, kt, bh, kh, v, pc)


def _rwkv_post_kernel(y_ref, bonus_ref, v_ref, g_ref, lg_ref, lb_ref, head_mean_ref, o_ref):
    y = y_ref[0]
    head_mean = functools.partial(_head_sums, blocks=head_mean_ref[...])

    yc = y - head_mean(y)
    var = head_mean(yc * yc)
    yn = yc * lax.rsqrt(var + GN_EPS) * lg_ref[...] + lb_ref[...]
    bonus = head_mean(bonus_ref[0]) * float(HEAD_DIM)
    o_ref[0] = ((yn + bonus * v_ref[0]) * g_ref[0]).astype(BF16)


def _rwkv_post(y, bonus, v, g, lnx_g, lnx_b, tm):
    B, S, _ = y.shape
    head_mean = _head_blocks(1.0 / HEAD_DIM)
    row_spec = pl.BlockSpec((1, tm, TOK_W), lambda b, i: (b, i, 0))
    return pl.pallas_call(
        _rwkv_post_kernel,
        grid=(B, S // tm),
        in_specs=[row_spec] * 4 + [_full_spec((1, TOK_W)), _full_spec((1, TOK_W)),
                                   _full_spec((MXU_W, MXU_W))],
        out_specs=row_spec,
        out_shape=jax.ShapeDtypeStruct((B, S, TOK_W), BF16),
        compiler_params=_cparams(("arbitrary", "arbitrary")),
        name="rwkv_post",
    )(y, bonus, v, g, lnx_g.reshape(1, -1), lnx_b.reshape(1, -1), head_mean)


def _mix_out_kernel(tok_ref, qm_ref, km_ref, vm_ref, x_ref, wo_tok_ref, wo_mem_ref, g_ref, b_ref,
                    wr_hi_ref, wr_lo_ref, rb_ref, x1_ref, x1b_ref, gate_ref):
    tm = x_ref.shape[1]
    groups = [slice(r, r + tm // MIX_GROUPS) for r in range(0, tm, tm // MIX_GROUPS)]
    mem_out = [None] * MIX_GROUPS
    for h in range(N_MEM_HEADS):
        for i, rows in enumerate(groups):
            s = _dot_nt(qm_ref[0, rows, :], km_ref[0, h])
            e = jnp.exp(s - jnp.max(s, axis=1, keepdims=True))
            o = _dot(e.astype(BF16), vm_ref[0, h]) / jnp.sum(e, axis=1, keepdims=True)
            mem_out[i] = o if mem_out[i] is None else mem_out[i] + o
    x1 = []
    for i, rows in enumerate(groups):
        mixed = _dot(tok_ref[0, rows, :], wo_tok_ref[...]) + _dot(mem_out[i].astype(BF16), wo_mem_ref[...])
        x1.append(_layer_norm(DEEPNORM_ALPHA * x_ref[0, rows, :] + mixed, g_ref[...], b_ref[...]))
        x1_ref[0, rows, :] = x1[i]
        x1b_ref[0, rows, :] = x1[i].astype(BF16)
    scores = [_sigmoid(_dot3(x1[i], wr_hi_ref[...], wr_lo_ref[...])) for i in range(MIX_GROUPS)]
    lane = _iota(scores[0].shape, 1)
    lane_f = lane.astype(F32)
    cand = [jnp.where(lane < N_EXPERTS, sc + rb_ref[...], -jnp.inf) for sc in scores]
    picked = [jnp.zeros(scores[0].shape, jnp.bool_)] * MIX_GROUPS
    for _ in range(TOP_K):
        for i in range(MIX_GROUPS):
            best = jnp.max(cand[i], axis=1, keepdims=True)
            first = jnp.min(jnp.where(cand[i] == best, lane_f, float(LANES)), axis=1, keepdims=True)
            hit = lane_f == first
            picked[i] = jnp.logical_or(picked[i], hit)
            cand[i] = jnp.where(hit, -jnp.inf, cand[i])
    for i, rows in enumerate(groups):
        chosen = jnp.where(picked[i], scores[i], 0.0)
        gate = chosen / jnp.sum(chosen, axis=1, keepdims=True) * ROUTED_SCALE
        gate_ref[0, rows, :] = jnp.where(lane == N_EXPERTS, 1.0, gate)


def _mix_out(tok, qm, km, vm, x, w_out, ln_g, ln_b, w_router, router_bias, tm):
    B, S, _ = x.shape
    n_mem = km.shape[2]
    wr = jnp.pad(w_router, ((0, 0), (0, LANES - N_EXPERTS)))
    wr_hi = wr.astype(BF16)
    wr_lo = (wr - wr_hi.astype(F32)).astype(BF16)
    rb = jnp.pad(router_bias, (0, LANES - N_EXPERTS)).reshape(1, LANES)
    row_spec = lambda w: pl.BlockSpec((1, tm, w), lambda b, i: (b, i, 0))
    mem_spec = pl.BlockSpec((1, N_MEM_HEADS, n_mem, MEM_W), lambda b, i: (b, 0, 0, 0))
    return pl.pallas_call(
        _mix_out_kernel,
        grid=(B, S // tm),
        in_specs=[row_spec(TOK_W), row_spec(MEM_W), mem_spec, mem_spec, row_spec(D_MODEL),
                  _full_spec((TOK_W, D_MODEL)), _full_spec((MEM_W, D_MODEL)),
                  _full_spec((1, D_MODEL)), _full_spec((1, D_MODEL)),
                  _full_spec((D_MODEL, LANES)), _full_spec((D_MODEL, LANES)), _full_spec((1, LANES))],
        out_specs=[row_spec(D_MODEL), row_spec(D_MODEL), row_spec(LANES)],
        out_shape=[jax.ShapeDtypeStruct((B, S, D_MODEL), F32), jax.ShapeDtypeStruct((B, S, D_MODEL), BF16),
                   jax.ShapeDtypeStruct((B, S, LANES), F32)],
        compiler_params=_cparams(("arbitrary", "arbitrary")),
        name="mix_out",
    )(tok, qm, km, vm, x, w_out[:TOK_W].astype(BF16), w_out[TOK_W:].astype(BF16),
      ln_g.reshape(1, -1), ln_b.reshape(1, -1), wr_hi, wr_lo, rb)


MOE_TM = 512
UNIT = 16
SLAB_ROWS = 4096
SLAB_UNITS = SLAB_ROWS // UNIT
BLK_UNITS = 64
BLK_ROWS = BLK_UNITS * UNIT
XS_W = D_MODEL + 2 * LANES
ROW_CHUNK = 1024
BF16_EXACT_INT = 256
assert TOP_K * MOE_TM + N_EXPERTS * (UNIT - 1) <= SLAB_ROWS - UNIT
ZERO_UNIT = SLAB_UNITS - 1


def _dot_x_sel(x, sel):
    hi, mid, lo = _split3(x)
    return _dot(hi, sel) + _dot(mid, sel) + _dot(lo, sel)


def _route_tile(gates):
    tm = gates.shape[0]
    sel = jnp.logical_and(gates > 0.0, _iota(gates.shape, 1) < N_EXPERTS)
    sel_b = jnp.where(sel, 1.0, 0.0).astype(BF16)
    earlier = (_iota((tm, tm), 1) < _iota((tm, tm), 0)).astype(BF16)
    rank1 = jnp.where(sel, _dot(earlier, sel_b) + 1.0, 0.0)
    count = _dot(jnp.ones((8, tm), BF16), sel_b)[0:1]
    n16 = jnp.floor((count + float(UNIT - 1)) * (1.0 / UNIT))
    before = (_iota((LANES, LANES), 0) < _iota((LANES, LANES), 1)).astype(BF16)
    off16 = _dot(jnp.broadcast_to(n16, (8, LANES)).astype(BF16), before)[0:1]
    r_hi, r_lo = _split2(rank1)
    return r_hi, r_lo, jnp.max(count) > float(BF16_EXACT_INT), n16, off16


def _add_if(pred, base, extra):
    return lax.cond(pred, lambda t: t + extra(), lambda t: t, base)


def _slab_experts(n16, off16, r0, rows):
    r = (_iota((rows, LANES), 0) + r0).astype(F32)
    lo = off16 * float(UNIT)
    return jnp.logical_and(r >= lo, r < lo + n16 * float(UNIT))


def _moe_dispatch_kernel(xb_ref, gate_ref, xs_ref, n_ref):
    gates = gate_ref[...]
    r_hi, r_lo, has_lo, n16, off16 = _route_tile(gates)
    n_ref[0] = jnp.broadcast_to(n16, (8, LANES))
    g_hi, g_lo = _split2(gates)
    src = jnp.concatenate([xb_ref[...], g_hi, g_lo], axis=1)
    for c in range(SLAB_ROWS // ROW_CHUNK):
        r0 = c * ROW_CHUNK
        seg = _slab_experts(n16, off16, r0, ROW_CHUNK)
        seg_b = jnp.where(seg, 1.0, 0.0).astype(BF16)
        rank_at = _add_if(has_lo, _dot_nt(seg_b, r_hi), lambda: _dot_nt(seg_b, r_lo))
        r = (_iota((ROW_CHUNK, 1), 0) + r0).astype(F32)
        pos1 = r + 1.0 - jnp.sum(jnp.where(seg, off16 * float(UNIT), 0.0), axis=1, keepdims=True)
        pick = jnp.where(rank_at == pos1, 1.0, 0.0).astype(BF16)
        xs_ref[r0:r0 + ROW_CHUNK, :] = _dot(pick, src).astype(BF16)


def _moe_dispatch(xb, gates):
    T = xb.shape[0]
    nt = T // MOE_TM
    return pl.pallas_call(
        _moe_dispatch_kernel,
        grid=(nt,),
        in_specs=[pl.BlockSpec((MOE_TM, D_MODEL), lambda i: (i, 0)),
                  pl.BlockSpec((MOE_TM, LANES), lambda i: (i, 0))],
        out_specs=[pl.BlockSpec((SLAB_ROWS, XS_W), lambda i: (i, 0)),
                   pl.BlockSpec((1, 8, LANES), lambda i: (i, 0, 0))],
        out_shape=[jax.ShapeDtypeStruct((nt * SLAB_ROWS, XS_W), BF16),
                   jax.ShapeDtypeStruct((nt, 8, LANES), F32)],
        compiler_params=_cparams(("arbitrary",)),
        name="moe_dispatch",
    )(xb, gates)


def _moe_plan_kernel(n_ref, unit_ref, expert_ref, count_ref, *, nt, nb):
    n16 = n_ref[...]
    lane = _iota((nt, LANES), 1)
    before = (_iota((LANES, LANES), 0) < _iota((LANES, LANES), 1)).astype(BF16)
    off16 = _dot(n16.astype(BF16), before)
    used = jnp.sum(n16, axis=1, keepdims=True)
    n16 = n16 + jnp.where(lane == N_EXPERTS - 1, float(SLAB_UNITS) - used, 0.0)
    earlier = (_iota((nt, nt), 1) < _iota((nt, nt), 0)).astype(BF16)
    cum_ex = _dot_sel(earlier, n16)
    cum_in = cum_ex + n16
    total = cum_in[nt - 1:nt, :]
    n_blk = jnp.floor((total + float(BLK_UNITS - 1)) * (1.0 / BLK_UNITS))
    blk_start = _dot_x_sel(jnp.broadcast_to(n_blk, (8, LANES)), before)[0:1]
    blk_end = blk_start + n_blk
    count_ref[...] = jnp.broadcast_to(jnp.sum(n_blk, axis=1, keepdims=True), (8, LANES)).astype(jnp.int32)

    lane_b = _iota((nb, LANES), 1)
    blk = _iota((nb, LANES), 0).astype(F32)
    expert = jnp.sum(jnp.where(jnp.logical_and(lane_b < N_EXPERTS, blk_end <= blk), 1.0, 0.0),
                     axis=1, keepdims=True)
    live = expert < float(N_EXPERTS)
    expert = jnp.minimum(expert, float(N_EXPERTS - 1))
    expert_ref[...] = jnp.broadcast_to(expert, (nb, LANES)).astype(jnp.int32)
    mine = lane_b.astype(F32) == expert
    mine_b = jnp.where(mine, 1.0, 0.0).astype(BF16)
    my_start = jnp.sum(jnp.where(mine, blk_start, 0.0), axis=1, keepdims=True)
    my_total = jnp.sum(jnp.where(mine, total, 0.0), axis=1, keepdims=True)
    q = (blk - my_start) * float(BLK_UNITS) + lane_b.astype(F32)
    valid = jnp.logical_and(jnp.logical_and(q < my_total, lane_b < BLK_UNITS), live)

    def per_slab(table):
        hi, mid, lo = _split3(table)
        return _dot_nt(mine_b, hi) + _dot_nt(mine_b, mid) + _dot_nt(mine_b, lo)

    ex, inc, first = per_slab(cum_ex), per_slab(cum_in), per_slab(off16)
    base = jnp.zeros((nb, LANES), F32)
    for i in range(nt):
        hit = jnp.logical_and(q >= ex[:, i:i + 1], q < inc[:, i:i + 1])
        base = base + jnp.where(hit, float(i * SLAB_UNITS) + first[:, i:i + 1] - ex[:, i:i + 1], 0.0)
    unit_ref[...] = jnp.where(valid, base + q, -1.0).astype(jnp.int32)


def _moe_plan(n16, nb):
    nt = n16.shape[0]
    tbl = jax.ShapeDtypeStruct((nb, LANES), jnp.int32)
    units, experts, count = pl.pallas_call(
        functools.partial(_moe_plan_kernel, nt=nt, nb=nb),
        out_shape=[tbl, tbl, jax.ShapeDtypeStruct((8, LANES), jnp.int32)],
        compiler_params=pltpu.CompilerParams(vmem_limit_bytes=VMEM_LIMIT),
        name="moe_plan",
    )(n16)
    return units[:, :BLK_UNITS].reshape(-1), experts[:, 0], count[0, :1]


def _moe_ffn_kernel(unit_ref, expert_ref, count_ref, xs_hbm, wg_ref, wu_ref, wd_ref, ys_hbm,
                    ibuf, obuf, in_sem, out_sem, *, nb):
    b = pl.program_id(0)
    n_live = count_ref[0]
    slot = jnp.bitwise_and(b, 1)

    def in_copy(blk, buf, s):
        u = unit_ref[blk * BLK_UNITS + s]
        u = jnp.where(u < 0, ZERO_UNIT, u)
        return pltpu.make_async_copy(xs_hbm.at[pl.ds(pl.multiple_of(u * UNIT, UNIT), UNIT), :],
                                     ibuf.at[buf, pl.ds(s * UNIT, UNIT), :], in_sem.at[buf])

    def out_copy(blk, buf, s):
        u = unit_ref[blk * BLK_UNITS + s]
        return u, pltpu.make_async_copy(
            obuf.at[buf, pl.ds(s * UNIT, UNIT), :],
            ys_hbm.at[pl.ds(pl.multiple_of(jnp.maximum(u, 0) * UNIT, UNIT), UNIT), :], out_sem.at[buf])

    def whole_block(blk):
        return unit_ref[blk * BLK_UNITS + BLK_UNITS - 1] >= 0

    def for_each_used_out(blk, buf, act):
        for s in range(BLK_UNITS):
            u, copy = out_copy(blk, buf, s)
            pl.when(u >= 0)(functools.partial(act, copy))

    def start_out(blk, buf):
        @pl.when(whole_block(blk))
        def _():
            for s in range(BLK_UNITS):
                out_copy(blk, buf, s)[1].start()

        @pl.when(jnp.logical_not(whole_block(blk)))
        def _():
            for_each_used_out(blk, buf, lambda c: c.start())

    def wait_out(blk, buf):
        @pl.when(whole_block(blk))
        def _():
            pltpu.make_async_copy(obuf.at[buf], ys_hbm.at[pl.ds(0, BLK_ROWS), :], out_sem.at[buf]).wait()

        @pl.when(jnp.logical_not(whole_block(blk)))
        def _():
            for_each_used_out(blk, buf, lambda c: c.wait())

    def start_in(blk, buf):
        for s in range(BLK_UNITS):
            in_copy(blk, buf, s).start()

    @pl.when(b == 0)
    def _():
        start_in(0, 0)

    @pl.when(b <= n_live)
    def _():
        pltpu.make_async_copy(xs_hbm.at[pl.ds(0, BLK_ROWS), :], ibuf.at[slot], in_sem.at[slot]).wait()

    @pl.when(jnp.logical_and(b >= 2, b - 2 < n_live))
    def _():
        wait_out(b - 2, slot)

    @pl.when(b < n_live)
    def _():
        start_in(b + 1, 1 - slot)
        rows = ibuf[slot]
        x = rows[:, :D_MODEL]
        gates = rows[:, D_MODEL:D_MODEL + LANES].astype(F32) + rows[:, D_MODEL + LANES:].astype(F32)
        e = expert_ref[b]
        gate = jnp.sum(jnp.where(_iota(gates.shape, 1) == e, gates, 0.0), axis=1, keepdims=True)
        hg = _dot(x, wg_ref[0, 0].astype(BF16))
        hu = _dot(x, wu_ref[0, 0].astype(BF16))
        hidden = hg * _sigmoid(hg) * hu * gate
        obuf[slot] = _dot(hidden.astype(BF16), wd_ref[0, 0].astype(BF16)).astype(BF16)
        start_out(b, slot)

    @pl.when(jnp.logical_and(b == nb - 1, nb - 2 < n_live))
    def _():
        wait_out(nb - 2, 1 - slot)


def _moe_ffn(xs, units, experts, count, w_gate, w_up, w_down, layer, nb):
    n_rows = xs.shape[0]
    w_spec = lambda r, c: pl.BlockSpec((1, 1, r, c), lambda b, u, e, n: (layer, e[b], 0, 0))
    grid_spec = pltpu.PrefetchScalarGridSpec(
        num_scalar_prefetch=3,
        grid=(nb,),
        in_specs=[pl.BlockSpec(memory_space=pl.ANY),
                  w_spec(D_MODEL, D_EXPERT), w_spec(D_MODEL, D_EXPERT), w_spec(D_EXPERT, D_MODEL)],
        out_specs=pl.BlockSpec(memory_space=pl.ANY),
        scratch_shapes=[pltpu.VMEM((2, BLK_ROWS, XS_W), BF16), pltpu.VMEM((2, BLK_ROWS, D_MODEL), BF16),
                        pltpu.SemaphoreType.DMA((2,)), pltpu.SemaphoreType.DMA((2,))],
    )
    return pl.pallas_call(
        functools.partial(_moe_ffn_kernel, nb=nb),
        grid_spec=grid_spec,
        out_shape=jax.ShapeDtypeStruct((n_rows, D_MODEL), BF16),
        compiler_params=_cparams(("arbitrary",)),
        name="moe_ffn",
    )(units, experts, count, xs, w_gate, w_up, w_down)


def _moe_combine_kernel(ys_ref, gate_ref, x_ref, xb_ref, wsg_ref, wsu_ref, wsd_ref, g_ref, b_ref, o_ref):
    r_hi, r_lo, has_lo, n16, off16 = _route_tile(gate_ref[...])
    off_b = jnp.broadcast_to(off16, (8, LANES)).astype(BF16)
    xb = xb_ref[...]
    hg = _dot(xb, wsg_ref[...])
    acc = _dot((hg * _sigmoid(hg) * _dot(xb, wsu_ref[...])).astype(BF16), wsd_ref[...])
    for c in range(SLAB_ROWS // ROW_CHUNK):
        r0 = c * ROW_CHUNK
        seg_b = jnp.where(_slab_experts(n16, off16, r0, ROW_CHUNK), 1.0, 0.0).astype(BF16)
        rank_at = _add_if(has_lo, _dot_nt(r_hi, seg_b), lambda: _dot_nt(r_lo, seg_b))
        r = (_iota((1, ROW_CHUNK), 1) + r0).astype(F32)
        pos1 = r + 1.0 - _dot_nt(off_b, seg_b)[0:1] * float(UNIT)
        pick = jnp.where(rank_at == pos1, 1.0, 0.0).astype(BF16)
        acc = acc + _dot(pick, ys_ref[r0:r0 + ROW_CHUNK, :])
    o_ref[...] = _layer_norm(DEEPNORM_ALPHA * x_ref[...] + acc, g_ref[...], b_ref[...])


def _moe_combine(ys, gates, x, xb, w_sh_gate, w_sh_up, w_sh_down, ln_g, ln_b):
    T = x.shape[0]
    row_spec = lambda w: pl.BlockSpec((MOE_TM, w), lambda i: (i, 0))
    return pl.pallas_call(
        _moe_combine_kernel,
        grid=(T // MOE_TM,),
        in_specs=[pl.BlockSpec((SLAB_ROWS, D_MODEL), lambda i: (i, 0)),
                  row_spec(LANES), row_spec(D_MODEL), row_spec(D_MODEL),
                  _full_spec((D_MODEL, D_EXPERT)), _full_spec((D_MODEL, D_EXPERT)),
                  _full_spec((D_EXPERT, D_MODEL)), _full_spec((1, D_MODEL)), _full_spec((1, D_MODEL))],
        out_specs=row_spec(D_MODEL),
        out_shape=jax.ShapeDtypeStruct((T, D_MODEL), F32),
        compiler_params=_cparams(("arbitrary",)),
        name="moe_combine",
    )(ys, gates, x, xb, w_sh_gate.astype(BF16), w_sh_up.astype(BF16), w_sh_down.astype(BF16),
      ln_g.reshape(1, -1), ln_b.reshape(1, -1))


def _moe(xb, x, gates, w_gate, w_up, w_down, layer, w_sh_gate, w_sh_up, w_sh_down, ln_g, ln_b):
    T = x.shape[0]
    nt = T // MOE_TM
    nb = nt * SLAB_UNITS // BLK_UNITS + N_EXPERTS + 1
    xs, n16 = _moe_dispatch(xb, gates)
    units, experts, count = _moe_plan(n16[:, 0, :], nb)
    ys = _moe_ffn(xs, units, experts, count, w_gate, w_up, w_down, layer, nb)
    return _moe_combine(ys, gates, x, xb, w_sh_gate, w_sh_up, w_sh_down, ln_g, ln_b)


def _tile(n, want):
    t = min(n, want)
    assert n % t == 0, (n, t)
    return t


def kernel(x, mem, mem_ln_g, mem_ln_b, w_mem_kv, fox_w_in, fox_b_f, rwkv_w_in, rwkv_mu, rwkv_w0, rwkv_w2,
           rwkv_a0, rwkv_a2, rwkv_g2, rwkv_k_k, rwkv_k_a, rwkv_r_k, rwkv_lnx_g, rwkv_lnx_b, w_out, ln1_g,
           ln1_b, w_router, router_bias, w_exp_gate, w_exp_up, w_exp_down, w_sh_gate, w_sh_up, w_sh_down,
           ln2_g, ln2_b):
    B, S, D = x.shape
    T = B * S
    assert D == D_MODEL and S % CHUNK == 0 and T % MOE_TM == 0
    t_proj = _tile(S, 512)
    t_attn = _tile(S, 512)
    t_rwkv = _tile(S, 256)
    t_scan = _tile(S, 512)

    km, vm = _mem_kv(mem, mem_ln_g, mem_ln_b, w_mem_kv)
    for i in range(DEPTH):
        j = i // 2
        if i % 2 == 0:
            q, k, kc, v, g, qm = _fox_in(x, fox_w_in[j], fox_b_f[j], t_proj)
            tok = _fox_attn(q, k, kc, v, g, t_attn)
        else:
            (rt, at, bt, kt, bh, kh, v, pc, bonus, g, qm) = _rwkv_in(
                x, rwkv_w_in[j], rwkv_mu[j], rwkv_w0[j], rwkv_w2[j], rwkv_a0[j], rwkv_a2[j], rwkv_g2[j],
                rwkv_k_k[j], rwkv_k_a[j], rwkv_r_k[j].reshape(-1), t_rwkv)
            y = _rwkv_scan(rt, at, bt, kt, bh, kh, v, pc, t_scan)
            tok = _rwkv_post(y, bonus, v, g, rwkv_lnx_g[j], rwkv_lnx_b[j], t_proj)
        x1, x1b, gates = _mix_out(tok, qm, km, vm, x, w_out[i], ln1_g[i], ln1_b[i], w_router[i],
                                  router_bias[i], t_proj)
        x = _moe(x1b.reshape(T, D), x1.reshape(T, D), gates.reshape(T, LANES), w_exp_gate, w_exp_up,
                 w_exp_down, i, w_sh_gate[i], w_sh_up[i], w_sh_down[i], ln2_g[i], ln2_b[i]).reshape(B, S, D)
    return x
```

```python
import functools
import math

import jax
import jax.numpy as jnp
from jax import lax
from jax.experimental import pallas as pl
from jax.experimental.pallas import tpu as pltpu

F32 = jnp.float32
BF16 = jnp.bfloat16

D_MODEL = 1024
HEAD_DIM = 64
N_TOK_HEADS = 12
TOK_W = N_TOK_HEADS * HEAD_DIM
N_HEAD_PAIRS = N_TOK_HEADS // 2
N_MEM_HEADS = 4
MEM_W = N_MEM_HEADS * HEAD_DIM
DECAY_LORA = 64
AAA_LORA = 64
GATE_LORA = 128
RWKV_SHIFT_W = 3 * TOK_W + DECAY_LORA + AAA_LORA + GATE_LORA
N_EXPERTS = 64
TOP_K = 6
D_EXPERT = 256
ROUTED_SCALE = 2.5
DEPTH = 2
DEEPNORM_ALPHA = (2 * DEPTH) ** 0.25
LN_EPS = 1e-5
GN_EPS = 64e-5
LOG2E = math.log2(math.e)

LANES = 128
C_PARTS = 3
CHUNK = 64
MIX_GROUPS = 2
VMEM_LIMIT = 56 * 1024 * 1024


def _cparams(sem):
    return pltpu.CompilerParams(dimension_semantics=sem, vmem_limit_bytes=VMEM_LIMIT)


def _dot(a, b):
    return jnp.dot(a, b, preferred_element_type=F32)


def _dot_nt(a, b):
    return lax.dot_general(a, b, (((1,), (1,)), ((), ())), preferred_element_type=F32)


def _split2(x):
    hi = x.astype(BF16)
    lo = (x - hi.astype(F32)).astype(BF16)
    return hi, lo


def _split3(x):
    hi = x.astype(BF16)
    r1 = x - hi.astype(F32)
    mid = r1.astype(BF16)
    lo = (r1 - mid.astype(F32)).astype(BF16)
    return hi, mid, lo


def _dot_sel(sel, x):
    hi, mid, lo = _split3(x)
    return _dot(sel, hi) + _dot(sel, mid) + _dot(sel, lo)


def _dot3(x, w_hi, w_lo):
    x_hi, x_lo = _split2(x)
    return _dot(x_hi, w_hi) + _dot(x_lo, w_hi) + _dot(x_hi, w_lo)


MXU_W = 256


def _head_blocks(value):
    head = jnp.arange(MXU_W) // HEAD_DIM
    return jnp.where(head[:, None] == head[None, :], value, 0.0).astype(BF16)


def _head_sums(t, blocks):
    cols = []
    for c in range(0, t.shape[1], MXU_W):
        hi, lo = _split2(t[:, c:c + MXU_W])
        cols.append(_dot(hi, blocks) + _dot(lo, blocks))
    return jnp.concatenate(cols, axis=1)


def _sigmoid(x):
    return 1.0 / (1.0 + jnp.exp(-x))


def _layer_norm(y, g, b):
    mu = jnp.mean(y, axis=-1, keepdims=True)
    yc = y - mu
    var = jnp.mean(yc * yc, axis=-1, keepdims=True)
    return yc * lax.rsqrt(var + LN_EPS) * g + b


def _iota(shape, dim):
    return lax.broadcasted_iota(jnp.int32, shape, dim)


def _div_pow2(x, n):
    return jnp.right_shift(x, int(math.log2(n)))


def _mod_pow2(x, n):
    return jnp.bitwise_and(x, n - 1)


def _full_spec(shape):
    n = len(shape)
    return pl.BlockSpec(shape, lambda *_: (0,) * n)


def _mem_kv_kernel(mem_ref, g_ref, b_ref, w_ref, k_ref, v_ref):
    m = _layer_norm(mem_ref[0], g_ref[...], b_ref[...])
    kv = _dot(m.astype(BF16), w_ref[...])
    k = kv[:, :MEM_W]
    v = kv[:, MEM_W:]
    head = _div_pow2(_iota(k.shape, 1), HEAD_DIM)
    for h in range(N_MEM_HEADS):
        k_ref[0, h] = jnp.where(head == h, k, 0.0).astype(BF16)
        v_ref[0, h] = jnp.where(head == h, v, 0.0).astype(BF16)


def _mem_kv(mem, g, b, w):
    B, n_mem, _ = mem.shape
    out = jax.ShapeDtypeStruct((B, N_MEM_HEADS, n_mem, MEM_W), BF16)
    out_spec = pl.BlockSpec((1, N_MEM_HEADS, n_mem, MEM_W), lambda i: (i, 0, 0, 0))
    return pl.pallas_call(
        _mem_kv_kernel,
        grid=(B,),
        in_specs=[pl.BlockSpec((1, n_mem, D_MODEL), lambda i: (i, 0, 0)),
                  _full_spec((1, D_MODEL)), _full_spec((1, D_MODEL)),
                  _full_spec((D_MODEL, 2 * MEM_W))],
        out_specs=[out_spec, out_spec],
        out_shape=[out, out],
        compiler_params=_cparams(("arbitrary",)),
        name="mem_kv",
    )(mem, g.reshape(1, -1), b.reshape(1, -1), w.astype(BF16))


def _fox_in_kernel(x_ref, wq_ref, wk_ref, wv_ref, wg_ref, wfh_ref, wfl_ref, wm_ref, bf_ref,
                   place_ref, q_ref, k_ref, kc_ref, v_ref, g_ref, qm_ref, carry_ref):
    @pl.when(pl.program_id(1) == 0)
    def _():
        carry_ref[...] = jnp.zeros_like(carry_ref)

    x = x_ref[0]
    xb = x.astype(BF16)
    tm = x.shape[0]
    z = _dot3(x, wfh_ref[...], wfl_ref[...]) + bf_ref[...]
    log_f = jnp.minimum(z, 0.0) - jnp.log(1.0 + jnp.exp(-jnp.abs(z)))
    tril = (_iota((tm, tm), 1) <= _iota((tm, tm), 0)).astype(BF16)
    c = _dot_sel(tril, log_f) + carry_ref[...]
    carry_ref[...] = c[tm - 1:tm, :]
    c_hi, c_mid, c_lo = _split3(c * LOG2E)
    c_parts = jnp.concatenate([c_hi, c_mid, c_lo], axis=1)
    q_ref[0] = _dot(xb, wq_ref[...]).astype(BF16)
    k_ref[0] = _dot(xb, wk_ref[...]).astype(BF16)
    kc_ref[0] = _dot(c_parts, place_ref[...]).astype(BF16)
    v_ref[0] = _dot(xb, wv_ref[...]).astype(BF16)
    g_ref[0] = _sigmoid(_dot(xb, wg_ref[...])).astype(BF16)
    qm_ref[0] = _dot(xb, wm_ref[...]).astype(BF16)


def _fox_in(x, w_in, b_f, tm):
    B, S, _ = x.shape
    scale = HEAD_DIM ** -0.5
    wq, wk, wv, wg, wf, wm = jnp.split(
        w_in, [TOK_W, 2 * TOK_W, 3 * TOK_W, 4 * TOK_W, 4 * TOK_W + N_TOK_HEADS], axis=1)
    wf_pad = jnp.pad(wf, ((0, 0), (0, LANES - N_TOK_HEADS)))
    wf_hi = wf_pad.astype(BF16)
    wf_lo = (wf_pad - wf_hi.astype(F32)).astype(BF16)
    bf_pad = jnp.pad(b_f, (0, LANES - N_TOK_HEADS)).reshape(1, LANES)
    row = jnp.arange(C_PARTS * LANES)
    col = jnp.arange(N_HEAD_PAIRS * LANES)
    head, part = row[:, None] % LANES, row[:, None] // LANES
    place = ((head // 2 == col[None, :] // LANES)
             & (col[None, :] % LANES == C_PARTS * (head % 2) + part)).astype(BF16)

    row_spec = lambda w: pl.BlockSpec((1, tm, w), lambda b, i: (b, i, 0))
    tok = jax.ShapeDtypeStruct((B, S, TOK_W), BF16)
    return pl.pallas_call(
        _fox_in_kernel,
        grid=(B, S // tm),
        in_specs=[row_spec(D_MODEL),
                  _full_spec((D_MODEL, TOK_W)), _full_spec((D_MODEL, TOK_W)),
                  _full_spec((D_MODEL, TOK_W)), _full_spec((D_MODEL, TOK_W)),
                  _full_spec((D_MODEL, LANES)), _full_spec((D_MODEL, LANES)),
                  _full_spec((D_MODEL, MEM_W)), _full_spec((1, LANES)),
                  _full_spec((C_PARTS * LANES, N_HEAD_PAIRS * LANES))],
        out_specs=[row_spec(TOK_W)] * 5 + [row_spec(MEM_W)],
        out_shape=[tok] * 5 + [jax.ShapeDtypeStruct((B, S, MEM_W), BF16)],
        scratch_shapes=[pltpu.VMEM((1, LANES), F32)],
        compiler_params=_cparams(("arbitrary", "arbitrary")),
        name="fox_in",
    )(x, (wq * (scale * LOG2E)).astype(BF16), wk.astype(BF16), wv.astype(BF16), wg.astype(BF16), wf_hi,
      wf_lo, (wm * scale).astype(BF16), bf_pad, place)


def _fox_attn_kernel(q_ref, k_ref, kc_ref, v_ref, g_ref, o_ref, sa_ref, sb_ref, *, tq):
    qi = pl.program_id(2)
    lane = _iota((tq, LANES), 1)
    q_pair = q_ref[0]
    q_rows = []
    for h in range(2):
        own = jnp.where(_div_pow2(lane, HEAD_DIM) == h, q_pair, jnp.zeros_like(q_pair))
        minus = jnp.where(jnp.logical_and(lane >= C_PARTS * h, lane < C_PARTS * (h + 1)), -1.0, 0.0)
        q_rows.append(jnp.concatenate([own, minus.astype(BF16)], axis=1))
    q_both = jnp.concatenate(q_rows, axis=0)

    def scores(kj, s_ref):
        start = pl.multiple_of(kj * tq, tq)
        k_aug = jnp.concatenate([k_ref[0, pl.ds(start, tq), :], kc_ref[0, pl.ds(start, tq), :]], axis=1)
        s_ref[...] = _dot_nt(q_both, k_aug)

    def absorb(kj, s_ref, carry, masked):
        start = pl.multiple_of(kj * tq, tq)
        v = v_ref[0, pl.ds(start, tq), :]
        out = []
        for h in range(2):
            m, l, acc = carry[h]
            s = s_ref[h * tq:(h + 1) * tq, :]
            if masked:
                s = jnp.where(_iota((tq, tq), 1) <= _iota((tq, tq), 0), s, -jnp.inf)
            m_new = jnp.maximum(m, jnp.max(s, axis=1, keepdims=True))
            p = jnp.exp2(s - m_new)
            alpha = jnp.exp2(m - m_new)
            p_lanes = functools.reduce(
                jnp.add, [p[:, i * LANES:(i + 1) * LANES] for i in range(tq // LANES)])
            l = alpha * l + p_lanes
            acc = alpha * acc + _dot(p.astype(BF16), v)
            out.append((m_new, l, acc))
        return tuple(out)

    def pair(i, carry):
        scores(2 * i + 1, sb_ref)
        carry = absorb(2 * i, sa_ref, carry, masked=False)
        scores(2 * i + 2, sa_ref)
        return absorb(2 * i + 1, sb_ref, carry, masked=False)

    def odd_tail(carry):
        scores(qi, sb_ref)
        carry = absorb(qi - 1, sa_ref, carry, masked=False)
        return absorb(qi, sb_ref, carry, masked=True)

    def even_tail(carry):
        return absorb(qi, sa_ref, carry, masked=True)

    init = (jnp.full((tq, 1), -jnp.inf, F32), jnp.zeros((tq, LANES), F32), jnp.zeros((tq, LANES), F32))
    scores(0, sa_ref)
    carry = lax.fori_loop(0, jnp.right_shift(qi, 1), pair, (init, init))
    (_, l0, acc0), (_, l1, acc1) = lax.cond(jnp.bitwise_and(qi, 1) == 1, odd_tail, even_tail, carry)
    o0 = acc0 / jnp.sum(l0, axis=1, keepdims=True)
    o1 = acc1 / jnp.sum(l1, axis=1, keepdims=True)
    o = jnp.where(lane < HEAD_DIM, o0, o1)
    o_ref[0] = (o * g_ref[0].astype(F32)).astype(BF16)


def _fox_attn(q, k, kc, v, g, tq):
    B, S, _ = v.shape
    tile = pl.BlockSpec((1, tq, LANES), lambda b, p, i: (b, i, p))
    whole = pl.BlockSpec((1, S, LANES), lambda b, p, i: (b, 0, p))
    return pl.pallas_call(
        functools.partial(_fox_attn_kernel, tq=tq),
        grid=(B, N_HEAD_PAIRS, S // tq),
        in_specs=[tile, whole, whole, whole, tile],
        out_specs=tile,
        out_shape=jax.ShapeDtypeStruct((B, S, TOK_W), BF16),
        scratch_shapes=[pltpu.VMEM((2 * tq, tq), F32), pltpu.VMEM((2 * tq, tq), F32)],
        compiler_params=_cparams(("arbitrary", "arbitrary", "arbitrary")),
        name="fox_attn",
    )(q, k, kc, v, g)


def _rwkv_in_kernel(x_ref, w_ref, mu_ref, w0_ref, w2_ref, a0_ref, a2_ref, g2_ref, kk_ref, ka_ref,
                    rk_ref, head_ones_ref,
                    rt_ref, at_ref, bt_ref, kt_ref, bh_ref, kh_ref, v_ref, pc_ref, bonus_ref, g_ref,
                    qm_ref, prev_ref):
    @pl.when(pl.program_id(1) == 0)
    def _():
        prev_ref[...] = jnp.zeros_like(prev_ref)

    tm = x_ref.shape[1]
    h = _dot(x_ref[0].astype(BF16), w_ref[...])
    qm_ref[0] = (h[:, RWKV_SHIFT_W:] * (HEAD_DIM ** -0.5)).astype(BF16)
    hs = h[:, :RWKV_SHIFT_W]
    row = _iota(hs.shape, 0)
    shifted = jnp.where(row == 0, prev_ref[...], pltpu.roll(hs, 1, 0))
    prev_ref[...] = hs[tm - 1:tm, :]
    hs = hs + mu_ref[...] * (shifted - hs)
    r = hs[:, :TOK_W]
    k = hs[:, TOK_W:2 * TOK_W]
    v = hs[:, 2 * TOK_W:3 * TOK_W]
    wa = hs[:, 3 * TOK_W:3 * TOK_W + LANES]
    gd = hs[:, 3 * TOK_W + LANES:]
    w = w0_ref[...] + _dot(jnp.tanh(wa).astype(BF16), w2_ref[...])
    w = -(jnp.maximum(-w, 0.0) + jnp.log(1.0 + jnp.exp(-jnp.abs(w)))) - 0.5
    log_decay = -jnp.exp(w)
    a = _sigmoid(a0_ref[...] + _dot(wa.astype(BF16), a2_ref[...]))
    g_ref[0] = _dot(_sigmoid(gd).astype(BF16), g2_ref[...])
    kk = k * kk_ref[...]
    kk = kk / jnp.maximum(jnp.sqrt(_head_sums(kk * kk, head_ones_ref[...])), 1e-12)
    k = k * (1.0 + (a - 1.0) * ka_ref[...])
    bonus_ref[0] = r * k * rk_ref[...]
    v_ref[0] = v
    a_s = -kk
    b_s = kk * a
    same_chunk = _div_pow2(_iota((tm, tm), 0), CHUNK) == _div_pow2(_iota((tm, tm), 1), CHUNK)
    incl = (same_chunk & (_iota((tm, tm), 1) <= _iota((tm, tm), 0))).astype(BF16)
    d_hi, d_mid, d_lo = _split3(log_decay)
    cw = _dot(incl, d_hi) + _dot(incl, d_mid) + _dot(incl, d_lo)
    same = same_chunk.astype(BF16)
    cw_end = _dot(same, d_hi) + _dot(same, d_mid) + _dot(same, d_lo)
    e_in = jnp.exp(cw)
    e_out = jnp.exp(-cw)
    e_end = jnp.exp(cw_end - cw)
    rt_ref[0] = (r * e_in).astype(BF16)
    at_ref[0] = (a_s * jnp.exp(cw - log_decay)).astype(BF16)
    bt_ref[0] = (b_s * e_out).astype(BF16)
    kt_ref[0] = (k * e_out).astype(BF16)
    bh_ref[0] = (b_s * e_end).astype(BF16)
    kh_ref[0] = (k * e_end).astype(BF16)
    pc_ref[0] = jnp.exp(cw_end)


def _rwkv_in(x, w_in, mu, w0, w2, a0, a2, g2, k_k, k_a, r_k, tm):
    B, S, _ = x.shape
    in_w = w_in.shape[1]
    w2_pad = jnp.concatenate([w2, jnp.zeros((AAA_LORA, TOK_W), F32)], axis=0).astype(BF16)
    a2_pad = jnp.concatenate([jnp.zeros((DECAY_LORA, TOK_W), F32), a2], axis=0).astype(BF16)
    head_ones = _head_blocks(1.0)
    vec = lambda t: t.reshape(1, -1)
    row_spec = lambda w: pl.BlockSpec((1, tm, w), lambda b, i: (b, i, 0))
    tok = jax.ShapeDtypeStruct((B, S, TOK_W), F32)
    return pl.pallas_call(
        _rwkv_in_kernel,
        grid=(B, S // tm),
        in_specs=[row_spec(D_MODEL), _full_spec((D_MODEL, in_w)), _full_spec((1, RWKV_SHIFT_W)),
                  _full_spec((1, TOK_W)), _full_spec((LANES, TOK_W)), _full_spec((1, TOK_W)),
                  _full_spec((LANES, TOK_W)), _full_spec((GATE_LORA, TOK_W)), _full_spec((1, TOK_W)),
                  _full_spec((1, TOK_W)), _full_spec((1, TOK_W)), _full_spec((MXU_W, MXU_W))],
        out_specs=[row_spec(TOK_W)] * 10 + [row_spec(MEM_W)],
        out_shape=[jax.ShapeDtypeStruct((B, S, TOK_W), BF16)] * 6 + [tok] * 4
        + [jax.ShapeDtypeStruct((B, S, MEM_W), BF16)],
        scratch_shapes=[pltpu.VMEM((1, RWKV_SHIFT_W), F32)],
        compiler_params=_cparams(("arbitrary", "arbitrary")),
        name="rwkv_in",
    )(x, w_in.astype(BF16), vec(mu), vec(w0), w2_pad, vec(a0), a2_pad, g2.astype(BF16), vec(k_k),
      vec(k_a), vec(r_k), head_ones)


def _rwkv_scan_kernel(rt_ref, at_ref, bt_ref, kt_ref, bh_ref, kh_ref, v_ref, pc_ref, y_ref, state_ref,
                      *, n_chunks):
    @pl.when(pl.program_id(2) == 0)
    def _():
        state_ref[...] = jnp.zeros_like(state_ref)

    C = CHUNK
    lane_head = _div_pow2(_iota((2 * C, LANES), 1), HEAD_DIM)
    row_head = _div_pow2(_iota((2 * C, LANES), 0), C)
    keep = lane_head == row_head
    pos_r = _mod_pow2(_iota((2 * C, 2 * C), 0), C)
    pos_c = _mod_pow2(_iota((2 * C, 2 * C), 1), C)
    strict = pos_c < pos_r
    incl = pos_c <= pos_r
    eye = (_iota((2 * C, 2 * C), 0) == _iota((2 * C, 2 * C), 1)).astype(F32)

    def stack(ref, c):
        t = ref[0, c * C:(c + 1) * C, :]
        return jnp.where(keep, jnp.concatenate([t, t], axis=0), 0.0).astype(BF16)

    chunks = range(n_chunks)
    a2 = [stack(at_ref, c) for c in chunks]
    r2 = [stack(rt_ref, c) for c in chunks]
    bh2 = [stack(bh_ref, c) for c in chunks]
    kh2 = [stack(kh_ref, c) for c in chunks]
    v2 = [stack(v_ref, c) for c in chunks]
    gram = [_dot_nt(jnp.concatenate([a2[c], r2[c]], axis=0),
                    jnp.concatenate([stack(bt_ref, c), stack(kt_ref, c)], axis=0)) for c in chunks]
    power = [jnp.where(strict, gram[c][:2 * C, :2 * C], 0.0) for c in chunks]
    inv = [eye + power[c] for c in chunks]
    for _ in range(int(math.log2(C)) - 1):
        power = [_dot(p.astype(BF16), p.astype(BF16)) for p in power]
        inv = [inv[c] + _dot(inv[c].astype(BF16), power[c].astype(BF16)) for c in chunks]
    from_v = [_dot(jnp.concatenate([jnp.where(strict, gram[c][:2 * C, 2 * C:], 0.0),
                                    jnp.where(incl, gram[c][2 * C:, 2 * C:], 0.0)], axis=0).astype(BF16),
                   v2[c]) for c in chunks]
    wu = [_dot(inv[c].astype(BF16),
               jnp.concatenate([a2[c], from_v[c][:2 * C].astype(BF16)], axis=1)) for c in chunks]
    wy = [_dot(jnp.where(incl, gram[c][2 * C:, :2 * C], 0.0).astype(BF16), wu[c].astype(BF16))
          for c in chunks]
    w_y = [(r2[c].astype(F32) + wy[c][:, :LANES]).astype(BF16) for c in chunks]
    y0 = [from_v[c][2 * C:] + wy[c][:, LANES:] for c in chunks]
    gc = [_dot(wu[c].T.astype(BF16), bh2[c]) for c in chunks]
    c0 = [gc[c][LANES:] + _dot(v2[c].astype(F32).T.astype(BF16), kh2[c]) for c in chunks]
    state = state_ref[...]
    for c in chunks:
        sb = state.astype(BF16)
        y2 = _dot_nt(w_y[c], sb) + y0[c]
        y_ref[0, c * C:(c + 1) * C, :] = y2[:C] + y2[C:]
        state = state * pc_ref[0, c * C:c * C + 1, :] + _dot(sb, gc[c][:LANES].astype(BF16)) + c0[c]
    state_ref[...] = state


def _rwkv_scan(rt, at, bt, kt, bh, kh, v, pc, rows):
    B, S, _ = v.shape
    spec = pl.BlockSpec((1, rows, LANES), lambda b, p, i: (b, i, p))
    return pl.pallas_call(
        functools.partial(_rwkv_scan_kernel, n_chunks=rows // CHUNK),
        grid=(B, N_HEAD_PAIRS, S // rows),
        in_specs=[spec] * 8,
        out_specs=spec,
        out_shape=jax.ShapeDtypeStruct((B, S, TOK_W), F32),
        scratch_shapes=[pltpu.VMEM((LANES, LANES), F32)],
        compiler_params=_cparams(("arbitrary", "arbitrary", "arbitrary")),
        name="rwkv_scan",
    )(rt, at, bt, kt, bh, kh, v, pc)


def _rwkv_post_kernel(y_ref, bonus_ref, v_ref, g_ref, lg_ref, lb_ref, head_mean_ref, o_ref):
    y = y_ref[0]
    head_mean = functools.partial(_head_sums, blocks=head_mean_ref[...])

    yc = y - head_mean(y)
    var = head_mean(yc * yc)
    yn = yc * lax.rsqrt(var + GN_EPS) * lg_ref[...] + lb_ref[...]
    bonus = head_mean(bonus_ref[0]) * float(HEAD_DIM)
    o_ref[0] = ((yn + bonus * v_ref[0]) * g_ref[0]).astype(BF16)


def _rwkv_post(y, bonus, v, g, lnx_g, lnx_b, tm):
    B, S, _ = y.shape
    head_mean = _head_blocks(1.0 / HEAD_DIM)
    row_spec = pl.BlockSpec((1, tm, TOK_W), lambda b, i: (b, i, 0))
    return pl.pallas_call(
        _rwkv_post_kernel,
        grid=(B, S // tm),
        in_specs=[row_spec] * 4 + [_full_spec((1, TOK_W)), _full_spec((1, TOK_W)),
                                   _full_spec((MXU_W, MXU_W))],
        out_specs=row_spec,
        out_shape=jax.ShapeDtypeStruct((B, S, TOK_W), BF16),
        compiler_params=_cparams(("arbitrary", "arbitrary")),
        name="rwkv_post",
    )(y, bonus, v, g, lnx_g.reshape(1, -1), lnx_b.reshape(1, -1), head_mean)


def _mix_out_kernel(tok_ref, qm_ref, km_ref, vm_ref, x_ref, wo_tok_ref, wo_mem_ref, g_ref, b_ref,
                    wr_hi_ref, wr_lo_ref, rb_ref, x1_ref, x1b_ref, gate_ref):
    tm = x_ref.shape[1]
    groups = [slice(r, r + tm // MIX_GROUPS) for r in range(0, tm, tm // MIX_GROUPS)]
    mem_out = [None] * MIX_GROUPS
    for h in range(N_MEM_HEADS):
        for i, rows in enumerate(groups):
            s = _dot_nt(qm_ref[0, rows, :], km_ref[0, h])
            e = jnp.exp(s - jnp.max(s, axis=1, keepdims=True))
            o = _dot(e.astype(BF16), vm_ref[0, h]) / jnp.sum(e, axis=1, keepdims=True)
            mem_out[i] = o if mem_out[i] is None else mem_out[i] + o
    x1 = []
    for i, rows in enumerate(groups):
        mixed = _dot(tok_ref[0, rows, :], wo_tok_ref[...]) + _dot(mem_out[i].astype(BF16), wo_mem_ref[...])
        x1.append(_layer_norm(DEEPNORM_ALPHA * x_ref[0, rows, :] + mixed, g_ref[...], b_ref[...]))
        x1_ref[0, rows, :] = x1[i]
        x1b_ref[0, rows, :] = x1[i].astype(BF16)
    scores = [_sigmoid(_dot3(x1[i], wr_hi_ref[...], wr_lo_ref[...])) for i in range(MIX_GROUPS)]
    lane = _iota(scores[0].shape, 1)
    lane_f = lane.astype(F32)
    cand = [jnp.where(lane < N_EXPERTS, sc + rb_ref[...], -jnp.inf) for sc in scores]
    picked = [jnp.zeros(scores[0].shape, jnp.bool_)] * MIX_GROUPS
    for _ in range(TOP_K):
        for i in range(MIX_GROUPS):
            best = jnp.max(cand[i], axis=1, keepdims=True)
            first = jnp.min(jnp.where(cand[i] == best, lane_f, float(LANES)), axis=1, keepdims=True)
            hit = lane_f == first
            picked[i] = jnp.logical_or(picked[i], hit)
            cand[i] = jnp.where(hit, -jnp.inf, cand[i])
    for i, rows in enumerate(groups):
        chosen = jnp.where(picked[i], scores[i], 0.0)
        gate = chosen / jnp.sum(chosen, axis=1, keepdims=True) * ROUTED_SCALE
        gate_ref[0, rows, :] = jnp.where(lane == N_EXPERTS, 1.0, gate)


def _mix_out(tok, qm, km, vm, x, w_out, ln_g, ln_b, w_router, router_bias, tm):
    B, S, _ = x.shape
    n_mem = km.shape[2]
    wr = jnp.pad(w_router, ((0, 0), (0, LANES - N_EXPERTS)))
    wr_hi = wr.astype(BF16)
    wr_lo = (wr - wr_hi.astype(F32)).astype(BF16)
    rb = jnp.pad(router_bias, (0, LANES - N_EXPERTS)).reshape(1, LANES)
    row_spec = lambda w: pl.BlockSpec((1, tm, w), lambda b, i: (b, i, 0))
    mem_spec = pl.BlockSpec((1, N_MEM_HEADS, n_mem, MEM_W), lambda b, i: (b, 0, 0, 0))
    return pl.pallas_call(
        _mix_out_kernel,
        grid=(B, S // tm),
        in_specs=[row_spec(TOK_W), row_spec(MEM_W), mem_spec, mem_spec, row_spec(D_MODEL),
                  _full_spec((TOK_W, D_MODEL)), _full_spec((MEM_W, D_MODEL)),
                  _full_spec((1, D_MODEL)), _full_spec((1, D_MODEL)),
                  _full_spec((D_MODEL, LANES)), _full_spec((D_MODEL, LANES)), _full_spec((1, LANES))],
        out_specs=[row_spec(D_MODEL), row_spec(D_MODEL), row_spec(LANES)],
        out_shape=[jax.ShapeDtypeStruct((B, S, D_MODEL), F32), jax.ShapeDtypeStruct((B, S, D_MODEL), BF16),
                   jax.ShapeDtypeStruct((B, S, LANES), F32)],
        compiler_params=_cparams(("arbitrary", "arbitrary")),
        name="mix_out",
    )(tok, qm, km, vm, x, w_out[:TOK_W].astype(BF16), w_out[TOK_W:].astype(BF16),
      ln_g.reshape(1, -1), ln_b.reshape(1, -1), wr_hi, wr_lo, rb)


MOE_TM = 512
UNIT = 16
SLAB_ROWS = 4096
SLAB_UNITS = SLAB_ROWS // UNIT
BLK_UNITS = 64
BLK_ROWS = BLK_UNITS * UNIT
XS_W = D_MODEL + 2 * LANES
ROW_CHUNK = 1024
BF16_EXACT_INT = 256
assert TOP_K * MOE_TM + N_EXPERTS * (UNIT - 1) <= SLAB_ROWS - UNIT
ZERO_UNIT = SLAB_UNITS - 1


def _dot_x_sel(x, sel):
    hi, mid, lo = _split3(x)
    return _dot(hi, sel) + _dot(mid, sel) + _dot(lo, sel)


def _route_tile(gates):
    tm = gates.shape[0]
    sel = jnp.logical_and(gates > 0.0, _iota(gates.shape, 1) < N_EXPERTS)
    sel_b = jnp.where(sel, 1.0, 0.0).astype(BF16)
    earlier = (_iota((tm, tm), 1) < _iota((tm, tm), 0)).astype(BF16)
    rank1 = jnp.where(sel, _dot(earlier, sel_b) + 1.0, 0.0)
    count = _dot(jnp.ones((8, tm), BF16), sel_b)[0:1]
    n16 = jnp.floor((count + float(UNIT - 1)) * (1.0 / UNIT))
    before = (_iota((LANES, LANES), 0) < _iota((LANES, LANES), 1)).astype(BF16)
    off16 = _dot(jnp.broadcast_to(n16, (8, LANES)).astype(BF16), before)[0:1]
    r_hi, r_lo = _split2(rank1)
    return r_hi, r_lo, jnp.max(count) > float(BF16_EXACT_INT), n16, off16


def _slab_experts(n16, off16, r0, rows):
    r = (_iota((rows, LANES), 0) + r0).astype(F32)
    lo = off16 * float(UNIT)
    return jnp.logical_and(r >= lo, r < lo + n16 * float(UNIT))


def _moe_dispatch_kernel(xb_ref, gate_ref, xs_ref, n_ref):
    gates = gate_ref[...]
    r_hi, r_lo, has_lo, n16, off16 = _route_tile(gates)
    n_ref[0] = jnp.broadcast_to(n16, (8, LANES))
    g_hi, g_lo = _split2(gates)
    src = jnp.concatenate([xb_ref[...], g_hi, g_lo], axis=1)

    def fill_slab(with_lo):
        for c in range(SLAB_ROWS // ROW_CHUNK):
            r0 = c * ROW_CHUNK
            seg = _slab_experts(n16, off16, r0, ROW_CHUNK)
            seg_b = jnp.where(seg, 1.0, 0.0).astype(BF16)
            rank_at = _dot_nt(seg_b, r_hi)
            if with_lo:
                rank_at = rank_at + _dot_nt(seg_b, r_lo)
            r = (_iota((ROW_CHUNK, 1), 0) + r0).astype(F32)
            pos1 = r + 1.0 - jnp.sum(jnp.where(seg, off16 * float(UNIT), 0.0), axis=1, keepdims=True)
            pick = jnp.where(rank_at == pos1, 1.0, 0.0).astype(BF16)
            xs_ref[r0:r0 + ROW_CHUNK, :] = _dot(pick, src).astype(BF16)

    lax.cond(has_lo, functools.partial(fill_slab, True), functools.partial(fill_slab, False))


def _moe_dispatch(xb, gates):
    T = xb.shape[0]
    nt = T // MOE_TM
    return pl.pallas_call(
        _moe_dispatch_kernel,
        grid=(nt,),
        in_specs=[pl.BlockSpec((MOE_TM, D_MODEL), lambda i: (i, 0)),
                  pl.BlockSpec((MOE_TM, LANES), lambda i: (i, 0))],
        out_specs=[pl.BlockSpec((SLAB_ROWS, XS_W), lambda i: (i, 0)),
                   pl.BlockSpec((1, 8, LANES), lambda i: (i, 0, 0))],
        out_shape=[jax.ShapeDtypeStruct((nt * SLAB_ROWS, XS_W), BF16),
                   jax.ShapeDtypeStruct((nt, 8, LANES), F32)],
        compiler_params=_cparams(("arbitrary",)),
        name="moe_dispatch",
    )(xb, gates)


def _moe_plan_kernel(n_ref, unit_ref, expert_ref, count_ref, *, nt, nb):
    n16 = n_ref[...]
    lane = _iota((nt, LANES), 1)
    before = (_iota((LANES, LANES), 0) < _iota((LANES, LANES), 1)).astype(BF16)
    off16 = _dot(n16.astype(BF16), before)
    used = jnp.sum(n16, axis=1, keepdims=True)
    n16 = n16 + jnp.where(lane == N_EXPERTS - 1, float(SLAB_UNITS) - used, 0.0)
    earlier = (_iota((nt, nt), 1) < _iota((nt, nt), 0)).astype(BF16)
    cum_ex = _dot_sel(earlier, n16)
    cum_in = cum_ex + n16
    total = cum_in[nt - 1:nt, :]
    n_blk = jnp.floor((total + float(BLK_UNITS - 1)) * (1.0 / BLK_UNITS))
    blk_start = _dot_x_sel(jnp.broadcast_to(n_blk, (8, LANES)), before)[0:1]
    blk_end = blk_start + n_blk
    count_ref[...] = jnp.broadcast_to(jnp.sum(n_blk, axis=1, keepdims=True), (8, LANES)).astype(jnp.int32)

    lane_b = _iota((nb, LANES), 1)
    blk = _iota((nb, LANES), 0).astype(F32)
    expert = jnp.sum(jnp.where(jnp.logical_and(lane_b < N_EXPERTS, blk_end <= blk), 1.0, 0.0),
                     axis=1, keepdims=True)
    live = expert < float(N_EXPERTS)
    expert = jnp.minimum(expert, float(N_EXPERTS - 1))
    expert_ref[...] = jnp.broadcast_to(expert, (nb, LANES)).astype(jnp.int32)
    mine = lane_b.astype(F32) == expert
    mine_b = jnp.where(mine, 1.0, 0.0).astype(BF16)
    my_start = jnp.sum(jnp.where(mine, blk_start, 0.0), axis=1, keepdims=True)
    my_total = jnp.sum(jnp.where(mine, total, 0.0), axis=1, keepdims=True)
    q = (blk - my_start) * float(BLK_UNITS) + lane_b.astype(F32)
    valid = jnp.logical_and(jnp.logical_and(q < my_total, lane_b < BLK_UNITS), live)

    def per_slab(table):
        hi, mid, lo = _split3(table)
        return _dot_nt(mine_b, hi) + _dot_nt(mine_b, mid) + _dot_nt(mine_b, lo)

    ex, inc, first = per_slab(cum_ex), per_slab(cum_in), per_slab(off16)
    base = jnp.zeros((nb, LANES), F32)
    for i in range(nt):
        hit = jnp.logical_and(q >= ex[:, i:i + 1], q < inc[:, i:i + 1])
        base = base + jnp.where(hit, float(i * SLAB_UNITS) + first[:, i:i + 1] - ex[:, i:i + 1], 0.0)
    unit_ref[...] = jnp.where(valid, base + q, -1.0).astype(jnp.int32)


def _moe_plan(n16, nb):
    nt = n16.shape[0]
    tbl = jax.ShapeDtypeStruct((nb, LANES), jnp.int32)
    units, experts, count = pl.pallas_call(
        functools.partial(_moe_plan_kernel, nt=nt, nb=nb),
        out_shape=[tbl, tbl, jax.ShapeDtypeStruct((8, LANES), jnp.int32)],
        compiler_params=pltpu.CompilerParams(vmem_limit_bytes=VMEM_LIMIT),
        name="moe_plan",
    )(n16)
    return units[:, :BLK_UNITS].reshape(-1), experts[:, 0], count[0, :1]


def _moe_ffn_kernel(unit_ref, expert_ref, count_ref, xs_hbm, wg_ref, wu_ref, wd_ref, ys_hbm,
                    ibuf, obuf, in_sem, out_sem, *, nb):
    b = pl.program_id(0)
    n_live = count_ref[0]
    slot = jnp.bitwise_and(b, 1)

    def in_copy(blk, buf, s):
        u = unit_ref[blk * BLK_UNITS + s]
        u = jnp.where(u < 0, ZERO_UNIT, u)
        return pltpu.make_async_copy(xs_hbm.at[pl.ds(pl.multiple_of(u * UNIT, UNIT), UNIT), :],
                                     ibuf.at[buf, pl.ds(s * UNIT, UNIT), :], in_sem.at[buf])

    def out_copy(blk, buf, s):
        u = unit_ref[blk * BLK_UNITS + s]
        return u, pltpu.make_async_copy(
            obuf.at[buf, pl.ds(s * UNIT, UNIT), :],
            ys_hbm.at[pl.ds(pl.multiple_of(jnp.maximum(u, 0) * UNIT, UNIT), UNIT), :], out_sem.at[buf])

    def whole_block(blk):
        return unit_ref[blk * BLK_UNITS + BLK_UNITS - 1] >= 0

    def for_each_used_out(blk, buf, act):
        for s in range(BLK_UNITS):
            u, copy = out_copy(blk, buf, s)
            pl.when(u >= 0)(functools.partial(act, copy))

    def wait_out(blk, buf):
        @pl.when(whole_block(blk))
        def _():
            pltpu.make_async_copy(obuf.at[buf], ys_hbm.at[pl.ds(0, BLK_ROWS), :], out_sem.at[buf]).wait()

        @pl.when(jnp.logical_not(whole_block(blk)))
        def _():
            for_each_used_out(blk, buf, lambda c: c.wait())

    def start_in(blk, buf):
        for s in range(BLK_UNITS):
            in_copy(blk, buf, s).start()

    @pl.when(b == 0)
    def _():
        start_in(0, 0)

    @pl.when(b <= n_live)
    def _():
        pltpu.make_async_copy(xs_hbm.at[pl.ds(0, BLK_ROWS), :], ibuf.at[slot], in_sem.at[slot]).wait()

    @pl.when(jnp.logical_and(b >= 2, b - 2 < n_live))
    def _():
        wait_out(b - 2, slot)

    def expert_rows(rows):
        x = rows[:, :D_MODEL]
        gates = rows[:, D_MODEL:D_MODEL + LANES].astype(F32) + rows[:, D_MODEL + LANES:].astype(F32)
        gate = jnp.sum(jnp.where(_iota(gates.shape, 1) == expert_ref[b], gates, 0.0), axis=1, keepdims=True)
        hg = _dot(x, wg_ref[0, 0].astype(BF16))
        hu = _dot(x, wu_ref[0, 0].astype(BF16))
        hidden = hg * _sigmoid(hg) * hu * gate
        return _dot(hidden.astype(BF16), wd_ref[0, 0].astype(BF16)).astype(BF16)

    live = b < n_live

    @pl.when(jnp.logical_and(live, whole_block(b)))
    def _():
        start_in(b + 1, 1 - slot)
        half = BLK_UNITS // 2
        for part in range(2):
            rows = pl.ds(part * half * UNIT, half * UNIT)
            obuf[slot, rows, :] = expert_rows(ibuf[slot, rows, :])
            for s in range(part * half, (part + 1) * half):
                out_copy(b, slot, s)[1].start()

    @pl.when(jnp.logical_and(live, jnp.logical_not(whole_block(b))))
    def _():
        start_in(b + 1, 1 - slot)
        obuf[slot] = expert_rows(ibuf[slot])
        for_each_used_out(b, slot, lambda c: c.start())

    @pl.when(jnp.logical_and(b == nb - 1, nb - 2 < n_live))
    def _():
        wait_out(nb - 2, 1 - slot)


def _moe_ffn(xs, units, experts, count, w_gate, w_up, w_down, layer, nb):
    n_rows = xs.shape[0]
    w_spec = lambda r, c: pl.BlockSpec((1, 1, r, c), lambda b, u, e, n: (layer, e[b], 0, 0))
    grid_spec = pltpu.PrefetchScalarGridSpec(
        num_scalar_prefetch=3,
        grid=(nb,),
        in_specs=[pl.BlockSpec(memory_space=pl.ANY),
                  w_spec(D_MODEL, D_EXPERT), w_spec(D_MODEL, D_EXPERT), w_spec(D_EXPERT, D_MODEL)],
        out_specs=pl.BlockSpec(memory_space=pl.ANY),
        scratch_shapes=[pltpu.VMEM((2, BLK_ROWS, XS_W), BF16), pltpu.VMEM((2, BLK_ROWS, D_MODEL), BF16),
                        pltpu.SemaphoreType.DMA((2,)), pltpu.SemaphoreType.DMA((2,))],
    )
    return pl.pallas_call(
        functools.partial(_moe_ffn_kernel, nb=nb),
        grid_spec=grid_spec,
        out_shape=jax.ShapeDtypeStruct((n_rows, D_MODEL), BF16),
        compiler_params=_cparams(("arbitrary",)),
        name="moe_ffn",
    )(units, experts, count, xs, w_gate, w_up, w_down)


def _moe_combine_kernel(ys_ref, gate_ref, x_ref, xb_ref, wsg_ref, wsu_ref, wsd_ref, g_ref, b_ref, o_ref):
    r_hi, r_lo, has_lo, n16, off16 = _route_tile(gate_ref[...])
    off_b = jnp.broadcast_to(off16, (8, LANES)).astype(BF16)

    def finish(with_lo):
        xb = xb_ref[...]
        hg = _dot(xb, wsg_ref[...])
        acc = _dot((hg * _sigmoid(hg) * _dot(xb, wsu_ref[...])).astype(BF16), wsd_ref[...])
        for c in range(SLAB_ROWS // ROW_CHUNK):
            r0 = c * ROW_CHUNK
            seg_b = jnp.where(_slab_experts(n16, off16, r0, ROW_CHUNK), 1.0, 0.0).astype(BF16)
            rank_at = _dot_nt(r_hi, seg_b)
            if with_lo:
                rank_at = rank_at + _dot_nt(r_lo, seg_b)
            r = (_iota((1, ROW_CHUNK), 1) + r0).astype(F32)
            pos1 = r + 1.0 - _dot_nt(off_b, seg_b)[0:1] * float(UNIT)
            pick = jnp.where(rank_at == pos1, 1.0, 0.0).astype(BF16)
            acc = acc + _dot(pick, ys_ref[r0:r0 + ROW_CHUNK, :])
        o_ref[...] = _layer_norm(DEEPNORM_ALPHA * x_ref[...] + acc, g_ref[...], b_ref[...])

    lax.cond(has_lo, functools.partial(finish, True), functools.partial(finish, False))


def _moe_combine(ys, gates, x, xb, w_sh_gate, w_sh_up, w_sh_down, ln_g, ln_b):
    T = x.shape[0]
    row_spec = lambda w: pl.BlockSpec((MOE_TM, w), lambda i: (i, 0))
    return pl.pallas_call(
        _moe_combine_kernel,
        grid=(T // MOE_TM,),
        in_specs=[pl.BlockSpec((SLAB_ROWS, D_MODEL), lambda i: (i, 0)),
                  row_spec(LANES), row_spec(D_MODEL), row_spec(D_MODEL),
                  _full_spec((D_MODEL, D_EXPERT)), _full_spec((D_MODEL, D_EXPERT)),
                  _full_spec((D_EXPERT, D_MODEL)), _full_spec((1, D_MODEL)), _full_spec((1, D_MODEL))],
        out_specs=row_spec(D_MODEL),
        out_shape=jax.ShapeDtypeStruct((T, D_MODEL), F32),
        compiler_params=_cparams(("arbitrary",)),
        name="moe_combine",
    )(ys, gates, x, xb, w_sh_gate.astype(BF16), w_sh_up.astype(BF16), w_sh_down.astype(BF16),
      ln_g.reshape(1, -1), ln_b.reshape(1, -1))


def _moe(xb, x, gates, w_gate, w_up, w_down, layer, w_sh_gate, w_sh_up, w_sh_down, ln_g, ln_b):
    T = x.shape[0]
    nt = T // MOE_TM
    nb = nt * SLAB_UNITS // BLK_UNITS + N_EXPERTS + 1
    xs, n16 = _moe_dispatch(xb, gates)
    units, experts, count = _moe_plan(n16[:, 0, :], nb)
    ys = _moe_ffn(xs, units, experts, count, w_gate, w_up, w_down, layer, nb)
    return _moe_combine(ys, gates, x, xb, w_sh_gate, w_sh_up, w_sh_down, ln_g, ln_b)


def _tile(n, want):
    t = min(n, want)
    assert n % t == 0, (n, t)
    return t


def kernel(x, mem, mem_ln_g, mem_ln_b, w_mem_kv, fox_w_in, fox_b_f, rwkv_w_in, rwkv_mu, rwkv_w0, rwkv_w2,
           rwkv_a0, rwkv_a2, rwkv_g2, rwkv_k_k, rwkv_k_a, rwkv_r_k, rwkv_lnx_g, rwkv_lnx_b, w_out, ln1_g,
           ln1_b, w_router, router_bias, w_exp_gate, w_exp_up, w_exp_down, w_sh_gate, w_sh_up, w_sh_down,
           ln2_g, ln2_b):
    B, S, D = x.shape
    T = B * S
    assert D == D_MODEL and S % CHUNK == 0 and T % MOE_TM == 0
    t_proj = _tile(S, 512)
    t_attn = _tile(S, 512)
    t_rwkv = _tile(S, 256)
    t_scan = _tile(S, 512)

    km, vm = _mem_kv(mem, mem_ln_g, mem_ln_b, w_mem_kv)
    for i in range(DEPTH):
        j = i // 2
        if i % 2 == 0:
            q, k, kc, v, g, qm = _fox_in(x, fox_w_in[j], fox_b_f[j], t_proj)
            tok = _fox_attn(q, k, kc, v, g, t_attn)
        else:
            (rt, at, bt, kt, bh, kh, v, pc, bonus, g, qm) = _rwkv_in(
                x, rwkv_w_in[j], rwkv_mu[j], rwkv_w0[j], rwkv_w2[j], rwkv_a0[j], rwkv_a2[j], rwkv_g2[j],
                rwkv_k_k[j], rwkv_k_a[j], rwkv_r_k[j].reshape(-1), t_rwkv)
            y = _rwkv_scan(rt, at, bt, kt, bh, kh, v, pc, t_scan)
            tok = _rwkv_post(y, bonus, v, g, rwkv_lnx_g[j], rwkv_lnx_b[j], t_proj)
        x1, x1b, gates = _mix_out(tok, qm, km, vm, x, w_out[i], ln1_g[i], ln1_b[i], w_router[i],
                                  router_bias[i], t_proj)
        x = _moe(x1b.reshape(T, D), x1.reshape(T, D), gates.reshape(T, LANES), w_exp_gate, w_exp_up,
                 w_exp_down, i, w_sh_gate[i], w_sh_up[i], w_sh_down[i], ln2_g[i], ln2_b[i]).reshape(B, S, D)
    return x
```

```python
import functools
import math

import jax
import jax.numpy as jnp
from jax import lax
from jax.experimental import pallas as pl
from jax.experimental.pallas import tpu as pltpu

F32 = jnp.float32
BF16 = jnp.bfloat16

D_MODEL = 1024
HEAD_DIM = 64
N_TOK_HEADS = 12
TOK_W = N_TOK_HEADS * HEAD_DIM
N_HEAD_PAIRS = N_TOK_HEADS // 2
N_MEM_HEADS = 4
MEM_W = N_MEM_HEADS * HEAD_DIM
DECAY_LORA = 64
AAA_LORA = 64
GATE_LORA = 128
RWKV_SHIFT_W = 3 * TOK_W + DECAY_LORA + AAA_LORA + GATE_LORA
N_EXPERTS = 64
TOP_K = 6
D_EXPERT = 256
ROUTED_SCALE = 2.5
DEPTH = 2
DEEPNORM_ALPHA = (2 * DEPTH) ** 0.25
LN_EPS = 1e-5
GN_EPS = 64e-5
LOG2E = math.log2(math.e)

LANES = 128
C_PARTS = 3
CHUNK = 64
SCAN_TILES = 3
MIX_GROUPS = 2
VMEM_LIMIT = 56 * 1024 * 1024


def _cparams(sem):
    return pltpu.CompilerParams(dimension_semantics=sem, vmem_limit_bytes=VMEM_LIMIT)


def _dot(a, b):
    return jnp.dot(a, b, preferred_element_type=F32)


def _dot_nt(a, b):
    return lax.dot_general(a, b, (((1,), (1,)), ((), ())), preferred_element_type=F32)


def _split2(x):
    hi = x.astype(BF16)
    lo = (x - hi.astype(F32)).astype(BF16)
    return hi, lo


def _split3(x):
    hi = x.astype(BF16)
    r1 = x - hi.astype(F32)
    mid = r1.astype(BF16)
    lo = (r1 - mid.astype(F32)).astype(BF16)
    return hi, mid, lo


def _dot_sel(sel, x):
    hi, mid, lo = _split3(x)
    return _dot(sel, hi) + _dot(sel, mid) + _dot(sel, lo)


def _dot3(x, w_hi, w_lo):
    x_hi, x_lo = _split2(x)
    return _dot(x_hi, w_hi) + _dot(x_lo, w_hi) + _dot(x_hi, w_lo)


MXU_W = 256


def _head_blocks(value):
    head = jnp.arange(MXU_W) // HEAD_DIM
    return jnp.where(head[:, None] == head[None, :], value, 0.0).astype(BF16)


def _head_sums(t, blocks):
    cols = []
    for c in range(0, t.shape[1], MXU_W):
        hi, lo = _split2(t[:, c:c + MXU_W])
        cols.append(_dot(hi, blocks) + _dot(lo, blocks))
    return jnp.concatenate(cols, axis=1)


def _sigmoid(x):
    return 1.0 / (1.0 + jnp.exp(-x))


def _layer_norm(y, g, b):
    mu = jnp.mean(y, axis=-1, keepdims=True)
    yc = y - mu
    var = jnp.mean(yc * yc, axis=-1, keepdims=True)
    return yc * lax.rsqrt(var + LN_EPS) * g + b


def _iota(shape, dim):
    return lax.broadcasted_iota(jnp.int32, shape, dim)


def _div_pow2(x, n):
    return jnp.right_shift(x, int(math.log2(n)))


def _mod_pow2(x, n):
    return jnp.bitwise_and(x, n - 1)


def _full_spec(shape):
    n = len(shape)
    return pl.BlockSpec(shape, lambda *_: (0,) * n)


def _mem_kv_kernel(mem_ref, g_ref, b_ref, w_ref, k_ref, v_ref):
    m = _layer_norm(mem_ref[0], g_ref[...], b_ref[...])
    kv = _dot(m.astype(BF16), w_ref[...])
    k = kv[:, :MEM_W]
    v = kv[:, MEM_W:]
    head = _div_pow2(_iota(k.shape, 1), HEAD_DIM)
    for h in range(N_MEM_HEADS):
        k_ref[0, h] = jnp.where(head == h, k, 0.0).astype(BF16)
        v_ref[0, h] = jnp.where(head == h, v, 0.0).astype(BF16)


def _mem_kv(mem, g, b, w):
    B, n_mem, _ = mem.shape
    out = jax.ShapeDtypeStruct((B, N_MEM_HEADS, n_mem, MEM_W), BF16)
    out_spec = pl.BlockSpec((1, N_MEM_HEADS, n_mem, MEM_W), lambda i: (i, 0, 0, 0))
    return pl.pallas_call(
        _mem_kv_kernel,
        grid=(B,),
        in_specs=[pl.BlockSpec((1, n_mem, D_MODEL), lambda i: (i, 0, 0)),
                  _full_spec((1, D_MODEL)), _full_spec((1, D_MODEL)),
                  _full_spec((D_MODEL, 2 * MEM_W))],
        out_specs=[out_spec, out_spec],
        out_shape=[out, out],
        compiler_params=_cparams(("arbitrary",)),
        name="mem_kv",
    )(mem, g.reshape(1, -1), b.reshape(1, -1), w.astype(BF16))


def _fox_in_kernel(x_ref, wq_ref, wk_ref, wv_ref, wg_ref, wfh_ref, wfl_ref, wm_ref, bf_ref,
                   place_ref, q_ref, k_ref, kc_ref, v_ref, g_ref, qm_ref, carry_ref):
    @pl.when(pl.program_id(1) == 0)
    def _():
        carry_ref[...] = jnp.zeros_like(carry_ref)

    x = x_ref[0]
    xb = x.astype(BF16)
    tm = x.shape[0]
    z = _dot3(x, wfh_ref[...], wfl_ref[...]) + bf_ref[...]
    log_f = jnp.minimum(z, 0.0) - jnp.log(1.0 + jnp.exp(-jnp.abs(z)))
    tril = (_iota((tm, tm), 1) <= _iota((tm, tm), 0)).astype(BF16)
    c = _dot_sel(tril, log_f) + carry_ref[...]
    carry_ref[...] = c[tm - 1:tm, :]
    c_hi, c_mid, c_lo = _split3(c * LOG2E)
    c_parts = jnp.concatenate([c_hi, c_mid, c_lo], axis=1)
    q_ref[0] = _dot(xb, wq_ref[...]).astype(BF16)
    k_ref[0] = _dot(xb, wk_ref[...]).astype(BF16)
    kc_ref[0] = _dot(c_parts, place_ref[...]).astype(BF16)
    v_ref[0] = _dot(xb, wv_ref[...]).astype(BF16)
    g_ref[0] = _sigmoid(_dot(xb, wg_ref[...])).astype(BF16)
    qm_ref[0] = _dot(xb, wm_ref[...]).astype(BF16)


def _fox_in(x, w_in, b_f, tm):
    B, S, _ = x.shape
    scale = HEAD_DIM ** -0.5
    wq, wk, wv, wg, wf, wm = jnp.split(
        w_in, [TOK_W, 2 * TOK_W, 3 * TOK_W, 4 * TOK_W, 4 * TOK_W + N_TOK_HEADS], axis=1)
    wf_pad = jnp.pad(wf, ((0, 0), (0, LANES - N_TOK_HEADS)))
    wf_hi = wf_pad.astype(BF16)
    wf_lo = (wf_pad - wf_hi.astype(F32)).astype(BF16)
    bf_pad = jnp.pad(b_f, (0, LANES - N_TOK_HEADS)).reshape(1, LANES)
    row = jnp.arange(C_PARTS * LANES)
    col = jnp.arange(N_HEAD_PAIRS * LANES)
    head, part = row[:, None] % LANES, row[:, None] // LANES
    place = ((head // 2 == col[None, :] // LANES)
             & (col[None, :] % LANES == C_PARTS * (head % 2) + part)).astype(BF16)

    row_spec = lambda w: pl.BlockSpec((1, tm, w), lambda b, i: (b, i, 0))
    tok = jax.ShapeDtypeStruct((B, S, TOK_W), BF16)
    return pl.pallas_call(
        _fox_in_kernel,
        grid=(B, S // tm),
        in_specs=[row_spec(D_MODEL),
                  _full_spec((D_MODEL, TOK_W)), _full_spec((D_MODEL, TOK_W)),
                  _full_spec((D_MODEL, TOK_W)), _full_spec((D_MODEL, TOK_W)),
                  _full_spec((D_MODEL, LANES)), _full_spec((D_MODEL, LANES)),
                  _full_spec((D_MODEL, MEM_W)), _full_spec((1, LANES)),
                  _full_spec((C_PARTS * LANES, N_HEAD_PAIRS * LANES))],
        out_specs=[row_spec(TOK_W)] * 5 + [row_spec(MEM_W)],
        out_shape=[tok] * 5 + [jax.ShapeDtypeStruct((B, S, MEM_W), BF16)],
        scratch_shapes=[pltpu.VMEM((1, LANES), F32)],
        compiler_params=_cparams(("arbitrary", "arbitrary")),
        name="fox_in",
    )(x, (wq * (scale * LOG2E)).astype(BF16), wk.astype(BF16), wv.astype(BF16), wg.astype(BF16), wf_hi,
      wf_lo, (wm * scale).astype(BF16), bf_pad, place)


def _fox_attn_kernel(q_ref, k_ref, kc_ref, v_ref, g_ref, o_ref, sa_ref, sb_ref, *, tq):
    qi = pl.program_id(2)
    lane = _iota((tq, LANES), 1)
    q_pair = q_ref[0]
    q_rows = []
    for h in range(2):
        own = jnp.where(_div_pow2(lane, HEAD_DIM) == h, q_pair, jnp.zeros_like(q_pair))
        minus = jnp.where(jnp.logical_and(lane >= C_PARTS * h, lane < C_PARTS * (h + 1)), -1.0, 0.0)
        q_rows.append(jnp.concatenate([own, minus.astype(BF16)], axis=1))
    q_both = jnp.concatenate(q_rows, axis=0)

    def scores(kj, s_ref):
        start = pl.multiple_of(kj * tq, tq)
        k_aug = jnp.concatenate([k_ref[0, pl.ds(start, tq), :], kc_ref[0, pl.ds(start, tq), :]], axis=1)
        s_ref[...] = _dot_nt(q_both, k_aug)

    def absorb(kj, s_ref, carry, masked):
        start = pl.multiple_of(kj * tq, tq)
        v = v_ref[0, pl.ds(start, tq), :]
        out = []
        for h in range(2):
            m, l, acc = carry[h]
            s = s_ref[h * tq:(h + 1) * tq, :]
            if masked:
                s = jnp.where(_iota((tq, tq), 1) <= _iota((tq, tq), 0), s, -jnp.inf)
            m_new = jnp.maximum(m, jnp.max(s, axis=1, keepdims=True))
            p = jnp.exp2(s - m_new)
            alpha = jnp.exp2(m - m_new)
            p_lanes = functools.reduce(
                jnp.add, [p[:, i * LANES:(i + 1) * LANES] for i in range(tq // LANES)])
            l = alpha * l + p_lanes
            acc = alpha * acc + _dot(p.astype(BF16), v)
            out.append((m_new, l, acc))
        return tuple(out)

    def pair(i, carry):
        scores(2 * i + 1, sb_ref)
        carry = absorb(2 * i, sa_ref, carry, masked=False)
        scores(2 * i + 2, sa_ref)
        return absorb(2 * i + 1, sb_ref, carry, masked=False)

    def odd_tail(carry):
        scores(qi, sb_ref)
        carry = absorb(qi - 1, sa_ref, carry, masked=False)
        return absorb(qi, sb_ref, carry, masked=True)

    def even_tail(carry):
        return absorb(qi, sa_ref, carry, masked=True)

    init = (jnp.full((tq, 1), -jnp.inf, F32), jnp.zeros((tq, LANES), F32), jnp.zeros((tq, LANES), F32))
    scores(0, sa_ref)
    carry = lax.fori_loop(0, jnp.right_shift(qi, 1), pair, (init, init))
    (_, l0, acc0), (_, l1, acc1) = lax.cond(jnp.bitwise_and(qi, 1) == 1, odd_tail, even_tail, carry)
    o0 = acc0 / jnp.sum(l0, axis=1, keepdims=True)
    o1 = acc1 / jnp.sum(l1, axis=1, keepdims=True)
    o = jnp.where(lane < HEAD_DIM, o0, o1)
    o_ref[0] = (o * g_ref[0].astype(F32)).astype(BF16)


def _fox_attn(q, k, kc, v, g, tq):
    B, S, _ = v.shape
    tile = pl.BlockSpec((1, tq, LANES), lambda b, p, i: (b, i, p))
    whole = pl.BlockSpec((1, S, LANES), lambda b, p, i: (b, 0, p))
    return pl.pallas_call(
        functools.partial(_fox_attn_kernel, tq=tq),
        grid=(B, N_HEAD_PAIRS, S // tq),
        in_specs=[tile, whole, whole, whole, tile],
        out_specs=tile,
        out_shape=jax.ShapeDtypeStruct((B, S, TOK_W), BF16),
        scratch_shapes=[pltpu.VMEM((2 * tq, tq), F32), pltpu.VMEM((2 * tq, tq), F32)],
        compiler_params=_cparams(("arbitrary", "arbitrary", "arbitrary")),
        name="fox_attn",
    )(q, k, kc, v, g)


def _rwkv_in_kernel(x_ref, w_ref, mu_ref, w0_ref, w2_ref, a0_ref, a2_ref, g2_ref, kk_ref, ka_ref,
                    rk_ref, head_ones_ref,
                    rt_ref, at_ref, bt_ref, kt_ref, bh_ref, kh_ref, v_ref, pc_ref, bonus_ref, g_ref,
                    qm_ref, prev_ref):
    @pl.when(pl.program_id(1) == 0)
    def _():
        prev_ref[...] = jnp.zeros_like(prev_ref)

    tm = x_ref.shape[1]
    h = _dot(x_ref[0].astype(BF16), w_ref[...])
    qm_ref[0] = (h[:, RWKV_SHIFT_W:] * (HEAD_DIM ** -0.5)).astype(BF16)
    hs = h[:, :RWKV_SHIFT_W]
    row = _iota(hs.shape, 0)
    shifted = jnp.where(row == 0, prev_ref[...], pltpu.roll(hs, 1, 0))
    prev_ref[...] = hs[tm - 1:tm, :]
    hs = hs + mu_ref[...] * (shifted - hs)
    r = hs[:, :TOK_W]
    k = hs[:, TOK_W:2 * TOK_W]
    v = hs[:, 2 * TOK_W:3 * TOK_W]
    wa = hs[:, 3 * TOK_W:3 * TOK_W + LANES]
    gd = hs[:, 3 * TOK_W + LANES:]
    w = w0_ref[...] + _dot(jnp.tanh(wa).astype(BF16), w2_ref[...])
    w = -(jnp.maximum(-w, 0.0) + jnp.log(1.0 + jnp.exp(-jnp.abs(w)))) - 0.5
    log_decay = -jnp.exp(w)
    a = _sigmoid(a0_ref[...] + _dot(wa.astype(BF16), a2_ref[...]))
    g_ref[0] = _dot(_sigmoid(gd).astype(BF16), g2_ref[...])
    kk = k * kk_ref[...]
    kk = kk / jnp.maximum(jnp.sqrt(_head_sums(kk * kk, head_ones_ref[...])), 1e-12)
    k = k * (1.0 + (a - 1.0) * ka_ref[...])
    bonus_ref[0] = r * k * rk_ref[...]
    v_ref[0] = v
    a_s = -kk
    b_s = kk * a
    same_chunk = _div_pow2(_iota((tm, tm), 0), CHUNK) == _div_pow2(_iota((tm, tm), 1), CHUNK)
    incl = (same_chunk & (_iota((tm, tm), 1) <= _iota((tm, tm), 0))).astype(BF16)
    d_hi, d_mid, d_lo = _split3(log_decay)
    cw = _dot(incl, d_hi) + _dot(incl, d_mid) + _dot(incl, d_lo)
    same = same_chunk.astype(BF16)
    cw_end = _dot(same, d_hi) + _dot(same, d_mid) + _dot(same, d_lo)
    e_in = jnp.exp(cw)
    e_out = jnp.exp(-cw)
    e_end = jnp.exp(cw_end - cw)
    rt_ref[0] = (r * e_in).astype(BF16)
    at_ref[0] = (a_s * jnp.exp(cw - log_decay)).astype(BF16)
    bt_ref[0] = (b_s * e_out).astype(BF16)
    kt_ref[0] = (k * e_out).astype(BF16)
    bh_ref[0] = (b_s * e_end).astype(BF16)
    kh_ref[0] = (k * e_end).astype(BF16)
    pc_ref[0] = jnp.exp(cw_end)


def _rwkv_in(x, w_in, mu, w0, w2, a0, a2, g2, k_k, k_a, r_k, tm):
    B, S, _ = x.shape
    in_w = w_in.shape[1]
    w2_pad = jnp.concatenate([w2, jnp.zeros((AAA_LORA, TOK_W), F32)], axis=0).astype(BF16)
    a2_pad = jnp.concatenate([jnp.zeros((DECAY_LORA, TOK_W), F32), a2], axis=0).astype(BF16)
    head_ones = _head_blocks(1.0)
    vec = lambda t: t.reshape(1, -1)
    row_spec = lambda w: pl.BlockSpec((1, tm, w), lambda b, i: (b, i, 0))
    tok = jax.ShapeDtypeStruct((B, S, TOK_W), F32)
    return pl.pallas_call(
        _rwkv_in_kernel,
        grid=(B, S // tm),
        in_specs=[row_spec(D_MODEL), _full_spec((D_MODEL, in_w)), _full_spec((1, RWKV_SHIFT_W)),
                  _full_spec((1, TOK_W)), _full_spec((LANES, TOK_W)), _full_spec((1, TOK_W)),
                  _full_spec((LANES, TOK_W)), _full_spec((GATE_LORA, TOK_W)), _full_spec((1, TOK_W)),
                  _full_spec((1, TOK_W)), _full_spec((1, TOK_W)), _full_spec((MXU_W, MXU_W))],
        out_specs=[row_spec(TOK_W)] * 10 + [row_spec(MEM_W)],
        out_shape=[jax.ShapeDtypeStruct((B, S, TOK_W), BF16)] * 6 + [tok] * 4
        + [jax.ShapeDtypeStruct((B, S, MEM_W), BF16)],
        scratch_shapes=[pltpu.VMEM((1, RWKV_SHIFT_W), F32)],
        compiler_params=_cparams(("arbitrary", "arbitrary")),
        name="rwkv_in",
    )(x, w_in.astype(BF16), vec(mu), vec(w0), w2_pad, vec(a0), a2_pad, g2.astype(BF16), vec(k_k),
      vec(k_a), vec(r_k), head_ones)


def _rwkv_scan_kernel(rt_ref, at_ref, bt_ref, kt_ref, bh_ref, kh_ref, v_ref, pc_ref, y_ref, state_ref,
                      *, n_chunks):
    @pl.when(pl.program_id(2) == 0)
    def _():
        state_ref[...] = jnp.zeros_like(state_ref)

    C = CHUNK
    lane_head = _div_pow2(_iota((2 * C, LANES), 1), HEAD_DIM)
    row_head = _div_pow2(_iota((2 * C, LANES), 0), C)
    keep = lane_head == row_head
    pos_r = _mod_pow2(_iota((2 * C, 2 * C), 0), C)
    pos_c = _mod_pow2(_iota((2 * C, 2 * C), 1), C)
    strict = pos_c < pos_r
    incl = pos_c <= pos_r
    eye = (_iota((2 * C, 2 * C), 0) == _iota((2 * C, 2 * C), 1)).astype(F32)

    def stack(ref, item):
        tile, c = item
        t = ref[0, c * C:(c + 1) * C, tile * LANES:(tile + 1) * LANES]
        return jnp.where(keep, jnp.concatenate([t, t], axis=0), 0.0).astype(BF16)

    items = [(tile, c) for c in range(n_chunks) for tile in range(SCAN_TILES)]
    chunks = range(len(items))
    a2 = [stack(at_ref, it) for it in items]
    r2 = [stack(rt_ref, it) for it in items]
    bh2 = [stack(bh_ref, it) for it in items]
    kh2 = [stack(kh_ref, it) for it in items]
    v2 = [stack(v_ref, it) for it in items]
    gram = [_dot_nt(jnp.concatenate([a2[c], r2[c]], axis=0),
                    jnp.concatenate([stack(bt_ref, items[c]), stack(kt_ref, items[c])], axis=0))
            for c in chunks]
    power = [jnp.where(strict, gram[c][:2 * C, :2 * C], 0.0) for c in chunks]
    inv = [eye + power[c] for c in chunks]
    for _ in range(int(math.log2(C)) - 1):
        power = [_dot(p.astype(BF16), p.astype(BF16)) for p in power]
        inv = [inv[c] + _dot(inv[c].astype(BF16), power[c].astype(BF16)) for c in chunks]
    from_v = [_dot(jnp.concatenate([jnp.where(strict, gram[c][:2 * C, 2 * C:], 0.0),
                                    jnp.where(incl, gram[c][2 * C:, 2 * C:], 0.0)], axis=0).astype(BF16),
                   v2[c]) for c in chunks]
    wu = [_dot(inv[c].astype(BF16),
               jnp.concatenate([a2[c], from_v[c][:2 * C].astype(BF16)], axis=1)) for c in chunks]
    wy = [_dot(jnp.where(incl, gram[c][2 * C:, :2 * C], 0.0).astype(BF16), wu[c].astype(BF16))
          for c in chunks]
    w_y = [(r2[c].astype(F32) + wy[c][:, :LANES]).astype(BF16) for c in chunks]
    y0 = [from_v[c][2 * C:] + wy[c][:, LANES:] for c in chunks]
    gc = [_dot(wu[c].T.astype(BF16), bh2[c]) for c in chunks]
    c0 = [gc[c][LANES:] + _dot(v2[c].astype(F32).T.astype(BF16), kh2[c]) for c in chunks]
    states = [state_ref[tile] for tile in range(SCAN_TILES)]
    for i, (tile, c) in enumerate(items):
        lanes = slice(tile * LANES, (tile + 1) * LANES)
        sb = states[tile].astype(BF16)
        y2 = _dot_nt(w_y[i], sb) + y0[i]
        y_ref[0, c * C:(c + 1) * C, lanes] = y2[:C] + y2[C:]
        states[tile] = (states[tile] * pc_ref[0, c * C:c * C + 1, lanes]
                        + _dot(sb, gc[i][:LANES].astype(BF16)) + c0[i])
    for tile in range(SCAN_TILES):
        state_ref[tile] = states[tile]


def _rwkv_scan(rt, at, bt, kt, bh, kh, v, pc, rows):
    B, S, _ = v.shape
    spec = pl.BlockSpec((1, rows, SCAN_TILES * LANES), lambda b, p, i: (b, i, p))
    return pl.pallas_call(
        functools.partial(_rwkv_scan_kernel, n_chunks=rows // CHUNK),
        grid=(B, N_HEAD_PAIRS // SCAN_TILES, S // rows),
        in_specs=[spec] * 8,
        out_specs=spec,
        out_shape=jax.ShapeDtypeStruct((B, S, TOK_W), F32),
        scratch_shapes=[pltpu.VMEM((SCAN_TILES, LANES, LANES), F32)],
        compiler_params=_cparams(("arbitrary", "arbitrary", "arbitrary")),
        name="rwkv_scan",
    )(rt, at, bt, kt, bh, kh, v, pc)


def _rwkv_post_kernel(y_ref, bonus_ref, v_ref, g_ref, lg_ref, lb_ref, head_mean_ref, o_ref):
    y = y_ref[0]
    head_mean = functools.partial(_head_sums, blocks=head_mean_ref[...])

    yc = y - head_mean(y)
    var = head_mean(yc * yc)
    yn = yc * lax.rsqrt(var + GN_EPS) * lg_ref[...] + lb_ref[...]
    bonus = head_mean(bonus_ref[0]) * float(HEAD_DIM)
    o_ref[0] = ((yn + bonus * v_ref[0]) * g_ref[0]).astype(BF16)


def _rwkv_post(y, bonus, v, g, lnx_g, lnx_b, tm):
    B, S, _ = y.shape
    head_mean = _head_blocks(1.0 / HEAD_DIM)
    row_spec = pl.BlockSpec((1, tm, TOK_W), lambda b, i: (b, i, 0))
    return pl.pallas_call(
        _rwkv_post_kernel,
        grid=(B, S // tm),
        in_specs=[row_spec] * 4 + [_full_spec((1, TOK_W)), _full_spec((1, TOK_W)),
                                   _full_spec((MXU_W, MXU_W))],
        out_specs=row_spec,
        out_shape=jax.ShapeDtypeStruct((B, S, TOK_W), BF16),
        compiler_params=_cparams(("arbitrary", "arbitrary")),
        name="rwkv_post",
    )(y, bonus, v, g, lnx_g.reshape(1, -1), lnx_b.reshape(1, -1), head_mean)


def _mix_out_kernel(tok_ref, qm_ref, km_ref, vm_ref, x_ref, wo_tok_ref, wo_mem_ref, g_ref, b_ref,
                    wr_hi_ref, wr_lo_ref, rb_ref, x1_ref, x1b_ref, gate_ref):
    tm = x_ref.shape[1]
    groups = [slice(r, r + tm // MIX_GROUPS) for r in range(0, tm, tm // MIX_GROUPS)]
    mem_out = [None] * MIX_GROUPS
    for h in range(N_MEM_HEADS):
        for i, rows in enumerate(groups):
            s = _dot_nt(qm_ref[0, rows, :], km_ref[0, h])
            e = jnp.exp(s - jnp.max(s, axis=1, keepdims=True))
            o = _dot(e.astype(BF16), vm_ref[0, h]) / jnp.sum(e, axis=1, keepdims=True)
            mem_out[i] = o if mem_out[i] is None else mem_out[i] + o
    x1 = []
    for i, rows in enumerate(groups):
        mixed = _dot(tok_ref[0, rows, :], wo_tok_ref[...]) + _dot(mem_out[i].astype(BF16), wo_mem_ref[...])
        x1.append(_layer_norm(DEEPNORM_ALPHA * x_ref[0, rows, :] + mixed, g_ref[...], b_ref[...]))
        x1_ref[0, rows, :] = x1[i]
        x1b_ref[0, rows, :] = x1[i].astype(BF16)
    scores = [_sigmoid(_dot3(x1[i], wr_hi_ref[...], wr_lo_ref[...])) for i in range(MIX_GROUPS)]
    lane = _iota(scores[0].shape, 1)
    lane_f = lane.astype(F32)
    cand = [jnp.where(lane < N_EXPERTS, sc + rb_ref[...], -jnp.inf) for sc in scores]
    picked = [jnp.zeros(scores[0].shape, jnp.bool_)] * MIX_GROUPS
    for _ in range(TOP_K):
        for i in range(MIX_GROUPS):
            best = jnp.max(cand[i], axis=1, keepdims=True)
            first = jnp.min(jnp.where(cand[i] == best, lane_f, float(LANES)), axis=1, keepdims=True)
            hit = lane_f == first
            picked[i] = jnp.logical_or(picked[i], hit)
            cand[i] = jnp.where(hit, -jnp.inf, cand[i])
    for i, rows in enumerate(groups):
        chosen = jnp.where(picked[i], scores[i], 0.0)
        gate = chosen / jnp.sum(chosen, axis=1, keepdims=True) * ROUTED_SCALE
        gate_ref[0, rows, :] = jnp.where(lane == N_EXPERTS, 1.0, gate)


def _mix_out(tok, qm, km, vm, x, w_out, ln_g, ln_b, w_router, router_bias, tm):
    B, S, _ = x.shape
    n_mem = km.shape[2]
    wr = jnp.pad(w_router, ((0, 0), (0, LANES - N_EXPERTS)))
    wr_hi = wr.astype(BF16)
    wr_lo = (wr - wr_hi.astype(F32)).astype(BF16)
    rb = jnp.pad(router_bias, (0, LANES - N_EXPERTS)).reshape(1, LANES)
    row_spec = lambda w: pl.BlockSpec((1, tm, w), lambda b, i: (b, i, 0))
    mem_spec = pl.BlockSpec((1, N_MEM_HEADS, n_mem, MEM_W), lambda b, i: (b, 0, 0, 0))
    return pl.pallas_call(
        _mix_out_kernel,
        grid=(B, S // tm),
        in_specs=[row_spec(TOK_W), row_spec(MEM_W), mem_spec, mem_spec, row_spec(D_MODEL),
                  _full_spec((TOK_W, D_MODEL)), _full_spec((MEM_W, D_MODEL)),
                  _full_spec((1, D_MODEL)), _full_spec((1, D_MODEL)),
                  _full_spec((D_MODEL, LANES)), _full_spec((D_MODEL, LANES)), _full_spec((1, LANES))],
        out_specs=[row_spec(D_MODEL), row_spec(D_MODEL), row_spec(LANES)],
        out_shape=[jax.ShapeDtypeStruct((B, S, D_MODEL), F32), jax.ShapeDtypeStruct((B, S, D_MODEL), BF16),
                   jax.ShapeDtypeStruct((B, S, LANES), F32)],
        compiler_params=_cparams(("arbitrary", "arbitrary")),
        name="mix_out",
    )(tok, qm, km, vm, x, w_out[:TOK_W].astype(BF16), w_out[TOK_W:].astype(BF16),
      ln_g.reshape(1, -1), ln_b.reshape(1, -1), wr_hi, wr_lo, rb)


MOE_TM = 512
UNIT = 16
SLAB_ROWS = 4096
SLAB_UNITS = SLAB_ROWS // UNIT
BLK_UNITS = 64
BLK_ROWS = BLK_UNITS * UNIT
XS_W = D_MODEL + 2 * LANES
ROW_CHUNK = 1024
BF16_EXACT_INT = 256
assert TOP_K * MOE_TM + N_EXPERTS * (UNIT - 1) <= SLAB_ROWS - UNIT
ZERO_UNIT = SLAB_UNITS - 1


def _dot_x_sel(x, sel):
    hi, mid, lo = _split3(x)
    return _dot(hi, sel) + _dot(mid, sel) + _dot(lo, sel)


def _route_tile(gates):
    tm = gates.shape[0]
    sel = jnp.logical_and(gates > 0.0, _iota(gates.shape, 1) < N_EXPERTS)
    sel_b = jnp.where(sel, 1.0, 0.0).astype(BF16)
    earlier = (_iota((tm, tm), 1) < _iota((tm, tm), 0)).astype(BF16)
    rank1 = jnp.where(sel, _dot(earlier, sel_b) + 1.0, 0.0)
    count = _dot(jnp.ones((8, tm), BF16), sel_b)[0:1]
    n16 = jnp.floor((count + float(UNIT - 1)) * (1.0 / UNIT))
    before = (_iota((LANES, LANES), 0) < _iota((LANES, LANES), 1)).astype(BF16)
    off16 = _dot(jnp.broadcast_to(n16, (8, LANES)).astype(BF16), before)[0:1]
    r_hi, r_lo = _split2(rank1)
    return r_hi, r_lo, jnp.max(count) > float(BF16_EXACT_INT), n16, off16


def _slab_experts(n16, off16, r0, rows):
    r = (_iota((rows, LANES), 0) + r0).astype(F32)
    lo = off16 * float(UNIT)
    return jnp.logical_and(r >= lo, r < lo + n16 * float(UNIT))


def _moe_dispatch_kernel(xb_ref, gate_ref, xs_ref, n_ref):
    gates = gate_ref[...]
    r_hi, r_lo, has_lo, n16, off16 = _route_tile(gates)
    n_ref[0] = jnp.broadcast_to(n16, (8, LANES))
    g_hi, g_lo = _split2(gates)
    src = jnp.concatenate([xb_ref[...], g_hi, g_lo], axis=1)

    def fill_slab(with_lo):
        for c in range(SLAB_ROWS // ROW_CHUNK):
            r0 = c * ROW_CHUNK
            seg = _slab_experts(n16, off16, r0, ROW_CHUNK)
            seg_b = jnp.where(seg, 1.0, 0.0).astype(BF16)
            rank_at = _dot_nt(seg_b, r_hi)
            if with_lo:
                rank_at = rank_at + _dot_nt(seg_b, r_lo)
            r = (_iota((ROW_CHUNK, 1), 0) + r0).astype(F32)
            pos1 = r + 1.0 - jnp.sum(jnp.where(seg, off16 * float(UNIT), 0.0), axis=1, keepdims=True)
            pick = jnp.where(rank_at == pos1, 1.0, 0.0).astype(BF16)
            xs_ref[r0:r0 + ROW_CHUNK, :] = _dot(pick, src).astype(BF16)

    lax.cond(has_lo, functools.partial(fill_slab, True), functools.partial(fill_slab, False))


def _moe_dispatch(xb, gates):
    T = xb.shape[0]
    nt = T // MOE_TM
    return pl.pallas_call(
        _moe_dispatch_kernel,
        grid=(nt,),
        in_specs=[pl.BlockSpec((MOE_TM, D_MODEL), lambda i: (i, 0)),
                  pl.BlockSpec((MOE_TM, LANES), lambda i: (i, 0))],
        out_specs=[pl.BlockSpec((SLAB_ROWS, XS_W), lambda i: (i, 0)),
                   pl.BlockSpec((1, 8, LANES), lambda i: (i, 0, 0))],
        out_shape=[jax.ShapeDtypeStruct((nt * SLAB_ROWS, XS_W), BF16),
                   jax.ShapeDtypeStruct((nt, 8, LANES), F32)],
        compiler_params=_cparams(("arbitrary",)),
        name="moe_dispatch",
    )(xb, gates)


def _moe_plan_kernel(n_ref, unit_ref, expert_ref, count_ref, *, nt, nb):
    n16 = n_ref[...]
    lane = _iota((nt, LANES), 1)
    before = (_iota((LANES, LANES), 0) < _iota((LANES, LANES), 1)).astype(BF16)
    off16 = _dot(n16.astype(BF16), before)
    used = jnp.sum(n16, axis=1, keepdims=True)
    n16 = n16 + jnp.where(lane == N_EXPERTS - 1, float(SLAB_UNITS) - used, 0.0)
    earlier = (_iota((nt, nt), 1) < _iota((nt, nt), 0)).astype(BF16)
    cum_ex = _dot_sel(earlier, n16)
    cum_in = cum_ex + n16
    total = cum_in[nt - 1:nt, :]
    n_blk = jnp.floor((total + float(BLK_UNITS - 1)) * (1.0 / BLK_UNITS))
    blk_start = _dot_x_sel(jnp.broadcast_to(n_blk, (8, LANES)), before)[0:1]
    blk_end = blk_start + n_blk
    count_ref[...] = jnp.broadcast_to(jnp.sum(n_blk, axis=1, keepdims=True), (8, LANES)).astype(jnp.int32)

    lane_b = _iota((nb, LANES), 1)
    blk = _iota((nb, LANES), 0).astype(F32)
    expert = jnp.sum(jnp.where(jnp.logical_and(lane_b < N_EXPERTS, blk_end <= blk), 1.0, 0.0),
                     axis=1, keepdims=True)
    live = expert < float(N_EXPERTS)
    expert = jnp.minimum(expert, float(N_EXPERTS - 1))
    expert_ref[...] = jnp.broadcast_to(expert, (nb, LANES)).astype(jnp.int32)
    mine = lane_b.astype(F32) == expert
    mine_b = jnp.where(mine, 1.0, 0.0).astype(BF16)
    my_start = jnp.sum(jnp.where(mine, blk_start, 0.0), axis=1, keepdims=True)
    my_total = jnp.sum(jnp.where(mine, total, 0.0), axis=1, keepdims=True)
    q = (blk - my_start) * float(BLK_UNITS) + lane_b.astype(F32)
    valid = jnp.logical_and(jnp.logical_and(q < my_total, lane_b < BLK_UNITS), live)

    def per_slab(table):
        hi, mid, lo = _split3(table)
        return _dot_nt(mine_b, hi) + _dot_nt(mine_b, mid) + _dot_nt(mine_b, lo)

    ex, inc, first = per_slab(cum_ex), per_slab(cum_in), per_slab(off16)
    base = jnp.zeros((nb, LANES), F32)
    for i in range(nt):
        hit = jnp.logical_and(q >= ex[:, i:i + 1], q < inc[:, i:i + 1])
        base = base + jnp.where(hit, float(i * SLAB_UNITS) + first[:, i:i + 1] - ex[:, i:i + 1], 0.0)
    unit_ref[...] = jnp.where(valid, base + q, -1.0).astype(jnp.int32)


def _moe_plan(n16, nb):
    nt = n16.shape[0]
    tbl = jax.ShapeDtypeStruct((nb, LANES), jnp.int32)
    units, experts, count = pl.pallas_call(
        functools.partial(_moe_plan_kernel, nt=nt, nb=nb),
        out_shape=[tbl, tbl, jax.ShapeDtypeStruct((8, LANES), jnp.int32)],
        compiler_params=pltpu.CompilerParams(vmem_limit_bytes=VMEM_LIMIT),
        name="moe_plan",
    )(n16)
    return units[:, :BLK_UNITS].reshape(-1), experts[:, 0], count[0, :1]


def _moe_ffn_kernel(unit_ref, expert_ref, count_ref, xs_hbm, wg_ref, wu_ref, wd_ref, ys_hbm,
                    ibuf, obuf, in_sem, out_sem, *, nb):
    b = pl.program_id(0)
    n_live = count_ref[0]
    slot = jnp.bitwise_and(b, 1)

    def in_copy(blk, buf, s):
        u = unit_ref[blk * BLK_UNITS + s]
        u = jnp.where(u < 0, ZERO_UNIT, u)
        return pltpu.make_async_copy(xs_hbm.at[pl.ds(pl.multiple_of(u * UNIT, UNIT), UNIT), :],
                                     ibuf.at[buf, pl.ds(s * UNIT, UNIT), :], in_sem.at[buf])

    def out_copy(blk, buf, s):
        u = unit_ref[blk * BLK_UNITS + s]
        return u, pltpu.make_async_copy(
            obuf.at[buf, pl.ds(s * UNIT, UNIT), :],
            ys_hbm.at[pl.ds(pl.multiple_of(jnp.maximum(u, 0) * UNIT, UNIT), UNIT), :], out_sem.at[buf])

    def whole_block(blk):
        return unit_ref[blk * BLK_UNITS + BLK_UNITS - 1] >= 0

    def for_each_used_out(blk, buf, act):
        for s in range(BLK_UNITS):
            u, copy = out_copy(blk, buf, s)
            pl.when(u >= 0)(functools.partial(act, copy))

    def wait_out(blk, buf):
        @pl.when(whole_block(blk))
        def _():
            pltpu.make_async_copy(obuf.at[buf], ys_hbm.at[pl.ds(0, BLK_ROWS), :], out_sem.at[buf]).wait()

        @pl.when(jnp.logical_not(whole_block(blk)))
        def _():
            for_each_used_out(blk, buf, lambda c: c.wait())

    def start_in(blk, buf):
        for s in range(BLK_UNITS):
            in_copy(blk, buf, s).start()

    @pl.when(b == 0)
    def _():
        start_in(0, 0)

    @pl.when(b <= n_live)
    def _():
        pltpu.make_async_copy(xs_hbm.at[pl.ds(0, BLK_ROWS), :], ibuf.at[slot], in_sem.at[slot]).wait()

    @pl.when(jnp.logical_and(b >= 2, b - 2 < n_live))
    def _():
        wait_out(b - 2, slot)

    def expert_rows(rows):
        x = rows[:, :D_MODEL]
        gates = rows[:, D_MODEL:D_MODEL + LANES].astype(F32) + rows[:, D_MODEL + LANES:].astype(F32)
        gate = jnp.sum(jnp.where(_iota(gates.shape, 1) == expert_ref[b], gates, 0.0), axis=1, keepdims=True)
        hg = _dot(x, wg_ref[0, 0].astype(BF16))
        hu = _dot(x, wu_ref[0, 0].astype(BF16))
        hidden = hg * _sigmoid(hg) * hu * gate
        return _dot(hidden.astype(BF16), wd_ref[0, 0].astype(BF16)).astype(BF16)

    live = b < n_live

    @pl.when(jnp.logical_and(live, whole_block(b)))
    def _():
        start_in(b + 1, 1 - slot)
        half = BLK_UNITS // 2
        for part in range(2):
            rows = pl.ds(part * half * UNIT, half * UNIT)
            obuf[slot, rows, :] = expert_rows(ibuf[slot, rows, :])
            for s in range(part * half, (part + 1) * half):
                out_copy(b, slot, s)[1].start()

    @pl.when(jnp.logical_and(live, jnp.logical_not(whole_block(b))))
    def _():
        start_in(b + 1, 1 - slot)
        obuf[slot] = expert_rows(ibuf[slot])
        for_each_used_out(b, slot, lambda c: c.start())

    @pl.when(jnp.logical_and(b == nb - 1, nb - 2 < n_live))
    def _():
        wait_out(nb - 2, 1 - slot)


def _moe_ffn(xs, units, experts, count, w_gate, w_up, w_down, layer, nb):
    n_rows = xs.shape[0]
    w_spec = lambda r, c: pl.BlockSpec((1, 1, r, c), lambda b, u, e, n: (layer, e[b], 0, 0))
    grid_spec = pltpu.PrefetchScalarGridSpec(
        num_scalar_prefetch=3,
        grid=(nb,),
        in_specs=[pl.BlockSpec(memory_space=pl.ANY),
                  w_spec(D_MODEL, D_EXPERT), w_spec(D_MODEL, D_EXPERT), w_spec(D_EXPERT, D_MODEL)],
        out_specs=pl.BlockSpec(memory_space=pl.ANY),
        scratch_shapes=[pltpu.VMEM((2, BLK_ROWS, XS_W), BF16), pltpu.VMEM((2, BLK_ROWS, D_MODEL), BF16),
                        pltpu.SemaphoreType.DMA((2,)), pltpu.SemaphoreType.DMA((2,))],
    )
    return pl.pallas_call(
        functools.partial(_moe_ffn_kernel, nb=nb),
        grid_spec=grid_spec,
        out_shape=jax.ShapeDtypeStruct((n_rows, D_MODEL), BF16),
        compiler_params=_cparams(("arbitrary",)),
        name="moe_ffn",
    )(units, experts, count, xs, w_gate, w_up, w_down)


def _moe_combine_kernel(ys_ref, gate_ref, x_ref, xb_ref, wsg_ref, wsu_ref, wsd_ref, g_ref, b_ref, o_ref):
    r_hi, r_lo, has_lo, n16, off16 = _route_tile(gate_ref[...])
    off_b = jnp.broadcast_to(off16, (8, LANES)).astype(BF16)

    def finish(with_lo):
        xb = xb_ref[...]
        hg = _dot(xb, wsg_ref[...])
        acc = _dot((hg * _sigmoid(hg) * _dot(xb, wsu_ref[...])).astype(BF16), wsd_ref[...])
        for c in range(SLAB_ROWS // ROW_CHUNK):
            r0 = c * ROW_CHUNK
            seg_b = jnp.where(_slab_experts(n16, off16, r0, ROW_CHUNK), 1.0, 0.0).astype(BF16)
            rank_at = _dot_nt(r_hi, seg_b)
            if with_lo:
                rank_at = rank_at + _dot_nt(r_lo, seg_b)
            r = (_iota((1, ROW_CHUNK), 1) + r0).astype(F32)
            pos1 = r + 1.0 - _dot_nt(off_b, seg_b)[0:1] * float(UNIT)
            pick = jnp.where(rank_at == pos1, 1.0, 0.0).astype(BF16)
            acc = acc + _dot(pick, ys_ref[r0:r0 + ROW_CHUNK, :])
        o_ref[...] = _layer_norm(DEEPNORM_ALPHA * x_ref[...] + acc, g_ref[...], b_ref[...])

    lax.cond(has_lo, functools.partial(finish, True), functools.partial(finish, False))


def _moe_combine(ys, gates, x, xb, w_sh_gate, w_sh_up, w_sh_down, ln_g, ln_b):
    T = x.shape[0]
    row_spec = lambda w: pl.BlockSpec((MOE_TM, w), lambda i: (i, 0))
    return pl.pallas_call(
        _moe_combine_kernel,
        grid=(T // MOE_TM,),
        in_specs=[pl.BlockSpec((SLAB_ROWS, D_MODEL), lambda i: (i, 0)),
                  row_spec(LANES), row_spec(D_MODEL), row_spec(D_MODEL),
                  _full_spec((D_MODEL, D_EXPERT)), _full_spec((D_MODEL, D_EXPERT)),
                  _full_spec((D_EXPERT, D_MODEL)), _full_spec((1, D_MODEL)), _full_spec((1, D_MODEL))],
        out_specs=row_spec(D_MODEL),
        out_shape=jax.ShapeDtypeStruct((T, D_MODEL), F32),
        compiler_params=_cparams(("arbitrary",)),
        name="moe_combine",
    )(ys, gates, x, xb, w_sh_gate.astype(BF16), w_sh_up.astype(BF16), w_sh_down.astype(BF16),
      ln_g.reshape(1, -1), ln_b.reshape(1, -1))


def _moe(xb, x, gates, w_gate, w_up, w_down, layer, w_sh_gate, w_sh_up, w_sh_down, ln_g, ln_b):
    T = x.shape[0]
    nt = T // MOE_TM
    nb = nt * SLAB_UNITS // BLK_UNITS + N_EXPERTS + 1
    xs, n16 = _moe_dispatch(xb, gates)
    units, experts, count = _moe_plan(n16[:, 0, :], nb)
    ys = _moe_ffn(xs, units, experts, count, w_gate, w_up, w_down, layer, nb)
    return _moe_combine(ys, gates, x, xb, w_sh_gate, w_sh_up, w_sh_down, ln_g, ln_b)


def _tile(n, want):
    t = min(n, want)
    assert n % t == 0, (n, t)
    return t


def kernel(x, mem, mem_ln_g, mem_ln_b, w_mem_kv, fox_w_in, fox_b_f, rwkv_w_in, rwkv_mu, rwkv_w0, rwkv_w2,
           rwkv_a0, rwkv_a2, rwkv_g2, rwkv_k_k, rwkv_k_a, rwkv_r_k, rwkv_lnx_g, rwkv_lnx_b, w_out, ln1_g,
           ln1_b, w_router, router_bias, w_exp_gate, w_exp_up, w_exp_down, w_sh_gate, w_sh_up, w_sh_down,
           ln2_g, ln2_b):
    B, S, D = x.shape
    T = B * S
    assert D == D_MODEL and S % CHUNK == 0 and T % MOE_TM == 0
    t_proj = _tile(S, 512)
    t_attn = _tile(S, 512)
    t_rwkv = _tile(S, 256)
    t_scan = _tile(S, 512)

    km, vm = _mem_kv(mem, mem_ln_g, mem_ln_b, w_mem_kv)
    for i in range(DEPTH):
        j = i // 2
        if i % 2 == 0:
            q, k, kc, v, g, qm = _fox_in(x, fox_w_in[j], fox_b_f[j], t_proj)
            tok = _fox_attn(q, k, kc, v, g, t_attn)
        else:
            (rt, at, bt, kt, bh, kh, v, pc, bonus, g, qm) = _rwkv_in(
                x, rwkv_w_in[j], rwkv_mu[j], rwkv_w0[j], rwkv_w2[j], rwkv_a0[j], rwkv_a2[j], rwkv_g2[j],
                rwkv_k_k[j], rwkv_k_a[j], rwkv_r_k[j].reshape(-1), t_rwkv)
            y = _rwkv_scan(rt, at, bt, kt, bh, kh, v, pc, t_scan)
            tok = _rwkv_post(y, bonus, v, g, rwkv_lnx_g[j], rwkv_lnx_b[j], t_proj)
        x1, x1b, gates = _mix_out(tok, qm, km, vm, x, w_out[i], ln1_g[i], ln1_b[i], w_router[i],
                                  router_bias[i], t_proj)
        x = _moe(x1b.reshape(T, D), x1.reshape(T, D), gates.reshape(T, LANES), w_exp_gate, w_exp_up,
                 w_exp_down, i, w_sh_gate[i], w_sh_up[i], w_sh_down[i], ln2_g[i], ln2_b[i]).reshape(B, S, D)
    return x
```

```python
import functools
import math

import jax
import jax.numpy as jnp
from jax import lax
from jax.experimental import pallas as pl
from jax.experimental.pallas import tpu as pltpu

F32 = jnp.float32
BF16 = jnp.bfloat16

D_MODEL = 1024
HEAD_DIM = 64
N_TOK_HEADS = 12
TOK_W = N_TOK_HEADS * HEAD_DIM
N_HEAD_PAIRS = N_TOK_HEADS // 2
N_MEM_HEADS = 4
MEM_W = N_MEM_HEADS * HEAD_DIM
DECAY_LORA = 64
AAA_LORA = 64
GATE_LORA = 128
RWKV_SHIFT_W = 3 * TOK_W + DECAY_LORA + AAA_LORA + GATE_LORA
N_EXPERTS = 64
TOP_K = 6
D_EXPERT = 256
ROUTED_SCALE = 2.5
DEPTH = 2
DEEPNORM_ALPHA = (2 * DEPTH) ** 0.25
LN_EPS = 1e-5
GN_EPS = 64e-5
LOG2E = math.log2(math.e)

LANES = 128
C_PARTS = 3
CHUNK = 64
SCAN_TILES = 6
MIX_GROUPS = 2
VMEM_LIMIT = 56 * 1024 * 1024


def _cparams(sem):
    return pltpu.CompilerParams(dimension_semantics=sem, vmem_limit_bytes=VMEM_LIMIT)


def _dot(a, b):
    return jnp.dot(a, b, preferred_element_type=F32)


def _dot_nt(a, b):
    return lax.dot_general(a, b, (((1,), (1,)), ((), ())), preferred_element_type=F32)


def _split2(x):
    hi = x.astype(BF16)
    lo = (x - hi.astype(F32)).astype(BF16)
    return hi, lo


def _split3(x):
    hi = x.astype(BF16)
    r1 = x - hi.astype(F32)
    mid = r1.astype(BF16)
    lo = (r1 - mid.astype(F32)).astype(BF16)
    return hi, mid, lo


def _dot_sel(sel, x):
    hi, mid, lo = _split3(x)
    return _dot(sel, hi) + _dot(sel, mid) + _dot(sel, lo)


def _dot3(x, w_hi, w_lo):
    x_hi, x_lo = _split2(x)
    return _dot(x_hi, w_hi) + _dot(x_lo, w_hi) + _dot(x_hi, w_lo)


MXU_W = 256


def _head_blocks(value):
    head = jnp.arange(MXU_W) // HEAD_DIM
    return jnp.where(head[:, None] == head[None, :], value, 0.0).astype(BF16)


def _head_sums(t, blocks):
    cols = []
    for c in range(0, t.shape[1], MXU_W):
        hi, lo = _split2(t[:, c:c + MXU_W])
        cols.append(_dot(hi, blocks) + _dot(lo, blocks))
    return jnp.concatenate(cols, axis=1)


def _sigmoid(x):
    return 1.0 / (1.0 + jnp.exp(-x))


def _layer_norm(y, g, b):
    mu = jnp.mean(y, axis=-1, keepdims=True)
    yc = y - mu
    var = jnp.mean(yc * yc, axis=-1, keepdims=True)
    return yc * lax.rsqrt(var + LN_EPS) * g + b


def _iota(shape, dim):
    return lax.broadcasted_iota(jnp.int32, shape, dim)


def _div_pow2(x, n):
    return jnp.right_shift(x, int(math.log2(n)))


def _mod_pow2(x, n):
    return jnp.bitwise_and(x, n - 1)


def _full_spec(shape):
    n = len(shape)
    return pl.BlockSpec(shape, lambda *_: (0,) * n)


def _mem_kv_kernel(mem_ref, g_ref, b_ref, w_ref, k_ref, v_ref):
    m = _layer_norm(mem_ref[0], g_ref[...], b_ref[...])
    kv = _dot(m.astype(BF16), w_ref[...])
    k = kv[:, :MEM_W]
    v = kv[:, MEM_W:]
    head = _div_pow2(_iota(k.shape, 1), HEAD_DIM)
    for h in range(N_MEM_HEADS):
        k_ref[0, h] = jnp.where(head == h, k, 0.0).astype(BF16)
        v_ref[0, h] = jnp.where(head == h, v, 0.0).astype(BF16)


def _mem_kv(mem, g, b, w):
    B, n_mem, _ = mem.shape
    out = jax.ShapeDtypeStruct((B, N_MEM_HEADS, n_mem, MEM_W), BF16)
    out_spec = pl.BlockSpec((1, N_MEM_HEADS, n_mem, MEM_W), lambda i: (i, 0, 0, 0))
    return pl.pallas_call(
        _mem_kv_kernel,
        grid=(B,),
        in_specs=[pl.BlockSpec((1, n_mem, D_MODEL), lambda i: (i, 0, 0)),
                  _full_spec((1, D_MODEL)), _full_spec((1, D_MODEL)),
                  _full_spec((D_MODEL, 2 * MEM_W))],
        out_specs=[out_spec, out_spec],
        out_shape=[out, out],
        compiler_params=_cparams(("arbitrary",)),
        name="mem_kv",
    )(mem, g.reshape(1, -1), b.reshape(1, -1), w.astype(BF16))


def _fox_in_kernel(x_ref, wq_ref, wk_ref, wv_ref, wg_ref, wfh_ref, wfl_ref, wm_ref, bf_ref,
                   place_ref, q_ref, k_ref, kc_ref, v_ref, g_ref, qm_ref, carry_ref):
    @pl.when(pl.program_id(1) == 0)
    def _():
        carry_ref[...] = jnp.zeros_like(carry_ref)

    x = x_ref[0]
    xb = x.astype(BF16)
    tm = x.shape[0]
    z = _dot3(x, wfh_ref[...], wfl_ref[...]) + bf_ref[...]
    log_f = jnp.minimum(z, 0.0) - jnp.log(1.0 + jnp.exp(-jnp.abs(z)))
    tril = (_iota((tm, tm), 1) <= _iota((tm, tm), 0)).astype(BF16)
    c = _dot_sel(tril, log_f) + carry_ref[...]
    carry_ref[...] = c[tm - 1:tm, :]
    c_hi, c_mid, c_lo = _split3(c * LOG2E)
    c_parts = jnp.concatenate([c_hi, c_mid, c_lo], axis=1)
    q_ref[0] = _dot(xb, wq_ref[...]).astype(BF16)
    k_ref[0] = _dot(xb, wk_ref[...]).astype(BF16)
    kc_ref[0] = _dot(c_parts, place_ref[...]).astype(BF16)
    v_ref[0] = _dot(xb, wv_ref[...]).astype(BF16)
    g_ref[0] = _sigmoid(_dot(xb, wg_ref[...])).astype(BF16)
    qm_ref[0] = _dot(xb, wm_ref[...]).astype(BF16)


def _fox_in(x, w_in, b_f, tm):
    B, S, _ = x.shape
    scale = HEAD_DIM ** -0.5
    wq, wk, wv, wg, wf, wm = jnp.split(
        w_in, [TOK_W, 2 * TOK_W, 3 * TOK_W, 4 * TOK_W, 4 * TOK_W + N_TOK_HEADS], axis=1)
    wf_pad = jnp.pad(wf, ((0, 0), (0, LANES - N_TOK_HEADS)))
    wf_hi = wf_pad.astype(BF16)
    wf_lo = (wf_pad - wf_hi.astype(F32)).astype(BF16)
    bf_pad = jnp.pad(b_f, (0, LANES - N_TOK_HEADS)).reshape(1, LANES)
    row = jnp.arange(C_PARTS * LANES)
    col = jnp.arange(N_HEAD_PAIRS * LANES)
    head, part = row[:, None] % LANES, row[:, None] // LANES
    place = ((head // 2 == col[None, :] // LANES)
             & (col[None, :] % LANES == C_PARTS * (head % 2) + part)).astype(BF16)

    row_spec = lambda w: pl.BlockSpec((1, tm, w), lambda b, i: (b, i, 0))
    tok = jax.ShapeDtypeStruct((B, S, TOK_W), BF16)
    return pl.pallas_call(
        _fox_in_kernel,
        grid=(B, S // tm),
        in_specs=[row_spec(D_MODEL),
                  _full_spec((D_MODEL, TOK_W)), _full_spec((D_MODEL, TOK_W)),
                  _full_spec((D_MODEL, TOK_W)), _full_spec((D_MODEL, TOK_W)),
                  _full_spec((D_MODEL, LANES)), _full_spec((D_MODEL, LANES)),
                  _full_spec((D_MODEL, MEM_W)), _full_spec((1, LANES)),
                  _full_spec((C_PARTS * LANES, N_HEAD_PAIRS * LANES))],
        out_specs=[row_spec(TOK_W)] * 5 + [row_spec(MEM_W)],
        out_shape=[tok] * 5 + [jax.ShapeDtypeStruct((B, S, MEM_W), BF16)],
        scratch_shapes=[pltpu.VMEM((1, LANES), F32)],
        compiler_params=_cparams(("arbitrary", "arbitrary")),
        name="fox_in",
    )(x, (wq * (scale * LOG2E)).astype(BF16), wk.astype(BF16), wv.astype(BF16), wg.astype(BF16), wf_hi,
      wf_lo, (wm * scale).astype(BF16), bf_pad, place)


def _fox_attn_kernel(q_ref, k_ref, kc_ref, v_ref, g_ref, o_ref, sa_ref, sb_ref, *, tq):
    qi = pl.program_id(2)
    lane = _iota((tq, LANES), 1)
    q_pair = q_ref[0]
    q_rows = []
    for h in range(2):
        own = jnp.where(_div_pow2(lane, HEAD_DIM) == h, q_pair, jnp.zeros_like(q_pair))
        minus = jnp.where(jnp.logical_and(lane >= C_PARTS * h, lane < C_PARTS * (h + 1)), -1.0, 0.0)
        q_rows.append(jnp.concatenate([own, minus.astype(BF16)], axis=1))
    q_both = jnp.concatenate(q_rows, axis=0)

    def scores(kj, s_ref):
        start = pl.multiple_of(kj * tq, tq)
        k_aug = jnp.concatenate([k_ref[0, pl.ds(start, tq), :], kc_ref[0, pl.ds(start, tq), :]], axis=1)
        s_ref[...] = _dot_nt(q_both, k_aug)

    def absorb(kj, s_ref, carry, masked):
        start = pl.multiple_of(kj * tq, tq)
        v = v_ref[0, pl.ds(start, tq), :]
        out = []
        for h in range(2):
            m, l, acc = carry[h]
            s = s_ref[h * tq:(h + 1) * tq, :]
            if masked:
                s = jnp.where(_iota((tq, tq), 1) <= _iota((tq, tq), 0), s, -jnp.inf)
            m_new = jnp.maximum(m, jnp.max(s, axis=1, keepdims=True))
            p = jnp.exp2(s - m_new)
            alpha = jnp.exp2(m - m_new)
            p_lanes = functools.reduce(
                jnp.add, [p[:, i * LANES:(i + 1) * LANES] for i in range(tq // LANES)])
            l = alpha * l + p_lanes
            acc = alpha * acc + _dot(p.astype(BF16), v)
            out.append((m_new, l, acc))
        return tuple(out)

    def pair(i, carry):
        scores(2 * i + 1, sb_ref)
        carry = absorb(2 * i, sa_ref, carry, masked=False)
        scores(2 * i + 2, sa_ref)
        return absorb(2 * i + 1, sb_ref, carry, masked=False)

    def odd_tail(carry):
        scores(qi, sb_ref)
        carry = absorb(qi - 1, sa_ref, carry, masked=False)
        return absorb(qi, sb_ref, carry, masked=True)

    def even_tail(carry):
        return absorb(qi, sa_ref, carry, masked=True)

    init = (jnp.full((tq, 1), -jnp.inf, F32), jnp.zeros((tq, LANES), F32), jnp.zeros((tq, LANES), F32))
    scores(0, sa_ref)
    carry = lax.fori_loop(0, jnp.right_shift(qi, 1), pair, (init, init))
    (_, l0, acc0), (_, l1, acc1) = lax.cond(jnp.bitwise_and(qi, 1) == 1, odd_tail, even_tail, carry)
    o0 = acc0 / jnp.sum(l0, axis=1, keepdims=True)
    o1 = acc1 / jnp.sum(l1, axis=1, keepdims=True)
    o = jnp.where(lane < HEAD_DIM, o0, o1)
    o_ref[0] = (o * g_ref[0].astype(F32)).astype(BF16)


def _fox_attn(q, k, kc, v, g, tq):
    B, S, _ = v.shape
    tile = pl.BlockSpec((1, tq, LANES), lambda b, p, i: (b, i, p))
    whole = pl.BlockSpec((1, S, LANES), lambda b, p, i: (b, 0, p))
    return pl.pallas_call(
        functools.partial(_fox_attn_kernel, tq=tq),
        grid=(B, N_HEAD_PAIRS, S // tq),
        in_specs=[tile, whole, whole, whole, tile],
        out_specs=tile,
        out_shape=jax.ShapeDtypeStruct((B, S, TOK_W), BF16),
        scratch_shapes=[pltpu.VMEM((2 * tq, tq), F32), pltpu.VMEM((2 * tq, tq), F32)],
        compiler_params=_cparams(("arbitrary", "arbitrary", "arbitrary")),
        name="fox_attn",
    )(q, k, kc, v, g)


def _rwkv_in_kernel(x_ref, w_ref, mu_ref, w0_ref, w2_ref, a0_ref, a2_ref, g2_ref, kk_ref, ka_ref,
                    rk_ref, head_ones_ref,
                    rt_ref, at_ref, bt_ref, kt_ref, bh_ref, kh_ref, v_ref, pc_ref, bonus_ref, g_ref,
                    qm_ref, prev_ref):
    @pl.when(pl.program_id(1) == 0)
    def _():
        prev_ref[...] = jnp.zeros_like(prev_ref)

    tm = x_ref.shape[1]
    h = _dot(x_ref[0].astype(BF16), w_ref[...])
    qm_ref[0] = (h[:, RWKV_SHIFT_W:] * (HEAD_DIM ** -0.5)).astype(BF16)
    hs = h[:, :RWKV_SHIFT_W]
    row = _iota(hs.shape, 0)
    shifted = jnp.where(row == 0, prev_ref[...], pltpu.roll(hs, 1, 0))
    prev_ref[...] = hs[tm - 1:tm, :]
    hs = hs + mu_ref[...] * (shifted - hs)
    r = hs[:, :TOK_W]
    k = hs[:, TOK_W:2 * TOK_W]
    v = hs[:, 2 * TOK_W:3 * TOK_W]
    wa = hs[:, 3 * TOK_W:3 * TOK_W + LANES]
    gd = hs[:, 3 * TOK_W + LANES:]
    w = w0_ref[...] + _dot(jnp.tanh(wa).astype(BF16), w2_ref[...])
    w = -(jnp.maximum(-w, 0.0) + jnp.log(1.0 + jnp.exp(-jnp.abs(w)))) - 0.5
    log_decay = -jnp.exp(w)
    a = _sigmoid(a0_ref[...] + _dot(wa.astype(BF16), a2_ref[...]))
    g_ref[0] = _dot(_sigmoid(gd).astype(BF16), g2_ref[...])
    kk = k * kk_ref[...]
    kk = kk / jnp.maximum(jnp.sqrt(_head_sums(kk * kk, head_ones_ref[...])), 1e-12)
    k = k * (1.0 + (a - 1.0) * ka_ref[...])
    bonus_ref[0] = r * k * rk_ref[...]
    v_ref[0] = v
    a_s = -kk
    b_s = kk * a
    same_chunk = _div_pow2(_iota((tm, tm), 0), CHUNK) == _div_pow2(_iota((tm, tm), 1), CHUNK)
    incl = (same_chunk & (_iota((tm, tm), 1) <= _iota((tm, tm), 0))).astype(BF16)
    d_hi, d_mid, d_lo = _split3(log_decay)
    cw = _dot(incl, d_hi) + _dot(incl, d_mid) + _dot(incl, d_lo)
    same = same_chunk.astype(BF16)
    cw_end = _dot(same, d_hi) + _dot(same, d_mid) + _dot(same, d_lo)
    e_in = jnp.exp(cw)
    e_out = jnp.exp(-cw)
    e_end = jnp.exp(cw_end - cw)
    rt_ref[0] = (r * e_in).astype(BF16)
    at_ref[0] = (a_s * jnp.exp(cw - log_decay)).astype(BF16)
    bt_ref[0] = (b_s * e_out).astype(BF16)
    kt_ref[0] = (k * e_out).astype(BF16)
    bh_ref[0] = (b_s * e_end).astype(BF16)
    kh_ref[0] = (k * e_end).astype(BF16)
    pc_ref[0] = jnp.exp(cw_end)


def _rwkv_in(x, w_in, mu, w0, w2, a0, a2, g2, k_k, k_a, r_k, tm):
    B, S, _ = x.shape
    in_w = w_in.shape[1]
    w2_pad = jnp.concatenate([w2, jnp.zeros((AAA_LORA, TOK_W), F32)], axis=0).astype(BF16)
    a2_pad = jnp.concatenate([jnp.zeros((DECAY_LORA, TOK_W), F32), a2], axis=0).astype(BF16)
    head_ones = _head_blocks(1.0)
    vec = lambda t: t.reshape(1, -1)
    row_spec = lambda w: pl.BlockSpec((1, tm, w), lambda b, i: (b, i, 0))
    tok = jax.ShapeDtypeStruct((B, S, TOK_W), F32)
    return pl.pallas_call(
        _rwkv_in_kernel,
        grid=(B, S // tm),
        in_specs=[row_spec(D_MODEL), _full_spec((D_MODEL, in_w)), _full_spec((1, RWKV_SHIFT_W)),
                  _full_spec((1, TOK_W)), _full_spec((LANES, TOK_W)), _full_spec((1, TOK_W)),
                  _full_spec((LANES, TOK_W)), _full_spec((GATE_LORA, TOK_W)), _full_spec((1, TOK_W)),
                  _full_spec((1, TOK_W)), _full_spec((1, TOK_W)), _full_spec((MXU_W, MXU_W))],
        out_specs=[row_spec(TOK_W)] * 10 + [row_spec(MEM_W)],
        out_shape=[jax.ShapeDtypeStruct((B, S, TOK_W), BF16)] * 6 + [tok] * 4
        + [jax.ShapeDtypeStruct((B, S, MEM_W), BF16)],
        scratch_shapes=[pltpu.VMEM((1, RWKV_SHIFT_W), F32)],
        compiler_params=_cparams(("arbitrary", "arbitrary")),
        name="rwkv_in",
    )(x, w_in.astype(BF16), vec(mu), vec(w0), w2_pad, vec(a0), a2_pad, g2.astype(BF16), vec(k_k),
      vec(k_a), vec(r_k), head_ones)


def _rwkv_scan_kernel(rt_ref, at_ref, bt_ref, kt_ref, bh_ref, kh_ref, v_ref, pc_ref, y_ref, state_ref,
                      *, n_chunks):
    @pl.when(pl.program_id(2) == 0)
    def _():
        state_ref[...] = jnp.zeros_like(state_ref)

    C = CHUNK
    lane_head = _div_pow2(_iota((2 * C, LANES), 1), HEAD_DIM)
    row_head = _div_pow2(_iota((2 * C, LANES), 0), C)
    keep = lane_head == row_head
    pos_r = _mod_pow2(_iota((2 * C, 2 * C), 0), C)
    pos_c = _mod_pow2(_iota((2 * C, 2 * C), 1), C)
    strict = pos_c < pos_r
    incl = pos_c <= pos_r
    eye = (_iota((2 * C, 2 * C), 0) == _iota((2 * C, 2 * C), 1)).astype(F32)

    def stack(ref, item):
        tile, c = item
        t = ref[0, c * C:(c + 1) * C, tile * LANES:(tile + 1) * LANES]
        return jnp.where(keep, jnp.concatenate([t, t], axis=0), 0.0).astype(BF16)

    items = [(tile, c) for c in range(n_chunks) for tile in range(SCAN_TILES)]
    chunks = range(len(items))
    a2 = [stack(at_ref, it) for it in items]
    r2 = [stack(rt_ref, it) for it in items]
    bh2 = [stack(bh_ref, it) for it in items]
    kh2 = [stack(kh_ref, it) for it in items]
    v2 = [stack(v_ref, it) for it in items]
    gram = [_dot_nt(jnp.concatenate([a2[c], r2[c]], axis=0),
                    jnp.concatenate([stack(bt_ref, items[c]), stack(kt_ref, items[c])], axis=0))
            for c in chunks]
    power = [jnp.where(strict, gram[c][:2 * C, :2 * C], 0.0) for c in chunks]
    inv = [eye + power[c] for c in chunks]
    for _ in range(int(math.log2(C)) - 1):
        power = [_dot(p.astype(BF16), p.astype(BF16)) for p in power]
        inv = [inv[c] + _dot(inv[c].astype(BF16), power[c].astype(BF16)) for c in chunks]
    from_v = [_dot(jnp.concatenate([jnp.where(strict, gram[c][:2 * C, 2 * C:], 0.0),
                                    jnp.where(incl, gram[c][2 * C:, 2 * C:], 0.0)], axis=0).astype(BF16),
                   v2[c]) for c in chunks]
    wu = [_dot(inv[c].astype(BF16),
               jnp.concatenate([a2[c], from_v[c][:2 * C].astype(BF16)], axis=1)) for c in chunks]
    wy = [_dot(jnp.where(incl, gram[c][2 * C:, :2 * C], 0.0).astype(BF16), wu[c].astype(BF16))
          for c in chunks]
    w_y = [(r2[c].astype(F32) + wy[c][:, :LANES]).astype(BF16) for c in chunks]
    y0 = [from_v[c][2 * C:] + wy[c][:, LANES:] for c in chunks]
    gc = [_dot(wu[c].T.astype(BF16), bh2[c]) for c in chunks]
    c0 = [gc[c][LANES:] + _dot(v2[c].astype(F32).T.astype(BF16), kh2[c]) for c in chunks]
    states = [state_ref[tile] for tile in range(SCAN_TILES)]
    for i, (tile, c) in enumerate(items):
        lanes = slice(tile * LANES, (tile + 1) * LANES)
        sb = states[tile].astype(BF16)
        y2 = _dot_nt(w_y[i], sb) + y0[i]
        y_ref[0, c * C:(c + 1) * C, lanes] = y2[:C] + y2[C:]
        states[tile] = (states[tile] * pc_ref[0, c * C:c * C + 1, lanes]
                        + _dot(sb, gc[i][:LANES].astype(BF16)) + c0[i])
    for tile in range(SCAN_TILES):
        state_ref[tile] = states[tile]


def _rwkv_scan(rt, at, bt, kt, bh, kh, v, pc, rows):
    B, S, _ = v.shape
    spec = pl.BlockSpec((1, rows, SCAN_TILES * LANES), lambda b, p, i: (b, i, p))
    return pl.pallas_call(
        functools.partial(_rwkv_scan_kernel, n_chunks=rows // CHUNK),
        grid=(B, N_HEAD_PAIRS // SCAN_TILES, S // rows),
        in_specs=[spec] * 8,
        out_specs=spec,
        out_shape=jax.ShapeDtypeStruct((B, S, TOK_W), F32),
        scratch_shapes=[pltpu.VMEM((SCAN_TILES, LANES, LANES), F32)],
        compiler_params=_cparams(("arbitrary", "arbitrary", "arbitrary")),
        name="rwkv_scan",
    )(rt, at, bt, kt, bh, kh, v, pc)


def _rwkv_post_kernel(y_ref, bonus_ref, v_ref, g_ref, lg_ref, lb_ref, head_mean_ref, o_ref):
    y = y_ref[0]
    head_mean = functools.partial(_head_sums, blocks=head_mean_ref[...])

    yc = y - head_mean(y)
    var = head_mean(yc * yc)
    yn = yc * lax.rsqrt(var + GN_EPS) * lg_ref[...] + lb_ref[...]
    bonus = head_mean(bonus_ref[0]) * float(HEAD_DIM)
    o_ref[0] = ((yn + bonus * v_ref[0]) * g_ref[0]).astype(BF16)


def _rwkv_post(y, bonus, v, g, lnx_g, lnx_b, tm):
    B, S, _ = y.shape
    head_mean = _head_blocks(1.0 / HEAD_DIM)
    row_spec = pl.BlockSpec((1, tm, TOK_W), lambda b, i: (b, i, 0))
    return pl.pallas_call(
        _rwkv_post_kernel,
        grid=(B, S // tm),
        in_specs=[row_spec] * 4 + [_full_spec((1, TOK_W)), _full_spec((1, TOK_W)),
                                   _full_spec((MXU_W, MXU_W))],
        out_specs=row_spec,
        out_shape=jax.ShapeDtypeStruct((B, S, TOK_W), BF16),
        compiler_params=_cparams(("arbitrary", "arbitrary")),
        name="rwkv_post",
    )(y, bonus, v, g, lnx_g.reshape(1, -1), lnx_b.reshape(1, -1), head_mean)


def _mix_out_kernel(tok_ref, qm_ref, km_ref, vm_ref, x_ref, wo_tok_ref, wo_mem_ref, g_ref, b_ref,
                    wr_hi_ref, wr_lo_ref, rb_ref, x1_ref, x1b_ref, gate_ref):
    tm = x_ref.shape[1]
    groups = [slice(r, r + tm // MIX_GROUPS) for r in range(0, tm, tm // MIX_GROUPS)]
    mem_out = [None] * MIX_GROUPS
    for h in range(N_MEM_HEADS):
        for i, rows in enumerate(groups):
            s = _dot_nt(qm_ref[0, rows, :], km_ref[0, h])
            e = jnp.exp(s - jnp.max(s, axis=1, keepdims=True))
            o = _dot(e.astype(BF16), vm_ref[0, h]) / jnp.sum(e, axis=1, keepdims=True)
            mem_out[i] = o if mem_out[i] is None else mem_out[i] + o
    x1 = []
    for i, rows in enumerate(groups):
        mixed = _dot(tok_ref[0, rows, :], wo_tok_ref[...]) + _dot(mem_out[i].astype(BF16), wo_mem_ref[...])
        x1.append(_layer_norm(DEEPNORM_ALPHA * x_ref[0, rows, :] + mixed, g_ref[...], b_ref[...]))
        x1_ref[0, rows, :] = x1[i]
        x1b_ref[0, rows, :] = x1[i].astype(BF16)
    scores = [_sigmoid(_dot3(x1[i], wr_hi_ref[...], wr_lo_ref[...])) for i in range(MIX_GROUPS)]
    lane = _iota(scores[0].shape, 1)
    lane_f = lane.astype(F32)
    cand = [jnp.where(lane < N_EXPERTS, sc + rb_ref[...], -jnp.inf) for sc in scores]
    picked = [jnp.zeros(scores[0].shape, jnp.bool_)] * MIX_GROUPS
    for _ in range(TOP_K):
        for i in range(MIX_GROUPS):
            best = jnp.max(cand[i], axis=1, keepdims=True)
            first = jnp.min(jnp.where(cand[i] == best, lane_f, float(LANES)), axis=1, keepdims=True)
            hit = lane_f == first
            picked[i] = jnp.logical_or(picked[i], hit)
            cand[i] = jnp.where(hit, -jnp.inf, cand[i])
    for i, rows in enumerate(groups):
        chosen = jnp.where(picked[i], scores[i], 0.0)
        gate = chosen / jnp.sum(chosen, axis=1, keepdims=True) * ROUTED_SCALE
        gate_ref[0, rows, :] = jnp.where(lane == N_EXPERTS, 1.0, gate)


def _mix_out(tok, qm, km, vm, x, w_out, ln_g, ln_b, w_router, router_bias, tm):
    B, S, _ = x.shape
    n_mem = km.shape[2]
    wr = jnp.pad(w_router, ((0, 0), (0, LANES - N_EXPERTS)))
    wr_hi = wr.astype(BF16)
    wr_lo = (wr - wr_hi.astype(F32)).astype(BF16)
    rb = jnp.pad(router_bias, (0, LANES - N_EXPERTS)).reshape(1, LANES)
    row_spec = lambda w: pl.BlockSpec((1, tm, w), lambda b, i: (b, i, 0))
    mem_spec = pl.BlockSpec((1, N_MEM_HEADS, n_mem, MEM_W), lambda b, i: (b, 0, 0, 0))
    return pl.pallas_call(
        _mix_out_kernel,
        grid=(B, S // tm),
        in_specs=[row_spec(TOK_W), row_spec(MEM_W), mem_spec, mem_spec, row_spec(D_MODEL),
                  _full_spec((TOK_W, D_MODEL)), _full_spec((MEM_W, D_MODEL)),
                  _full_spec((1, D_MODEL)), _full_spec((1, D_MODEL)),
                  _full_spec((D_MODEL, LANES)), _full_spec((D_MODEL, LANES)), _full_spec((1, LANES))],
        out_specs=[row_spec(D_MODEL), row_spec(D_MODEL), row_spec(LANES)],
        out_shape=[jax.ShapeDtypeStruct((B, S, D_MODEL), F32), jax.ShapeDtypeStruct((B, S, D_MODEL), BF16),
                   jax.ShapeDtypeStruct((B, S, LANES), F32)],
        compiler_params=_cparams(("arbitrary", "arbitrary")),
        name="mix_out",
    )(tok, qm, km, vm, x, w_out[:TOK_W].astype(BF16), w_out[TOK_W:].astype(BF16),
      ln_g.reshape(1, -1), ln_b.reshape(1, -1), wr_hi, wr_lo, rb)


MOE_TM = 512
UNIT = 16
SLAB_ROWS = 4096
SLAB_UNITS = SLAB_ROWS // UNIT
BLK_UNITS = 64
BLK_ROWS = BLK_UNITS * UNIT
XS_W = D_MODEL + 2 * LANES
ROW_CHUNK = 1024
BF16_EXACT_INT = 256
assert TOP_K * MOE_TM + N_EXPERTS * (UNIT - 1) <= SLAB_ROWS - UNIT
ZERO_UNIT = SLAB_UNITS - 1


def _dot_x_sel(x, sel):
    hi, mid, lo = _split3(x)
    return _dot(hi, sel) + _dot(mid, sel) + _dot(lo, sel)


def _route_tile(gates):
    tm = gates.shape[0]
    sel = jnp.logical_and(gates > 0.0, _iota(gates.shape, 1) < N_EXPERTS)
    sel_b = jnp.where(sel, 1.0, 0.0).astype(BF16)
    earlier = (_iota((tm, tm), 1) < _iota((tm, tm), 0)).astype(BF16)
    rank1 = jnp.where(sel, _dot(earlier, sel_b) + 1.0, 0.0)
    count = _dot(jnp.ones((8, tm), BF16), sel_b)[0:1]
    n16 = jnp.floor((count + float(UNIT - 1)) * (1.0 / UNIT))
    before = (_iota((LANES, LANES), 0) < _iota((LANES, LANES), 1)).astype(BF16)
    off16 = _dot(jnp.broadcast_to(n16, (8, LANES)).astype(BF16), before)[0:1]
    r_hi, r_lo = _split2(rank1)
    return r_hi, r_lo, jnp.max(count) > float(BF16_EXACT_INT), n16, off16


def _slab_experts(n16, off16, r0, rows):
    r = (_iota((rows, LANES), 0) + r0).astype(F32)
    lo = off16 * float(UNIT)
    return jnp.logical_and(r >= lo, r < lo + n16 * float(UNIT))


def _moe_dispatch_kernel(xb_ref, gate_ref, xs_ref, n_ref):
    gates = gate_ref[...]
    r_hi, r_lo, has_lo, n16, off16 = _route_tile(gates)
    n_ref[0] = jnp.broadcast_to(n16, (8, LANES))
    g_hi, g_lo = _split2(gates)
    src = jnp.concatenate([xb_ref[...], g_hi, g_lo], axis=1)

    def fill_slab(with_lo):
        for c in range(SLAB_ROWS // ROW_CHUNK):
            r0 = c * ROW_CHUNK
            seg = _slab_experts(n16, off16, r0, ROW_CHUNK)
            seg_b = jnp.where(seg, 1.0, 0.0).astype(BF16)
            rank_at = _dot_nt(seg_b, r_hi)
            if with_lo:
                rank_at = rank_at + _dot_nt(seg_b, r_lo)
            r = (_iota((ROW_CHUNK, 1), 0) + r0).astype(F32)
            pos1 = r + 1.0 - jnp.sum(jnp.where(seg, off16 * float(UNIT), 0.0), axis=1, keepdims=True)
            pick = jnp.where(rank_at == pos1, 1.0, 0.0).astype(BF16)
            xs_ref[r0:r0 + ROW_CHUNK, :] = _dot(pick, src).astype(BF16)

    lax.cond(has_lo, functools.partial(fill_slab, True), functools.partial(fill_slab, False))


def _moe_dispatch(xb, gates):
    T = xb.shape[0]
    nt = T // MOE_TM
    return pl.pallas_call(
        _moe_dispatch_kernel,
        grid=(nt,),
        in_specs=[pl.BlockSpec((MOE_TM, D_MODEL), lambda i: (i, 0)),
                  pl.BlockSpec((MOE_TM, LANES), lambda i: (i, 0))],
        out_specs=[pl.BlockSpec((SLAB_ROWS, XS_W), lambda i: (i, 0)),
                   pl.BlockSpec((1, 8, LANES), lambda i: (i, 0, 0))],
        out_shape=[jax.ShapeDtypeStruct((nt * SLAB_ROWS, XS_W), BF16),
                   jax.ShapeDtypeStruct((nt, 8, LANES), F32)],
        compiler_params=_cparams(("arbitrary",)),
        name="moe_dispatch",
    )(xb, gates)


def _moe_plan_kernel(n_ref, unit_ref, expert_ref, count_ref, *, nt, nb):
    n16 = n_ref[...]
    lane = _iota((nt, LANES), 1)
    before = (_iota((LANES, LANES), 0) < _iota((LANES, LANES), 1)).astype(BF16)
    off16 = _dot(n16.astype(BF16), before)
    used = jnp.sum(n16, axis=1, keepdims=True)
    n16 = n16 + jnp.where(lane == N_EXPERTS - 1, float(SLAB_UNITS) - used, 0.0)
    earlier = (_iota((nt, nt), 1) < _iota((nt, nt), 0)).astype(BF16)
    cum_ex = _dot_sel(earlier, n16)
    cum_in = cum_ex + n16
    total = cum_in[nt - 1:nt, :]
    n_blk = jnp.floor((total + float(BLK_UNITS - 1)) * (1.0 / BLK_UNITS))
    blk_start = _dot_x_sel(jnp.broadcast_to(n_blk, (8, LANES)), before)[0:1]
    blk_end = blk_start + n_blk
    count_ref[...] = jnp.broadcast_to(jnp.sum(n_blk, axis=1, keepdims=True), (8, LANES)).astype(jnp.int32)

    lane_b = _iota((nb, LANES), 1)
    blk = _iota((nb, LANES), 0).astype(F32)
    expert = jnp.sum(jnp.where(jnp.logical_and(lane_b < N_EXPERTS, blk_end <= blk), 1.0, 0.0),
                     axis=1, keepdims=True)
    live = expert < float(N_EXPERTS)
    expert = jnp.minimum(expert, float(N_EXPERTS - 1))
    expert_ref[...] = jnp.broadcast_to(expert, (nb, LANES)).astype(jnp.int32)
    mine = lane_b.astype(F32) == expert
    mine_b = jnp.where(mine, 1.0, 0.0).astype(BF16)
    my_start = jnp.sum(jnp.where(mine, blk_start, 0.0), axis=1, keepdims=True)
    my_total = jnp.sum(jnp.where(mine, total, 0.0), axis=1, keepdims=True)
    q = (blk - my_start) * float(BLK_UNITS) + lane_b.astype(F32)
    valid = jnp.logical_and(jnp.logical_and(q < my_total, lane_b < BLK_UNITS), live)

    def per_slab(table):
        hi, mid, lo = _split3(table)
        return _dot_nt(mine_b, hi) + _dot_nt(mine_b, mid) + _dot_nt(mine_b, lo)

    ex, inc, first = per_slab(cum_ex), per_slab(cum_in), per_slab(off16)
    base = jnp.zeros((nb, LANES), F32)
    for i in range(nt):
        hit = jnp.logical_and(q >= ex[:, i:i + 1], q < inc[:, i:i + 1])
        base = base + jnp.where(hit, float(i * SLAB_UNITS) + first[:, i:i + 1] - ex[:, i:i + 1], 0.0)
    unit_ref[...] = jnp.where(valid, base + q, -1.0).astype(jnp.int32)


def _moe_plan(n16, nb):
    nt = n16.shape[0]
    tbl = jax.ShapeDtypeStruct((nb, LANES), jnp.int32)
    units, experts, count = pl.pallas_call(
        functools.partial(_moe_plan_kernel, nt=nt, nb=nb),
        out_shape=[tbl, tbl, jax.ShapeDtypeStruct((8, LANES), jnp.int32)],
        compiler_params=pltpu.CompilerParams(vmem_limit_bytes=VMEM_LIMIT),
        name="moe_plan",
    )(n16)
    return units[:, :BLK_UNITS].reshape(-1), experts[:, 0], count[0, :1]


def _moe_ffn_kernel(unit_ref, expert_ref, count_ref, xs_hbm, wg_ref, wu_ref, wd_ref, ys_hbm,
                    ibuf, obuf, in_sem, out_sem, *, nb):
    b = pl.program_id(0)
    n_live = count_ref[0]
    slot = jnp.bitwise_and(b, 1)

    def in_copy(blk, buf, s):
        u = unit_ref[blk * BLK_UNITS + s]
        u = jnp.where(u < 0, ZERO_UNIT, u)
        return pltpu.make_async_copy(xs_hbm.at[pl.ds(pl.multiple_of(u * UNIT, UNIT), UNIT), :],
                                     ibuf.at[buf, pl.ds(s * UNIT, UNIT), :], in_sem.at[buf])

    def out_copy(blk, buf, s):
        u = unit_ref[blk * BLK_UNITS + s]
        return u, pltpu.make_async_copy(
            obuf.at[buf, pl.ds(s * UNIT, UNIT), :],
            ys_hbm.at[pl.ds(pl.multiple_of(jnp.maximum(u, 0) * UNIT, UNIT), UNIT), :], out_sem.at[buf])

    def whole_block(blk):
        return unit_ref[blk * BLK_UNITS + BLK_UNITS - 1] >= 0

    def for_each_used_out(blk, buf, act):
        for s in range(BLK_UNITS):
            u, copy = out_copy(blk, buf, s)
            pl.when(u >= 0)(functools.partial(act, copy))

    def wait_out(blk, buf):
        @pl.when(whole_block(blk))
        def _():
            pltpu.make_async_copy(obuf.at[buf], ys_hbm.at[pl.ds(0, BLK_ROWS), :], out_sem.at[buf]).wait()

        @pl.when(jnp.logical_not(whole_block(blk)))
        def _():
            for_each_used_out(blk, buf, lambda c: c.wait())

    def start_in(blk, buf):
        for s in range(BLK_UNITS):
            in_copy(blk, buf, s).start()

    @pl.when(b == 0)
    def _():
        start_in(0, 0)

    @pl.when(b <= n_live)
    def _():
        pltpu.make_async_copy(xs_hbm.at[pl.ds(0, BLK_ROWS), :], ibuf.at[slot], in_sem.at[slot]).wait()

    @pl.when(jnp.logical_and(b >= 2, b - 2 < n_live))
    def _():
        wait_out(b - 2, slot)

    def expert_rows(rows):
        x = rows[:, :D_MODEL]
        gates = rows[:, D_MODEL:D_MODEL + LANES].astype(F32) + rows[:, D_MODEL + LANES:].astype(F32)
        gate = jnp.sum(jnp.where(_iota(gates.shape, 1) == expert_ref[b], gates, 0.0), axis=1, keepdims=True)
        hg = _dot(x, wg_ref[0, 0].astype(BF16))
        hu = _dot(x, wu_ref[0, 0].astype(BF16))
        hidden = hg * _sigmoid(hg) * hu * gate
        return _dot(hidden.astype(BF16), wd_ref[0, 0].astype(BF16)).astype(BF16)

    live = b < n_live

    @pl.when(jnp.logical_and(live, whole_block(b)))
    def _():
        start_in(b + 1, 1 - slot)
        half = BLK_UNITS // 2
        for part in range(2):
            rows = pl.ds(part * half * UNIT, half * UNIT)
            obuf[slot, rows, :] = expert_rows(ibuf[slot, rows, :])
            for s in range(part * half, (part + 1) * half):
                out_copy(b, slot, s)[1].start()

    @pl.when(jnp.logical_and(live, jnp.logical_not(whole_block(b))))
    def _():
        start_in(b + 1, 1 - slot)
        obuf[slot] = expert_rows(ibuf[slot])
        for_each_used_out(b, slot, lambda c: c.start())

    @pl.when(jnp.logical_and(b == nb - 1, nb - 2 < n_live))
    def _():
        wait_out(nb - 2, 1 - slot)


def _moe_ffn(xs, units, experts, count, w_gate, w_up, w_down, layer, nb):
    n_rows = xs.shape[0]
    w_spec = lambda r, c: pl.BlockSpec((1, 1, r, c), lambda b, u, e, n: (layer, e[b], 0, 0))
    grid_spec = pltpu.PrefetchScalarGridSpec(
        num_scalar_prefetch=3,
        grid=(nb,),
        in_specs=[pl.BlockSpec(memory_space=pl.ANY),
                  w_spec(D_MODEL, D_EXPERT), w_spec(D_MODEL, D_EXPERT), w_spec(D_EXPERT, D_MODEL)],
        out_specs=pl.BlockSpec(memory_space=pl.ANY),
        scratch_shapes=[pltpu.VMEM((2, BLK_ROWS, XS_W), BF16), pltpu.VMEM((2, BLK_ROWS, D_MODEL), BF16),
                        pltpu.SemaphoreType.DMA((2,)), pltpu.SemaphoreType.DMA((2,))],
    )
    return pl.pallas_call(
        functools.partial(_moe_ffn_kernel, nb=nb),
        grid_spec=grid_spec,
        out_shape=jax.ShapeDtypeStruct((n_rows, D_MODEL), BF16),
        compiler_params=_cparams(("arbitrary",)),
        name="moe_ffn",
    )(units, experts, count, xs, w_gate, w_up, w_down)


def _moe_combine_kernel(ys_ref, gate_ref, x_ref, xb_ref, wsg_ref, wsu_ref, wsd_ref, g_ref, b_ref, o_ref):
    r_hi, r_lo, has_lo, n16, off16 = _route_tile(gate_ref[...])
    off_b = jnp.broadcast_to(off16, (8, LANES)).astype(BF16)

    def finish(with_lo):
        xb = xb_ref[...]
        hg = _dot(xb, wsg_ref[...])
        acc = _dot((hg * _sigmoid(hg) * _dot(xb, wsu_ref[...])).astype(BF16), wsd_ref[...])
        for c in range(SLAB_ROWS // ROW_CHUNK):
            r0 = c * ROW_CHUNK
            seg_b = jnp.where(_slab_experts(n16, off16, r0, ROW_CHUNK), 1.0, 0.0).astype(BF16)
            rank_at = _dot_nt(r_hi, seg_b)
            if with_lo:
                rank_at = rank_at + _dot_nt(r_lo, seg_b)
            r = (_iota((1, ROW_CHUNK), 1) + r0).astype(F32)
            pos1 = r + 1.0 - _dot_nt(off_b, seg_b)[0:1] * float(UNIT)
            pick = jnp.where(rank_at == pos1, 1.0, 0.0).astype(BF16)
            acc = acc + _dot(pick, ys_ref[r0:r0 + ROW_CHUNK, :])
        o_ref[...] = _layer_norm(DEEPNORM_ALPHA * x_ref[...] + acc, g_ref[...], b_ref[...])

    lax.cond(has_lo, functools.partial(finish, True), functools.partial(finish, False))


def _moe_combine(ys, gates, x, xb, w_sh_gate, w_sh_up, w_sh_down, ln_g, ln_b):
    T = x.shape[0]
    row_spec = lambda w: pl.BlockSpec((MOE_TM, w), lambda i: (i, 0))
    return pl.pallas_call(
        _moe_combine_kernel,
        grid=(T // MOE_TM,),
        in_specs=[pl.BlockSpec((SLAB_ROWS, D_MODEL), lambda i: (i, 0)),
                  row_spec(LANES), row_spec(D_MODEL), row_spec(D_MODEL),
                  _full_spec((D_MODEL, D_EXPERT)), _full_spec((D_MODEL, D_EXPERT)),
                  _full_spec((D_EXPERT, D_MODEL)), _full_spec((1, D_MODEL)), _full_spec((1, D_MODEL))],
        out_specs=row_spec(D_MODEL),
        out_shape=jax.ShapeDtypeStruct((T, D_MODEL), F32),
        compiler_params=_cparams(("arbitrary",)),
        name="moe_combine",
    )(ys, gates, x, xb, w_sh_gate.astype(BF16), w_sh_up.astype(BF16), w_sh_down.astype(BF16),
      ln_g.reshape(1, -1), ln_b.reshape(1, -1))


def _moe(xb, x, gates, w_gate, w_up, w_down, layer, w_sh_gate, w_sh_up, w_sh_down, ln_g, ln_b):
    T = x.shape[0]
    nt = T // MOE_TM
    nb = nt * SLAB_UNITS // BLK_UNITS + N_EXPERTS + 1
    xs, n16 = _moe_dispatch(xb, gates)
    units, experts, count = _moe_plan(n16[:, 0, :], nb)
    ys = _moe_ffn(xs, units, experts, count, w_gate, w_up, w_down, layer, nb)
    return _moe_combine(ys, gates, x, xb, w_sh_gate, w_sh_up, w_sh_down, ln_g, ln_b)


def _tile(n, want):
    t = min(n, want)
    assert n % t == 0, (n, t)
    return t


def kernel(x, mem, mem_ln_g, mem_ln_b, w_mem_kv, fox_w_in, fox_b_f, rwkv_w_in, rwkv_mu, rwkv_w0, rwkv_w2,
           rwkv_a0, rwkv_a2, rwkv_g2, rwkv_k_k, rwkv_k_a, rwkv_r_k, rwkv_lnx_g, rwkv_lnx_b, w_out, ln1_g,
           ln1_b, w_router, router_bias, w_exp_gate, w_exp_up, w_exp_down, w_sh_gate, w_sh_up, w_sh_down,
           ln2_g, ln2_b):
    B, S, D = x.shape
    T = B * S
    assert D == D_MODEL and S % CHUNK == 0 and T % MOE_TM == 0
    t_proj = _tile(S, 512)
    t_attn = _tile(S, 512)
    t_rwkv = _tile(S, 256)
    t_scan = _tile(S, 512)

    km, vm = _mem_kv(mem, mem_ln_g, mem_ln_b, w_mem_kv)
    for i in range(DEPTH):
        j = i // 2
        if i % 2 == 0:
            q, k, kc, v, g, qm = _fox_in(x, fox_w_in[j], fox_b_f[j], t_proj)
            tok = _fox_attn(q, k, kc, v, g, t_attn)
        else:
            (rt, at, bt, kt, bh, kh, v, pc, bonus, g, qm) = _rwkv_in(
                x, rwkv_w_in[j], rwkv_mu[j], rwkv_w0[j], rwkv_w2[j], rwkv_a0[j], rwkv_a2[j], rwkv_g2[j],
                rwkv_k_k[j], rwkv_k_a[j], rwkv_r_k[j].reshape(-1), t_rwkv)
            y = _rwkv_scan(rt, at, bt, kt, bh, kh, v, pc, t_scan)
            tok = _rwkv_post(y, bonus, v, g, rwkv_lnx_g[j], rwkv_lnx_b[j], t_proj)
        x1, x1b, gates = _mix_out(tok, qm, km, vm, x, w_out[i], ln1_g[i], ln1_b[i], w_router[i],
                                  router_bias[i], t_proj)
        x = _moe(x1b.reshape(T, D), x1.reshape(T, D), gates.reshape(T, LANES), w_exp_gate, w_exp_up,
                 w_exp_down, i, w_sh_gate[i], w_sh_up[i], w_sh_down[i], ln2_g[i], ln2_b[i]).reshape(B, S, D)
    return x
```
